```python
import math
import jax, jax.numpy as jnp
from jax import lax
import numpy as np


D_MODEL = 1024
BATCH = 16
SEQ = 2048
DEPTH = 4

N_MIXERS = 2
N_META = 16
N_CONV_LAYERS = (DEPTH + N_MIXERS - 1) // N_MIXERS
N_SSM_LAYERS = DEPTH // N_MIXERS
CONV_KERNEL = 31
CONV_INNER = D_MODEL
SSM_EXPAND = 2
D_INNER = SSM_EXPAND * D_MODEL
SSM_HEAD_DIM = 64
SSM_HEADS = D_INNER // SSM_HEAD_DIM
SSM_GROUPS = 8
SSM_HPG = SSM_HEADS // SSM_GROUPS
SSM_STATE = 128
SSM_CONV = 4
SSM_CHUNK = 128
SSM_GN = SSM_GROUPS * SSM_STATE
SSM_CONV_DIM = D_INNER + 2 * SSM_GN
SSM_IN_DIM = D_INNER + SSM_CONV_DIM + SSM_HEADS
N_EXPERTS = 32
TOP_K = 4
D_FF = D_MODEL
SWIGLU_ALPHA = 1.702
SWIGLU_LIMIT = 7.0
MOE_BLOCK = 128
DEEPNORM_ALPHA = (2 * DEPTH) ** 0.25
DEEPNORM_BETA = (8 * DEPTH) ** -0.25
LN_EPS = 1e-5
RMS_EPS = 1e-5

kernel_name = 'hybrid_conformer_mamba2_moe_deepnorm'


def layer_norm(x, g, b):
    xf = x.astype(jnp.float32)
    mu = jnp.mean(xf, axis=-1, keepdims=True)
    xc = xf - mu
    var = jnp.mean(xc * xc, axis=-1, keepdims=True)
    return (xc * lax.rsqrt(var + LN_EPS) * g + b).astype(x.dtype)


def causal_depthwise_conv(u, w, b):
    k = w.shape[0]
    y = lax.conv_general_dilated(u, w[:, None, :], window_strides=(1,), padding=[(k - 1, 0)],
                                 dimension_numbers=('NWC', 'WIO', 'NWC'),
                                 feature_group_count=u.shape[-1])
    return y + b


def conformer_conv_mixer(h, w1, b1, dw, dwb, ln_g, ln_b, w2, b2):
    u = h @ w1 + b1
    u = u[..., :CONV_INNER] * jax.nn.sigmoid(u[..., CONV_INNER:])
    u = causal_depthwise_conv(u, dw, dwb)
    u = jax.nn.silu(layer_norm(u, ln_g, ln_b))
    return u @ w2 + b2


def mamba2_ssd_mixer(h, w_in, conv_w, conv_b, dt_bias, a_log, d_skip, norm_g, w_out):
    bsz, seq_len, _ = h.shape
    f32 = jnp.float32
    proj = h @ w_in
    z = proj[..., :D_INNER]
    xbc = proj[..., D_INNER:D_INNER + SSM_CONV_DIM]
    dt_raw = proj[..., D_INNER + SSM_CONV_DIM:]
    xbc = jax.nn.silu(causal_depthwise_conv(xbc, conv_w, conv_b))
    pad = SSM_CHUNK - N_META
    xbc = jnp.pad(xbc, ((0, 0), (pad, 0), (0, 0)))
    dt = jax.nn.softplus(dt_raw.astype(f32) + dt_bias.astype(f32))
    dt = jnp.pad(dt, ((0, 0), (pad, 0), (0, 0)))
    n_chunks = (seq_len + pad) // SSM_CHUNK
    cs = (bsz, n_chunks, SSM_CHUNK)
    xs = xbc[..., :D_INNER].reshape(*cs, SSM_GROUPS, SSM_HPG, SSM_HEAD_DIM).astype(f32)
    bm = xbc[..., D_INNER:D_INNER + SSM_GN].reshape(*cs, SSM_GROUPS, SSM_STATE).astype(f32)
    cm = xbc[..., D_INNER + SSM_GN:].reshape(*cs, SSM_GROUPS, SSM_STATE).astype(f32)
    dt = dt.reshape(*cs, SSM_GROUPS, SSM_HPG)
    a = -jnp.exp(a_log.astype(f32)).reshape(SSM_GROUPS, SSM_HPG)
    a_cs = jnp.cumsum(dt * a, axis=2)
    x_dt = xs * dt[..., None]
    seg = a_cs[:, :, :, None] - a_cs[:, :, None, :]
    causal = jnp.tril(jnp.ones((SSM_CHUNK, SSM_CHUNK), dtype=bool))[:, :, None, None]
    decay = jnp.exp(jnp.where(causal, seg, -jnp.inf))
    cb = jnp.einsum('bclgn,bcsgn->bclsg', cm, bm)
    y_diag = jnp.einsum('bclsgr,bcsgrp->bclgrp', cb[..., None] * decay, x_dt)
    a_last = a_cs[:, :, -1:]
    states = jnp.einsum('bclgn,bclgrp->bcgrpn', bm, x_dt * jnp.exp(a_last - a_cs)[..., None])
    chunk_decay = jnp.exp(a_last[:, :, 0])

    def step(carry, inp):
        st, dec = inp
        return carry * dec[..., None, None] + st, carry

    init = jnp.zeros((bsz, SSM_GROUPS, SSM_HPG, SSM_HEAD_DIM, SSM_STATE), f32)
    _, states_in = lax.scan(step, init, (jnp.moveaxis(states, 1, 0), jnp.moveaxis(chunk_decay, 1, 0)))
    states_in = jnp.moveaxis(states_in, 0, 1)
    y_off = jnp.einsum('bclgn,bcgrpn->bclgrp', cm, states_in) * jnp.exp(a_cs)[..., None]
    y = y_diag + y_off + d_skip.astype(f32).reshape(SSM_GROUPS, SSM_HPG)[:, :, None] * xs
    y = y.reshape(bsz, seq_len + pad, D_INNER)[:, pad:]
    y = y * jax.nn.silu(z.astype(f32))
    yg = y.reshape(bsz, seq_len, SSM_GROUPS, D_INNER // SSM_GROUPS)
    yg = yg * lax.rsqrt(jnp.mean(yg * yg, axis=-1, keepdims=True) + RMS_EPS)
    y = yg.reshape(bsz, seq_len, D_INNER) * norm_g
    return y.astype(h.dtype) @ w_out


def clamped_swiglu(hdn):
    x_glu = jnp.minimum(hdn[..., ::2], SWIGLU_LIMIT)
    x_lin = jnp.clip(hdn[..., 1::2], -SWIGLU_LIMIT, SWIGLU_LIMIT)
    return x_glu * jax.nn.sigmoid(SWIGLU_ALPHA * x_glu) * (x_lin + 1.0)


def moe_ffn(h, w_r, b_r, w1, b1, w2, b2):
    bsz, seq_len, d = h.shape
    t = h.reshape(-1, d)
    n_tok = t.shape[0]
    n_asg = n_tok * TOP_K
    logits = (t @ w_r + b_r).astype(jnp.float32)
    top_v, top_e = lax.top_k(logits, TOP_K)
    gates = jax.nn.softmax(top_v, axis=-1).astype(h.dtype)
    e_flat = top_e.reshape(-1)
    tok_flat = jnp.arange(n_asg, dtype=jnp.int32) // TOP_K
    g_flat = gates.reshape(-1)
    order = jnp.argsort(e_flat)
    e_sorted = e_flat[order]
    counts = jnp.bincount(e_flat, length=N_EXPERTS)
    padded = ((counts + MOE_BLOCK - 1) // MOE_BLOCK) * MOE_BLOCK
    off = jnp.cumsum(counts) - counts
    pend = jnp.cumsum(padded)
    poff = pend - padded
    dest = poff[e_sorted] + (jnp.arange(n_asg, dtype=jnp.int32) - off[e_sorted])
    n_rows = -(-n_asg // MOE_BLOCK) * MOE_BLOCK + N_EXPERTS * MOE_BLOCK
    n_blk = n_rows // MOE_BLOCK
    buf_tok = jnp.full((n_rows,), n_tok, dtype=jnp.int32).at[dest].set(tok_flat[order])
    buf_g = jnp.zeros((n_rows,), h.dtype).at[dest].set(g_flat[order])
    blk_start = jnp.arange(n_blk, dtype=jnp.int32) * MOE_BLOCK
    blk_e = jnp.minimum(jnp.sum(blk_start[:, None] >= pend[None, :], axis=1), N_EXPERTS - 1)
    t_pad = jnp.concatenate([t, jnp.zeros((1, d), t.dtype)], axis=0)
    x_buf = t_pad[buf_tok].reshape(n_blk, MOE_BLOCK, d)

    def expert_block(args):
        xb, e = args
        return clamped_swiglu(xb @ w1[e] + b1[e]) @ w2[e] + b2[e]

    y_buf = lax.map(expert_block, (x_buf, blk_e)).reshape(n_rows, d)
    out = jnp.zeros((n_tok + 1, d), h.dtype).at[buf_tok].add(y_buf * buf_g[:, None])
    return out[:n_tok].reshape(bsz, seq_len, d)


def setup_inputs(seed: int = 0) -> dict:
    key = jax.random.key(seed)
    ks = jax.random.split(key, 40)
    f32 = jnp.float32

    def nrm(k, shape, scale):
        return scale * jax.random.normal(k, shape, f32)

    nc, ns = N_CONV_LAYERS, N_SSM_LAYERS
    dt0 = jnp.exp(jax.random.uniform(ks[14], (ns, SSM_HEADS), f32, math.log(1e-3), math.log(1e-1)))
    return {
        'x': nrm(ks[0], (BATCH, SEQ, D_MODEL), 1.0),
        'meta_tokens': nrm(ks[1], (N_META, D_MODEL), 1.0),
        'conv_w1': nrm(ks[2], (nc, D_MODEL, 2 * CONV_INNER), D_MODEL ** -0.5),
        'conv_b1': nrm(ks[3], (nc, 2 * CONV_INNER), 0.01),
        'conv_dw': nrm(ks[4], (nc, CONV_KERNEL, CONV_INNER), CONV_KERNEL ** -0.5),
        'conv_dwb': nrm(ks[5], (nc, CONV_INNER), 0.01),
        'conv_ln_g': 1.0 + nrm(ks[6], (nc, CONV_INNER), 0.1),
        'conv_ln_b': nrm(ks[7], (nc, CONV_INNER), 0.01),
        'conv_w2': nrm(ks[8], (nc, CONV_INNER, D_MODEL), DEEPNORM_BETA * CONV_INNER ** -0.5),
        'conv_b2': nrm(ks[9], (nc, D_MODEL), 0.01),
        'ssm_w_in': nrm(ks[10], (ns, D_MODEL, SSM_IN_DIM), D_MODEL ** -0.5),
        'ssm_conv_w': nrm(ks[11], (ns, SSM_CONV, SSM_CONV_DIM), SSM_CONV ** -0.5),
        'ssm_conv_b': nrm(ks[12], (ns, SSM_CONV_DIM), 0.01),
        'ssm_dt_bias': dt0 + jnp.log(-jnp.expm1(-dt0)),
        'ssm_a_log': jnp.log(jax.random.uniform(ks[13], (ns, SSM_HEADS), f32, 1.0, 16.0)),
        'ssm_d': 1.0 + nrm(ks[15], (ns, SSM_HEADS), 0.1),
        'ssm_norm_g': 1.0 + nrm(ks[16], (ns, D_INNER), 0.1),
        'ssm_w_out': nrm(ks[17], (ns, D_INNER, D_MODEL), DEEPNORM_BETA * D_INNER ** -0.5),
        'moe_w_router': nrm(ks[18], (DEPTH, D_MODEL, N_EXPERTS), D_MODEL ** -0.5),
        'moe_b_router': nrm(ks[19], (DEPTH, N_EXPERTS), 0.01),
        'moe_w1': nrm(ks[20], (DEPTH, N_EXPERTS, D_MODEL, 2 * D_FF), D_MODEL ** -0.5),
        'moe_b1': nrm(ks[21], (DEPTH, N_EXPERTS, 2 * D_FF), 0.01),
        'moe_w2': nrm(ks[22], (DEPTH, N_EXPERTS, D_FF, D_MODEL), DEEPNORM_BETA * D_FF ** -0.5),
        'moe_b2': nrm(ks[23], (DEPTH, N_EXPERTS, D_MODEL), 0.01),
        'ln_mix_g': 1.0 + nrm(ks[24], (DEPTH, D_MODEL), 0.1),
        'ln_mix_b': nrm(ks[25], (DEPTH, D_MODEL), 0.01),
        'ln_ffn_g': 1.0 + nrm(ks[26], (DEPTH, D_MODEL), 0.1),
        'ln_ffn_b': nrm(ks[27], (DEPTH, D_MODEL), 0.01),
    }


def reference(x, meta_tokens, conv_w1, conv_b1, conv_dw, conv_dwb, conv_ln_g, conv_ln_b, conv_w2, conv_b2,
              ssm_w_in, ssm_conv_w, ssm_conv_b, ssm_dt_bias, ssm_a_log, ssm_d, ssm_norm_g, ssm_w_out,
              moe_w_router, moe_b_router, moe_w1, moe_b1, moe_w2, moe_b2,
              ln_mix_g, ln_mix_b, ln_ffn_g, ln_ffn_b):
    bsz = x.shape[0]
    meta = jnp.broadcast_to(meta_tokens.astype(x.dtype)[None], (bsz, N_META, x.shape[-1]))
    h = jnp.concatenate([meta, x], axis=1)
    for i in range(DEPTH):
        j = i // N_MIXERS
        if i % N_MIXERS == 0:
            mix = conformer_conv_mixer(h, conv_w1[j], conv_b1[j], conv_dw[j], conv_dwb[j],
                                       conv_ln_g[j], conv_ln_b[j], conv_w2[j], conv_b2[j])
        else:
            mix = mamba2_ssd_mixer(h, ssm_w_in[j], ssm_conv_w[j], ssm_conv_b[j], ssm_dt_bias[j],
                                   ssm_a_log[j], ssm_d[j], ssm_norm_g[j], ssm_w_out[j])
        h = layer_norm(DEEPNORM_ALPHA * h + mix, ln_mix_g[i], ln_mix_b[i])
        ffn = moe_ffn(h, moe_w_router[i], moe_b_router[i], moe_w1[i], moe_b1[i], moe_w2[i], moe_b2[i])
        h = layer_norm(DEEPNORM_ALPHA * h + ffn, ln_ffn_g[i], ln_ffn_b[i])
    return h[:, N_META:]
```

```python
import functools

import jax
import jax.numpy as jnp
from jax import lax
from jax.experimental import pallas as pl
from jax.experimental.pallas import tpu as pltpu

F32 = jnp.float32
BF16 = jnp.bfloat16
I32 = jnp.int32
HIGHEST = lax.Precision.HIGHEST

N_META = 16
SSM_HEAD_DIM = 64
SSM_GROUPS = 8
SSM_STATE = 128
SSM_CHUNK = 128
TOP_K = 4
SWIGLU_ALPHA = 1.702
SWIGLU_LIMIT = 7.0
LN_EPS = 1e-5
RMS_EPS = 1e-5

CONV_HALO = 32
CONV_ROWS = 16
MOE_ROWS = 256
VMEM_LIMIT = 56 * 1024 * 1024


def _cparams(*sem):
    return pltpu.CompilerParams(dimension_semantics=tuple(sem), vmem_limit_bytes=VMEM_LIMIT)


def _pick(n, prefs):
    for p in prefs:
        if n % p == 0:
            return p
    raise ValueError(f"no tile for {n} in {prefs}")


def _layer_norm(y, g, b):
    mu = jnp.mean(y, axis=-1, keepdims=True)
    yc = y - mu
    var = jnp.mean(yc * yc, axis=-1, keepdims=True)
    return yc * lax.rsqrt(var + LN_EPS) * g + b


def _sigmoid(x):
    return 1.0 / (1.0 + jnp.exp(-x))


def _softplus(x):
    return jnp.maximum(x, 0.0) + jnp.log(1.0 + jnp.exp(-jnp.abs(x)))


def _glu_kernel(x_ref, wa_ref, wg_ref, ba_ref, bg_ref, o_ref, *, pad):
    j = pl.program_id(1)
    x = x_ref[0].astype(BF16)
    a = jnp.dot(x, wa_ref[...], preferred_element_type=F32) + ba_ref[...]
    g = jnp.dot(x, wg_ref[...], preferred_element_type=F32) + bg_ref[...]
    u = a * _sigmoid(g)
    row = lax.broadcasted_iota(I32, u.shape, 0)
    valid = jnp.logical_or(j > 0, row >= pad)
    o_ref[0] = jnp.where(valid, u, 0.0)


def _glu(h, wa, wg, ba, bg, pad, tl):
    bsz, lp, d = h.shape
    c = wa.shape[1]
    return pl.pallas_call(
        functools.partial(_glu_kernel, pad=pad),
        grid=(bsz, lp // tl),
        in_specs=[
            pl.BlockSpec((1, tl, d), lambda b, j: (b, j, 0)),
            pl.BlockSpec((d, c), lambda b, j: (0, 0)),
            pl.BlockSpec((d, c), lambda b, j: (0, 0)),
            pl.BlockSpec((1, c), lambda b, j: (0, 0)),
            pl.BlockSpec((1, c), lambda b, j: (0, 0)),
        ],
        out_specs=pl.BlockSpec((1, tl, c), lambda b, j: (b, j, 0)),
        out_shape=jax.ShapeDtypeStruct((bsz, lp, c), F32),
        compiler_params=_cparams("parallel", "parallel"),
        name="conf_glu",
    )(h, wa, wg, ba, bg)


def _conv_mix_kernel(ucur_ref, uprev_ref, h_ref, dw_ref, dwb_ref, g1_ref, b1_ref, w2_ref, b2_ref,
                     g2_ref, bb2_ref, o_ref, win_ref, act_ref, *, taps, alpha):
    j = pl.program_id(1)
    tl = ucur_ref.shape[1]

    @pl.when(j == 0)
    def _():
        win_ref[0:CONV_HALO, :] = jnp.zeros((CONV_HALO, win_ref.shape[1]), F32)

    @pl.when(j > 0)
    def _():
        win_ref[0:CONV_HALO, :] = uprev_ref[0, tl - CONV_HALO:tl, :]

    win_ref[CONV_HALO:CONV_HALO + tl, :] = ucur_ref[0]

    def chunk(c, carry):
        base = pl.multiple_of(c * CONV_ROWS, CONV_ROWS)
        acc = jnp.zeros((CONV_ROWS, win_ref.shape[1]), F32) + dwb_ref[...]
        win = win_ref[pl.ds(base, CONV_HALO + CONV_ROWS), :]
        for k in range(taps):
            off = CONV_HALO + k - (taps - 1)
            acc = acc + win[off:off + CONV_ROWS, :] * dw_ref[k:k + 1, :]
        y = _layer_norm(acc, g1_ref[...], b1_ref[...])
        y = y * _sigmoid(y)
        act_ref[pl.ds(base, CONV_ROWS), :] = y.astype(BF16)
        return carry

    lax.fori_loop(0, tl // CONV_ROWS, chunk, 0)
    mix = jnp.dot(act_ref[...], w2_ref[...], preferred_element_type=F32) + b2_ref[...]
    y = alpha * h_ref[0] + mix
    o_ref[0] = _layer_norm(y, g2_ref[...], bb2_ref[...])


def _conv_mix(u, h, dw, dwb, g1, b1, w2, b2, g2, bb2, alpha, tl):
    bsz, lp, c = u.shape
    d = h.shape[2]
    taps = dw.shape[0]
    assert taps - 1 <= CONV_HALO and tl % CONV_ROWS == 0
    vec = lambda n: pl.BlockSpec((1, n), lambda b, j: (0, 0))
    return pl.pallas_call(
        functools.partial(_conv_mix_kernel, taps=taps, alpha=alpha),
        grid=(bsz, lp // tl),
        in_specs=[
            pl.BlockSpec((1, tl, c), lambda b, j: (b, j, 0)),
            pl.BlockSpec((1, tl, c), lambda b, j: (b, jnp.maximum(j - 1, 0), 0)),
            pl.BlockSpec((1, tl, d), lambda b, j: (b, j, 0)),
            pl.BlockSpec((taps, c), lambda b, j: (0, 0)),
            vec(c), vec(c), vec(c),
            pl.BlockSpec((c, d), lambda b, j: (0, 0)),
            vec(d), vec(d), vec(d),
        ],
        out_specs=pl.BlockSpec((1, tl, d), lambda b, j: (b, j, 0)),
        out_shape=jax.ShapeDtypeStruct((bsz, lp, d), F32),
        scratch_shapes=[pltpu.VMEM((CONV_HALO + tl, c), F32), pltpu.VMEM((tl, c), BF16)],
        compiler_params=_cparams("parallel", "parallel"),
        name="conf_conv_mix",
    )(u, u, h, dw, dwb, g1, b1, w2, b2, g2, bb2)


def _mm_kernel(x_ref, w_ref, o_ref):
    o_ref[...] = jnp.dot(x_ref[...].astype(BF16), w_ref[...], preferred_element_type=F32).astype(o_ref.dtype)


def _mm(x, w, tm, tn, out_dtype=F32):
    m, k = x.shape
    n = w.shape[1]
    return pl.pallas_call(
        _mm_kernel,
        grid=(n // tn, m // tm),
        in_specs=[pl.BlockSpec((tm, k), lambda jn, im: (im, 0)),
                  pl.BlockSpec((k, tn), lambda jn, im: (0, jn))],
        out_specs=pl.BlockSpec((tm, tn), lambda jn, im: (im, jn)),
        out_shape=jax.ShapeDtypeStruct((m, n), out_dtype),
        compiler_params=_cparams("parallel", "parallel"),
        name="mm",
    )(x, w)


def _mm_resid_ln_kernel(x_ref, w_ref, h_ref, g_ref, b_ref, o_ref, *, alpha):
    mix = jnp.dot(x_ref[...].astype(BF16), w_ref[...], preferred_element_type=F32)
    o_ref[...] = _layer_norm(alpha * h_ref[...] + mix, g_ref[...], b_ref[...])


def _mm_resid_ln(x, w, h, g, b, alpha, tm):
    m, k = x.shape
    d = w.shape[1]
    return pl.pallas_call(
        functools.partial(_mm_resid_ln_kernel, alpha=alpha),
        grid=(m // tm,),
        in_specs=[pl.BlockSpec((tm, k), lambda i: (i, 0)),
                  pl.BlockSpec((k, d), lambda i: (0, 0)),
                  pl.BlockSpec((tm, d), lambda i: (i, 0)),
                  pl.BlockSpec((1, d), lambda i: (0, 0)),
                  pl.BlockSpec((1, d), lambda i: (0, 0))],
        out_specs=pl.BlockSpec((tm, d), lambda i: (i, 0)),
        out_shape=jax.ShapeDtypeStruct((m, d), F32),
        compiler_params=_cparams("parallel"),
        name="mm_resid_ln",
    )(x, w, h, g, b)


def _dt_kernel(x_ref, wc_ref, wr_ref, bc_ref, br_ref, ac_ref, ar_ref, dtc_ref, csc_ref, csr_ref, *, pad):
    c = pl.program_id(1)
    x = x_ref[0]
    n = x.shape[0]
    raw_c = jnp.dot(x, wc_ref[...], preferred_element_type=F32, precision=HIGHEST)
    raw_r = lax.dot_general(wr_ref[...], x, (((1,), (1,)), ((), ())),
                            preferred_element_type=F32, precision=HIGHEST)
    row_c = lax.broadcasted_iota(I32, raw_c.shape, 0)
    row_r = lax.broadcasted_iota(I32, raw_r.shape, 1)
    first = c == 0
    dt_c = jnp.where(jnp.logical_and(first, row_c < pad), 0.0, _softplus(raw_c + bc_ref[...]))
    dt_r = jnp.where(jnp.logical_and(first, row_r < pad), 0.0, _softplus(raw_r + br_ref[...]))
    li = lax.broadcasted_iota(I32, (n, n), 0)
    si = lax.broadcasted_iota(I32, (n, n), 1)
    tri = (si <= li).astype(F32)
    tri_t = (li <= si).astype(F32)
    dtc_ref[0] = dt_c
    csc_ref[0] = jnp.dot(tri, dt_c * ac_ref[...], preferred_element_type=F32, precision=HIGHEST)
    csr_ref[0, 0] = jnp.dot(dt_r * ar_ref[...], tri_t, preferred_element_type=F32, precision=HIGHEST)


def _dt_prep(h, w_dt, dt_bias, a_neg, pad):
    bsz, lp, d = h.shape
    nh = w_dt.shape[1]
    nc = lp // SSM_CHUNK
    lanes = 128
    wc = jnp.zeros((d, lanes), F32).at[:, :nh].set(w_dt)
    bc = jnp.zeros((1, lanes), F32).at[0, :nh].set(dt_bias)
    ac = jnp.zeros((1, lanes), F32).at[0, :nh].set(a_neg)
    return pl.pallas_call(
        functools.partial(_dt_kernel, pad=pad),
        grid=(bsz, nc),
        in_specs=[
            pl.BlockSpec((1, SSM_CHUNK, d), lambda b, c: (b, c, 0)),
            pl.BlockSpec((d, lanes), lambda b, c: (0, 0)),
            pl.BlockSpec((nh, d), lambda b, c: (0, 0)),
            pl.BlockSpec((1, lanes), lambda b, c: (0, 0)),
            pl.BlockSpec((nh, 1), lambda b, c: (0, 0)),
            pl.BlockSpec((1, lanes), lambda b, c: (0, 0)),
            pl.BlockSpec((nh, 1), lambda b, c: (0, 0)),
        ],
        out_specs=[
            pl.BlockSpec((1, SSM_CHUNK, lanes), lambda b, c: (b, c, 0)),
            pl.BlockSpec((1, SSM_CHUNK, lanes), lambda b, c: (b, c, 0)),
            pl.BlockSpec((1, 1, nh, SSM_CHUNK), lambda b, c: (b, c, 0, 0)),
        ],
        out_shape=[
            jax.ShapeDtypeStruct((bsz, lp, lanes), F32),
            jax.ShapeDtypeStruct((bsz, lp, lanes), F32),
            jax.ShapeDtypeStruct((bsz, nc, nh, SSM_CHUNK), F32),
        ],
        compiler_params=_cparams("parallel", "parallel"),
        name="ssm_dt",
    )(h, wc, w_dt.T, bc, dt_bias[:, None], ac, a_neg[:, None])


def _ssd_kernel(xs_ref, bm_ref, cm_ref, z_ref, dtc_ref, csc_ref, csr_ref,
                cwx_ref, cwb_ref, cwc_ref, cbx_ref, cbb_ref, cbc_ref, dsk_ref, ng_ref,
                o_ref, state_ref, carx_ref, carb_ref, carc_ref, *, pad, hpg, taps):
    g = pl.program_id(1)
    q = SSM_CHUNK
    hd = SSM_HEAD_DIM
    gw = hpg * hd
    n_chunks = xs_ref.shape[1] // q

    state_ref[...] = jnp.zeros(state_ref.shape, F32)
    carx_ref[...] = jnp.zeros(carx_ref.shape, F32)
    carb_ref[...] = jnp.zeros(carb_ref.shape, F32)
    carc_ref[...] = jnp.zeros(carc_ref.shape, F32)

    hrow = lax.broadcasted_iota(I32, (128, gw), 0)
    hcol = lax.broadcasted_iota(I32, (128, gw), 1) // hd
    sel = (hrow == g * hpg + hcol).astype(F32)
    lane_head = lax.broadcasted_iota(I32, (q, gw), 1) // hd
    li = lax.broadcasted_iota(I32, (q, q), 0)
    si = lax.broadcasted_iota(I32, (q, q), 1)
    causal = si <= li
    row1 = lax.broadcasted_iota(I32, (q, 1), 0)

    def conv_act(raw, car_ref, w_ref, b_ref, valid):
        raw = jnp.where(valid, raw, 0.0)
        win = jnp.concatenate([car_ref[...], raw], axis=0)
        acc = jnp.zeros(raw.shape, F32) + b_ref[...]
        for k in range(taps):
            off = 8 + k - (taps - 1)
            acc = acc + win[off:off + q, :] * w_ref[k:k + 1, :]
        car_ref[...] = raw[q - 8:q, :]
        act = acc * _sigmoid(acc)
        return jnp.where(valid, act, 0.0)

    def chunk(c, carry):
        r0 = pl.multiple_of(c * q, q)
        valid = jnp.logical_or(c > 0, row1 >= pad)
        xs = conv_act(xs_ref[0, pl.ds(r0, q), :], carx_ref, cwx_ref, cbx_ref, valid)
        bm = conv_act(bm_ref[0, pl.ds(r0, q), :], carb_ref, cwb_ref, cbb_ref, valid)
        cm = conv_act(cm_ref[0, pl.ds(r0, q), :], carc_ref, cwc_ref, cbc_ref, valid)

        dt_e = jnp.dot(dtc_ref[0, pl.ds(r0, q), :], sel, preferred_element_type=F32, precision=HIGHEST)
        cs_e = jnp.dot(csc_ref[0, pl.ds(r0, q), :], sel, preferred_element_type=F32, precision=HIGHEST)
        cs_r = csr_ref[0, 0, c]
        a_last = cs_e[q - 1:q, :]

        x_dt = xs * dt_e
        xw = (x_dt * jnp.exp(a_last - cs_e)).astype(BF16)
        bm16 = bm.astype(BF16)
        cm16 = cm.astype(BF16)
        cb = lax.dot_general(cm16, bm16, (((1,), (1,)), ((), ())), preferred_element_type=F32)

        m_parts = []
        x_parts = []
        for r in range(hpg):
            col = cs_e[:, r * hd:r * hd + 1]
            seg = col - cs_r[r:r + 1, :]
            decay = jnp.exp(jnp.where(causal, seg, -jnp.inf))
            m_parts.append((cb * decay).astype(BF16))
            x_parts.append(jnp.where(lane_head == r, x_dt, 0.0).astype(BF16))
        m_cat = jnp.concatenate(m_parts, axis=1)
        x_bd = jnp.concatenate(x_parts, axis=0)
        y = jnp.dot(m_cat, x_bd, preferred_element_type=F32)

        st = state_ref[...]
        y = y + jnp.dot(cm16, st.astype(BF16), preferred_element_type=F32) * jnp.exp(cs_e)
        new = jnp.dot(bm.T.astype(BF16), xw, preferred_element_type=F32)
        state_ref[...] = st * jnp.exp(a_last) + new
        y = y + dsk_ref[...] * xs

        z = z_ref[0, pl.ds(r0, q), :]
        y = y * (z * _sigmoid(z))
        y = y * lax.rsqrt(jnp.mean(y * y, axis=-1, keepdims=True) + RMS_EPS) * ng_ref[...]
        o_ref[0, pl.ds(r0, q), :] = y.astype(o_ref.dtype)
        return carry

    lax.fori_loop(0, n_chunks, chunk, 0)


def _ssd(proj, dtc, csc, csr, conv_w, conv_b, d_exp, norm_g, pad, d_inner):
    bsz, lp, _ = proj.shape
    g = SSM_GROUPS
    n = SSM_STATE
    gw = d_inner // g
    hpg = gw // SSM_HEAD_DIM
    assert gw % 128 == 0 and n == 128 and dtc.shape[2] == 128
    taps = conv_w.shape[0]
    nc = lp // SSM_CHUNK
    zb = 0
    xb = d_inner // gw
    bb = 2 * d_inner // n
    cb_ = (2 * d_inner + g * n) // n
    cxb = 0
    cbb = d_inner // n
    ccb = (d_inner + g * n) // n
    seq = lambda w, off: pl.BlockSpec((1, lp, w), lambda b, j, off=off: (b, 0, off + j))
    cw = lambda w, off: pl.BlockSpec((taps, w), lambda b, j, off=off: (0, off + j))
    cbv = lambda w, off: pl.BlockSpec((1, w), lambda b, j, off=off: (0, off + j))
    return pl.pallas_call(
        functools.partial(_ssd_kernel, pad=pad, hpg=hpg, taps=taps),
        grid=(bsz, g),
        in_specs=[
            seq(gw, xb), seq(n, bb), seq(n, cb_), seq(gw, zb),
            pl.BlockSpec((1, lp, 128), lambda b, j: (b, 0, 0)),
            pl.BlockSpec((1, lp, 128), lambda b, j: (b, 0, 0)),
            pl.BlockSpec((1, 1, nc, hpg, SSM_CHUNK), lambda b, j: (b, j, 0, 0, 0)),
            cw(gw, cxb), cw(n, cbb), cw(n, ccb),
            cbv(gw, cxb), cbv(n, cbb), cbv(n, ccb),
            pl.BlockSpec((1, gw), lambda b, j: (0, j)),
            pl.BlockSpec((1, gw), lambda b, j: (0, j)),
        ],
        out_specs=pl.BlockSpec((1, lp, gw), lambda b, j: (b, 0, j)),
        out_shape=jax.ShapeDtypeStruct((bsz, lp, d_inner), BF16),
        scratch_shapes=[pltpu.VMEM((n, gw), F32), pltpu.VMEM((8, gw), F32),
                        pltpu.VMEM((8, n), F32), pltpu.VMEM((8, n), F32)],
        compiler_params=_cparams("parallel", "parallel"),
        name="ssm_ssd",
    )(proj, proj, proj, proj, dtc, csc, csr, conv_w, conv_w, conv_w, conv_b, conv_b, conv_b, d_exp, norm_g)


def _router_kernel(x_ref, wr_ref, br_ref, e_ref, gate_ref, rank_ref, cnt_ref, carry_ref):
    i = pl.program_id(0)

    @pl.when(i == 0)
    def _():
        carry_ref[...] = jnp.zeros(carry_ref.shape, F32)

    x = x_ref[...]
    tm = x.shape[0]
    ne = wr_ref.shape[0]
    logits = lax.dot_general(wr_ref[...], x, (((1,), (1,)), ((), ())),
                             preferred_element_type=F32, precision=HIGHEST) + br_ref[...]
    ids = lax.broadcasted_iota(I32, (ne, tm), 0)
    vals = logits
    top_v, top_e, sels = [], [], []
    for _ in range(TOP_K):
        m = jnp.max(vals, axis=0, keepdims=True)
        idx = jnp.min(jnp.where(vals == m, ids, ne), axis=0, keepdims=True)
        sel = ids == idx
        top_v.append(m)
        top_e.append(idx)
        sels.append(sel)
        vals = jnp.where(sel, -jnp.inf, vals)
    ex = [jnp.exp(v - top_v[0]) for v in top_v]
    den = ex[0]
    for t in ex[1:]:
        den = den + t
    e_ref[...] = jnp.concatenate(top_e, axis=0)
    gate_ref[...] = jnp.concatenate([t / den for t in ex], axis=0)

    onehot = sels[0].astype(F32)
    for s in sels[1:]:
        onehot = onehot + s.astype(F32)
    ji = lax.broadcasted_iota(I32, (tm, tm), 0)
    ti = lax.broadcasted_iota(I32, (tm, tm), 1)
    before = (ji < ti).astype(BF16)
    prefix = jnp.dot(onehot.astype(BF16), before, preferred_element_type=F32)
    prefix = prefix + carry_ref[:, 0:1]
    ranks = [jnp.sum(jnp.where(s, prefix, 0.0), axis=0, keepdims=True) for s in sels]
    rank_ref[...] = jnp.concatenate(ranks, axis=0).astype(I32)
    carry_ref[...] = carry_ref[...] + jnp.sum(onehot, axis=1, keepdims=True)
    cnt_ref[...] = carry_ref[...].astype(I32)


def _router(t, w_r, b_r, tm):
    nt, d = t.shape
    ne = w_r.shape[1]
    return pl.pallas_call(
        _router_kernel,
        grid=(nt // tm,),
        in_specs=[pl.BlockSpec((tm, d), lambda i: (i, 0)),
                  pl.BlockSpec((ne, d), lambda i: (0, 0)),
                  pl.BlockSpec((ne, 1), lambda i: (0, 0))],
        out_specs=[pl.BlockSpec((TOP_K, tm), lambda i: (0, i)),
                   pl.BlockSpec((TOP_K, tm), lambda i: (0, i)),
                   pl.BlockSpec((TOP_K, tm), lambda i: (0, i)),
                   pl.BlockSpec((ne, 128), lambda i: (0, 0))],
        out_shape=[jax.ShapeDtypeStruct((TOP_K, nt), I32),
                   jax.ShapeDtypeStruct((TOP_K, nt), F32),
                   jax.ShapeDtypeStruct((TOP_K, nt), I32),
                   jax.ShapeDtypeStruct((ne, 128), I32)],
        scratch_shapes=[pltpu.VMEM((ne, 128), F32)],
        compiler_params=_cparams("arbitrary"),
        name="moe_router",
    )(t, w_r.T, b_r[:, None])


def _dispatch_kernel(dest_hbm, t_hbm, zeros_hbm, xbuf_hbm, idx_ref, isem, sem, *, tm):
    del zeros_hbm
    i = pl.program_id(0)
    n = TOP_K * tm
    cp = pltpu.make_async_copy(dest_hbm.at[pl.ds(pl.multiple_of(i * n, n), n)], idx_ref, isem)
    cp.start()
    cp.wait()

    def row_copy(k, r):
        return pltpu.make_async_copy(t_hbm.at[pl.ds(i * tm + r, 1), :],
                                     xbuf_hbm.at[pl.ds(idx_ref[k * tm + r], 1), :], sem)

    def issue(r, carry):
        for k in range(TOP_K):
            row_copy(k, r).start()
        return carry

    def drain(r, carry):
        for k in range(TOP_K):
            row_copy(k, r).wait()
        return carry

    lax.fori_loop(0, tm, issue, 0)
    lax.fori_loop(0, tm, drain, 0)


def _dispatch(t, dest_tiles, n_rows, tm):
    nt, d = t.shape
    zeros = jnp.zeros((n_rows, d), t.dtype)
    return pl.pallas_call(
        functools.partial(_dispatch_kernel, tm=tm),
        grid=(nt // tm,),
        in_specs=[pl.BlockSpec(memory_space=pl.ANY),
                  pl.BlockSpec(memory_space=pl.ANY),
                  pl.BlockSpec(memory_space=pl.ANY)],
        out_specs=pl.BlockSpec(memory_space=pl.ANY),
        out_shape=jax.ShapeDtypeStruct((n_rows, d), t.dtype),
        scratch_shapes=[pltpu.SMEM((TOP_K * tm,), I32), pltpu.SemaphoreType.DMA(()),
                        pltpu.SemaphoreType.DMA(())],
        input_output_aliases={2: 0},
        compiler_params=pltpu.CompilerParams(dimension_semantics=("arbitrary",), has_side_effects=True),
        name="moe_dispatch",
    )(dest_tiles, t, zeros)


def _ffn_kernel(blk_e_ref, nused_ref, x_ref, w1g_ref, w1l_ref, b1g_ref, b1l_ref, w2_ref, b2_ref, y_ref):
    del blk_e_ref
    i = pl.program_id(0)

    @pl.when(i < nused_ref[0])
    def _():
        x = x_ref[...].astype(BF16)
        hg = jnp.dot(x, w1g_ref[0], preferred_element_type=F32) + b1g_ref[0]
        hl = jnp.dot(x, w1l_ref[0], preferred_element_type=F32) + b1l_ref[0]
        xg = jnp.minimum(hg, SWIGLU_LIMIT)
        xl = jnp.clip(hl, -SWIGLU_LIMIT, SWIGLU_LIMIT)
        act = xg * _sigmoid(SWIGLU_ALPHA * xg) * (xl + 1.0)
        y_ref[...] = jnp.dot(act.astype(BF16), w2_ref[0], preferred_element_type=F32) + b2_ref[0]

    @pl.when(i >= nused_ref[0])
    def _():
        y_ref[...] = jnp.zeros(y_ref.shape, F32)


def _ffn(x_buf, blk_e, n_used, w1g, w1l, b1g, b1l, w2, b2):
    n_rows, d = x_buf.shape
    f = w1g.shape[2]
    n_blk = n_rows // MOE_ROWS
    wspec = lambda a, b: pl.BlockSpec((1, a, b), lambda i, be, nu: (be[i], 0, 0))
    return pl.pallas_call(
        _ffn_kernel,
        grid_spec=pltpu.PrefetchScalarGridSpec(
            num_scalar_prefetch=2,
            grid=(n_blk,),
            in_specs=[pl.BlockSpec((MOE_ROWS, d), lambda i, be, nu: (i, 0)),
                      wspec(d, f), wspec(d, f), wspec(1, f), wspec(1, f), wspec(f, d), wspec(1, d)],
            out_specs=pl.BlockSpec((MOE_ROWS, d), lambda i, be, nu: (i, 0)),
        ),
        out_shape=jax.ShapeDtypeStruct((n_rows, d), F32),
        compiler_params=_cparams("arbitrary"),
        name="moe_ffn",
    )(blk_e, n_used, x_buf, w1g, w1l, b1g, b1l, w2, b2)


def _combine_kernel(dest_hbm, ybuf_hbm, gate_ref, h_ref, g_ref, b_ref, o_ref, idx_ref, rows_ref, isem, sem,
                    *, tm, alpha):
    i = pl.program_id(0)
    n = TOP_K * tm
    cp = pltpu.make_async_copy(dest_hbm.at[pl.ds(pl.multiple_of(i * n, n), n)], idx_ref, isem)
    cp.start()
    cp.wait()

    def row_copy(k, r):
        return pltpu.make_async_copy(ybuf_hbm.at[pl.ds(idx_ref[k * tm + r], 1), :],
                                     rows_ref.at[k, pl.ds(r, 1), :], sem)

    def issue(r, carry):
        for k in range(TOP_K):
            row_copy(k, r).start()
        return carry

    def drain(r, carry):
        for k in range(TOP_K):
            row_copy(k, r).wait()
        return carry

    lax.fori_loop(0, tm, issue, 0)
    lax.fori_loop(0, tm, drain, 0)
    gates = gate_ref[...]
    y = alpha * h_ref[...]
    for k in range(TOP_K):
        y = y + gates[:, k:k + 1] * rows_ref[k]
    o_ref[...] = _layer_norm(y, g_ref[...], b_ref[...])


def _combine(y_buf, dest_tiles, gates_t, h, g, b, alpha, tm):
    nt, d = h.shape
    return pl.pallas_call(
        functools.partial(_combine_kernel, tm=tm, alpha=alpha),
        grid=(nt // tm,),
        in_specs=[pl.BlockSpec(memory_space=pl.ANY),
                  pl.BlockSpec(memory_space=pl.ANY),
                  pl.BlockSpec((tm, TOP_K), lambda i: (i, 0)),
                  pl.BlockSpec((tm, d), lambda i: (i, 0)),
                  pl.BlockSpec((1, d), lambda i: (0, 0)),
                  pl.BlockSpec((1, d), lambda i: (0, 0))],
        out_specs=pl.BlockSpec((tm, d), lambda i: (i, 0)),
        out_shape=jax.ShapeDtypeStruct((nt, d), F32),
        scratch_shapes=[pltpu.SMEM((TOP_K * tm,), I32), pltpu.VMEM((TOP_K, tm, d), F32),
                        pltpu.SemaphoreType.DMA(()), pltpu.SemaphoreType.DMA(())],
        compiler_params=_cparams("arbitrary"),
        name="moe_combine",
    )(dest_tiles, y_buf, gates_t, h, g, b)


def _moe_layer(t, w_r, b_r, w1, b1, w2, b2, ln_g, ln_b, alpha):
    nt, d = t.shape
    ne = w_r.shape[1]
    tm_r = _pick(nt, (512, 256, 128))
    tm_d = _pick(nt, (256, 128))
    top_e, gates, rank, cnt = _router(t, w_r, b_r, tm_r)

    counts = cnt[:, 0]
    padded = ((counts + MOE_ROWS - 1) // MOE_ROWS) * MOE_ROWS
    pend = jnp.cumsum(padded)
    poff = pend - padded
    dest = poff[top_e] + rank
    n_rows = nt * TOP_K + ne * MOE_ROWS
    n_blk = n_rows // MOE_ROWS
    blk_start = jnp.arange(n_blk, dtype=I32) * MOE_ROWS
    blk_e = jnp.minimum(jnp.sum(blk_start[:, None] >= pend[None, :], axis=1), ne - 1).astype(I32)
    n_used = (pend[-1:] // MOE_ROWS).astype(I32)
    dest_tiles = dest.reshape(TOP_K, nt // tm_d, tm_d).transpose(1, 0, 2).reshape(-1)

    x_buf = _dispatch(t, dest_tiles, n_rows, tm_d)
    w1g = w1[:, :, 0::2].astype(BF16)
    w1l = w1[:, :, 1::2].astype(BF16)
    b1g = b1[:, None, 0::2]
    b1l = b1[:, None, 1::2]
    y_buf = _ffn(x_buf, blk_e, n_used, w1g, w1l, b1g, b1l, w2.astype(BF16), b2[:, None, :])
    return _combine(y_buf, dest_tiles, gates.T, t, ln_g[None, :], ln_b[None, :], alpha, tm_d)


def _conformer_layer(h, w1, b1, dw, dwb, ln_g, ln_b, w2, b2, mix_g, mix_b, alpha, pad):
    bsz, lp, d = h.shape
    c = dw.shape[1]
    tl = _pick(lp, (544, 272, 384, 192, 128))
    u = _glu(h, w1[:, :c].astype(BF16), w1[:, c:].astype(BF16), b1[None, :c], b1[None, c:], pad, tl)
    return _conv_mix(u, h, dw, dwb[None, :], ln_g[None, :], ln_b[None, :], w2.astype(BF16), b2[None, :],
                     mix_g[None, :], mix_b[None, :], alpha, tl)


def _mamba_layer(h, w_in, conv_w, conv_b, dt_bias, a_log, d_skip, norm_g, w_out, mix_g, mix_b, alpha, pad):
    bsz, lp, d = h.shape
    nt = bsz * lp
    d_inner = w_out.shape[0]
    nh = dt_bias.shape[0]
    conv_dim = conv_w.shape[1]
    nproj = d_inner + conv_dim
    hpg = nh // SSM_GROUPS
    tm = _pick(nt, (1024, 512, 256, 128))
    proj = _mm(h.reshape(nt, d), w_in[:, :nproj].astype(BF16), tm, _pick(nproj, (2048, 1024, 512, 256, 128)))
    a_neg = -jnp.exp(a_log.astype(F32))
    dtc, csc, csr = _dt_prep(h, w_in[:, nproj:], dt_bias, a_neg, pad)
    nc = lp // SSM_CHUNK
    csr = csr.reshape(bsz, nc, SSM_GROUPS, hpg, SSM_CHUNK).transpose(0, 2, 1, 3, 4)
    d_exp = jnp.repeat(d_skip.astype(F32), SSM_HEAD_DIM)[None, :]
    y = _ssd(proj.reshape(bsz, lp, nproj), dtc, csc, csr, conv_w, conv_b[None, :], d_exp, norm_g[None, :],
             pad, d_inner)
    out = _mm_resid_ln(y.reshape(nt, d_inner), w_out.astype(BF16), h.reshape(nt, d),
                       mix_g[None, :], mix_b[None, :], alpha, _pick(nt, (512, 256, 128)))
    return out.reshape(bsz, lp, d)


def kernel(x, meta_tokens, conv_w1, conv_b1, conv_dw, conv_dwb, conv_ln_g, conv_ln_b, conv_w2, conv_b2,
           ssm_w_in, ssm_conv_w, ssm_conv_b, ssm_dt_bias, ssm_a_log, ssm_d, ssm_norm_g, ssm_w_out,
           moe_w_router, moe_b_router, moe_w1, moe_b1, moe_w2, moe_b2,
           ln_mix_g, ln_mix_b, ln_ffn_g, ln_ffn_b):
    bsz, seq, d = x.shape
    depth = ln_mix_g.shape[0]
    alpha = (2 * depth) ** 0.25
    pad = SSM_CHUNK - N_META
    lp = pad + N_META + seq
    assert lp % SSM_CHUNK == 0
    meta = jnp.broadcast_to(meta_tokens.astype(x.dtype)[None], (bsz, N_META, d))
    h = jnp.concatenate([jnp.zeros((bsz, pad, d), x.dtype), meta, x], axis=1)
    for i in range(depth):
        j = i // 2
        if i % 2 == 0:
            h = _conformer_layer(h, conv_w1[j], conv_b1[j], conv_dw[j], conv_dwb[j], conv_ln_g[j],
                                 conv_ln_b[j], conv_w2[j], conv_b2[j], ln_mix_g[i], ln_mix_b[i], alpha, pad)
        else:
            h = _mamba_layer(h, ssm_w_in[j], ssm_conv_w[j], ssm_conv_b[j], ssm_dt_bias[j], ssm_a_log[j],
                             ssm_d[j], ssm_norm_g[j], ssm_w_out[j], ln_mix_g[i], ln_mix_b[i], alpha, pad)
        t = _moe_layer(h.reshape(bsz * lp, d), moe_w_router[i], moe_b_router[i], moe_w1[i], moe_b1[i],
                       moe_w2[i], moe_b2[i], ln_ffn_g[i], ln_ffn_b[i], alpha)
        h = t.reshape(bsz, lp, d)
    return h[:, pad + N_META:]
```

```python
import functools

import jax
import jax.numpy as jnp
from jax import lax
from jax.experimental import pallas as pl
from jax.experimental.pallas import tpu as pltpu

F32 = jnp.float32
BF16 = jnp.bfloat16
I32 = jnp.int32
HIGHEST = lax.Precision.HIGHEST

N_META = 16
SSM_HEAD_DIM = 64
SSM_GROUPS = 8
SSM_STATE = 128
SSM_CHUNK = 128
TOP_K = 4
SWIGLU_ALPHA = 1.702
SWIGLU_LIMIT = 7.0
LN_EPS = 1e-5
RMS_EPS = 1e-5

CONV_HALO = 32
CONV_ROWS = 32
CONV_LANES = 256
MOE_ROWS = 256
DMA_UNROLL = 8
LANES = 128
VMEM_LIMIT = 56 * 1024 * 1024


def _cparams(*sem):
    return pltpu.CompilerParams(dimension_semantics=tuple(sem), vmem_limit_bytes=VMEM_LIMIT)


def _pick(n, prefs):
    for p in prefs:
        if n % p == 0:
            return p
    raise ValueError(f"no tile for {n} in {prefs}")


def _layer_norm(y, g, b):
    mu = jnp.mean(y, axis=-1, keepdims=True)
    yc = y - mu
    var = jnp.mean(yc * yc, axis=-1, keepdims=True)
    return yc * lax.rsqrt(var + LN_EPS) * g + b


def _sigmoid(x):
    return 1.0 / (1.0 + jnp.exp(-x))


def _softplus(x):
    return jnp.maximum(x, 0.0) + jnp.log(1.0 + jnp.exp(-jnp.abs(x)))


def _glu_kernel(x_ref, wa_ref, wg_ref, ba_ref, bg_ref, o_ref, *, pad):
    j = pl.program_id(1)
    x = x_ref[0].astype(BF16)
    a = jnp.dot(x, wa_ref[...], preferred_element_type=F32) + ba_ref[...]
    g = jnp.dot(x, wg_ref[...], preferred_element_type=F32) + bg_ref[...]
    u = a * _sigmoid(g)
    row = lax.broadcasted_iota(I32, u.shape, 0)
    valid = jnp.logical_or(j > 0, row >= pad)
    o_ref[0] = jnp.where(valid, u, 0.0)


def _glu(h, wa, wg, ba, bg, pad, tl):
    bsz, lp, d = h.shape
    c = wa.shape[1]
    return pl.pallas_call(
        functools.partial(_glu_kernel, pad=pad),
        grid=(bsz, lp // tl),
        in_specs=[
            pl.BlockSpec((1, tl, d), lambda b, j: (b, j, 0)),
            pl.BlockSpec((d, c), lambda b, j: (0, 0)),
            pl.BlockSpec((d, c), lambda b, j: (0, 0)),
            pl.BlockSpec((1, c), lambda b, j: (0, 0)),
            pl.BlockSpec((1, c), lambda b, j: (0, 0)),
        ],
        out_specs=pl.BlockSpec((1, tl, c), lambda b, j: (b, j, 0)),
        out_shape=jax.ShapeDtypeStruct((bsz, lp, c), F32),
        compiler_params=_cparams("parallel", "parallel"),
        name="conf_glu",
    )(h, wa, wg, ba, bg)


def _conv_mix_kernel(ucur_ref, uprev_ref, h_ref, dw_ref, dwb_ref, g1_ref, b1_ref, w2_ref, b2_ref,
                     g2_ref, bb2_ref, o_ref, win_ref, act_ref, *, taps, alpha):
    j = pl.program_id(1)
    tl = ucur_ref.shape[1]

    @pl.when(j == 0)
    def _():
        win_ref[0:CONV_HALO, :] = jnp.zeros((CONV_HALO, win_ref.shape[1]), F32)

    @pl.when(j > 0)
    def _():
        win_ref[0:CONV_HALO, :] = uprev_ref[0, tl - CONV_HALO:tl, :]

    win_ref[CONV_HALO:CONV_HALO + tl, :] = ucur_ref[0]

    def chunk(c, carry):
        base = pl.multiple_of(c * CONV_ROWS, CONV_ROWS)
        n = CONV_HALO + CONV_ROWS
        parts = []
        for lc in range(win_ref.shape[1] // CONV_LANES):
            sl = slice(lc * CONV_LANES, (lc + 1) * CONV_LANES)
            win = win_ref[pl.ds(base, n), sl]
            acc = jnp.zeros((CONV_ROWS, CONV_LANES), F32) + dwb_ref[:, sl]
            for s in range(8):
                ws = win if s == 0 else pltpu.roll(win, n - s, 0)
                for k in range(taps):
                    off = CONV_HALO + k - (taps - 1)
                    if off % 8 == s:
                        acc = acc + ws[off - s:off - s + CONV_ROWS, :] * dw_ref[k:k + 1, sl]
            parts.append(acc)
        y = _layer_norm(jnp.concatenate(parts, axis=1), g1_ref[...], b1_ref[...])
        y = y * _sigmoid(y)
        act_ref[pl.ds(base, CONV_ROWS), :] = y.astype(BF16)
        return carry

    lax.fori_loop(0, tl // CONV_ROWS, chunk, 0)
    mix = jnp.dot(act_ref[...], w2_ref[...], preferred_element_type=F32) + b2_ref[...]
    y = alpha * h_ref[0] + mix
    o_ref[0] = _layer_norm(y, g2_ref[...], bb2_ref[...])


def _conv_mix(u, h, dw, dwb, g1, b1, w2, b2, g2, bb2, alpha, tl):
    bsz, lp, c = u.shape
    d = h.shape[2]
    taps = dw.shape[0]
    assert taps - 1 <= CONV_HALO and tl % CONV_ROWS == 0
    vec = lambda n: pl.BlockSpec((1, n), lambda b, j: (0, 0))
    return pl.pallas_call(
        functools.partial(_conv_mix_kernel, taps=taps, alpha=alpha),
        grid=(bsz, lp // tl),
        in_specs=[
            pl.BlockSpec((1, tl, c), lambda b, j: (b, j, 0)),
            pl.BlockSpec((1, tl, c), lambda b, j: (b, jnp.maximum(j - 1, 0), 0)),
            pl.BlockSpec((1, tl, d), lambda b, j: (b, j, 0)),
            pl.BlockSpec((taps, c), lambda b, j: (0, 0)),
            vec(c), vec(c), vec(c),
            pl.BlockSpec((c, d), lambda b, j: (0, 0)),
            vec(d), vec(d), vec(d),
        ],
        out_specs=pl.BlockSpec((1, tl, d), lambda b, j: (b, j, 0)),
        out_shape=jax.ShapeDtypeStruct((bsz, lp, d), F32),
        scratch_shapes=[pltpu.VMEM((CONV_HALO + tl, c), F32), pltpu.VMEM((tl, c), BF16)],
        compiler_params=_cparams("parallel", "parallel"),
        name="conf_conv_mix",
    )(u, u, h, dw, dwb, g1, b1, w2, b2, g2, bb2)


def _mm_kernel(x_ref, w_ref, o_ref):
    o_ref[...] = jnp.dot(x_ref[...].astype(BF16), w_ref[...], preferred_element_type=F32).astype(o_ref.dtype)


def _mm(x, w, tm, tn, out_dtype=F32):
    m, k = x.shape
    n = w.shape[1]
    return pl.pallas_call(
        _mm_kernel,
        grid=(n // tn, m // tm),
        in_specs=[pl.BlockSpec((tm, k), lambda jn, im: (im, 0)),
                  pl.BlockSpec((k, tn), lambda jn, im: (0, jn))],
        out_specs=pl.BlockSpec((tm, tn), lambda jn, im: (im, jn)),
        out_shape=jax.ShapeDtypeStruct((m, n), out_dtype),
        compiler_params=_cparams("parallel", "parallel"),
        name="mm",
    )(x, w)


def _mm_resid_ln_kernel(x_ref, w_ref, h_ref, g_ref, b_ref, o_ref, *, alpha):
    mix = jnp.dot(x_ref[...].astype(BF16), w_ref[...], preferred_element_type=F32)
    o_ref[...] = _layer_norm(alpha * h_ref[...] + mix, g_ref[...], b_ref[...])


def _mm_resid_ln(x, w, h, g, b, alpha, tm):
    m, k = x.shape
    d = w.shape[1]
    return pl.pallas_call(
        functools.partial(_mm_resid_ln_kernel, alpha=alpha),
        grid=(m // tm,),
        in_specs=[pl.BlockSpec((tm, k), lambda i: (i, 0)),
                  pl.BlockSpec((k, d), lambda i: (0, 0)),
                  pl.BlockSpec((tm, d), lambda i: (i, 0)),
                  pl.BlockSpec((1, d), lambda i: (0, 0)),
                  pl.BlockSpec((1, d), lambda i: (0, 0))],
        out_specs=pl.BlockSpec((tm, d), lambda i: (i, 0)),
        out_shape=jax.ShapeDtypeStruct((m, d), F32),
        compiler_params=_cparams("parallel"),
        name="mm_resid_ln",
    )(x, w, h, g, b)


def _dt_kernel(x_ref, wc_ref, wr_ref, bc_ref, br_ref, ac_ref, ar_ref, dtc_ref, csc_ref, csr_ref, *, pad):
    c = pl.program_id(1)
    x = x_ref[0]
    n = x.shape[0]
    raw_c = jnp.dot(x, wc_ref[...], preferred_element_type=F32, precision=HIGHEST)
    raw_r = lax.dot_general(wr_ref[...], x, (((1,), (1,)), ((), ())),
                            preferred_element_type=F32, precision=HIGHEST)
    row_c = lax.broadcasted_iota(I32, raw_c.shape, 0)
    row_r = lax.broadcasted_iota(I32, raw_r.shape, 1)
    first = c == 0
    dt_c = jnp.where(jnp.logical_and(first, row_c < pad), 0.0, _softplus(raw_c + bc_ref[...]))
    dt_r = jnp.where(jnp.logical_and(first, row_r < pad), 0.0, _softplus(raw_r + br_ref[...]))
    li = lax.broadcasted_iota(I32, (n, n), 0)
    si = lax.broadcasted_iota(I32, (n, n), 1)
    tri = (si <= li).astype(F32)
    tri_t = (li <= si).astype(F32)
    dtc_ref[0] = dt_c
    csc_ref[0] = jnp.dot(tri, dt_c * ac_ref[...], preferred_element_type=F32, precision=HIGHEST)
    csr_ref[0, 0] = jnp.dot(dt_r * ar_ref[...], tri_t, preferred_element_type=F32, precision=HIGHEST)


def _dt_prep(h, w_dt, dt_bias, a_neg, pad):
    bsz, lp, d = h.shape
    nh = w_dt.shape[1]
    nc = lp // SSM_CHUNK
    lanes = 128
    wc = jnp.zeros((d, lanes), F32).at[:, :nh].set(w_dt)
    bc = jnp.zeros((1, lanes), F32).at[0, :nh].set(dt_bias)
    ac = jnp.zeros((1, lanes), F32).at[0, :nh].set(a_neg)
    return pl.pallas_call(
        functools.partial(_dt_kernel, pad=pad),
        grid=(bsz, nc),
        in_specs=[
            pl.BlockSpec((1, SSM_CHUNK, d), lambda b, c: (b, c, 0)),
            pl.BlockSpec((d, lanes), lambda b, c: (0, 0)),
            pl.BlockSpec((nh, d), lambda b, c: (0, 0)),
            pl.BlockSpec((1, lanes), lambda b, c: (0, 0)),
            pl.BlockSpec((nh, 1), lambda b, c: (0, 0)),
            pl.BlockSpec((1, lanes), lambda b, c: (0, 0)),
            pl.BlockSpec((nh, 1), lambda b, c: (0, 0)),
        ],
        out_specs=[
            pl.BlockSpec((1, SSM_CHUNK, lanes), lambda b, c: (b, c, 0)),
            pl.BlockSpec((1, SSM_CHUNK, lanes), lambda b, c: (b, c, 0)),
            pl.BlockSpec((1, 1, nh, SSM_CHUNK), lambda b, c: (b, c, 0, 0)),
        ],
        out_shape=[
            jax.ShapeDtypeStruct((bsz, lp, lanes), F32),
            jax.ShapeDtypeStruct((bsz, lp, lanes), F32),
            jax.ShapeDtypeStruct((bsz, nc, nh, SSM_CHUNK), F32),
        ],
        compiler_params=_cparams("parallel", "parallel"),
        name="ssm_dt",
    )(h, wc, w_dt.T, bc, dt_bias[:, None], ac, a_neg[:, None])


def _ssd_kernel(xs_ref, bm_ref, cm_ref, z_ref, dtc_ref, csc_ref, csr_ref,
                cwx_ref, cwb_ref, cwc_ref, cbx_ref, cbb_ref, cbc_ref, dsk_ref, ng_ref,
                o_ref, state_ref, carx_ref, carb_ref, carc_ref, *, pad, hpg, taps):
    g = pl.program_id(1)
    q = SSM_CHUNK
    hd = SSM_HEAD_DIM
    gw = hpg * hd
    n_chunks = xs_ref.shape[1] // q

    state_ref[...] = jnp.zeros(state_ref.shape, F32)
    carx_ref[...] = jnp.zeros(carx_ref.shape, F32)
    carb_ref[...] = jnp.zeros(carb_ref.shape, F32)
    carc_ref[...] = jnp.zeros(carc_ref.shape, F32)

    hrow = lax.broadcasted_iota(I32, (128, gw), 0)
    hcol = lax.broadcasted_iota(I32, (128, gw), 1) // hd
    sel = (hrow == g * hpg + hcol).astype(F32)
    lane_head = lax.broadcasted_iota(I32, (q, gw), 1) // hd
    li = lax.broadcasted_iota(I32, (q, q), 0)
    si = lax.broadcasted_iota(I32, (q, q), 1)
    causal = si <= li
    row1 = lax.broadcasted_iota(I32, (q, 1), 0)

    def conv_act(raw, car_ref, w_ref, b_ref, valid):
        raw = jnp.where(valid, raw, 0.0)
        win = jnp.concatenate([car_ref[...], raw], axis=0)
        acc = jnp.zeros(raw.shape, F32) + b_ref[...]
        for k in range(taps):
            off = 8 + k - (taps - 1)
            acc = acc + win[off:off + q, :] * w_ref[k:k + 1, :]
        car_ref[...] = raw[q - 8:q, :]
        act = acc * _sigmoid(acc)
        return jnp.where(valid, act, 0.0)

    def chunk(c, carry):
        r0 = pl.multiple_of(c * q, q)
        valid = jnp.logical_or(c > 0, row1 >= pad)
        xs = conv_act(xs_ref[0, pl.ds(r0, q), :], carx_ref, cwx_ref, cbx_ref, valid)
        bm = conv_act(bm_ref[0, pl.ds(r0, q), :], carb_ref, cwb_ref, cbb_ref, valid)
        cm = conv_act(cm_ref[0, pl.ds(r0, q), :], carc_ref, cwc_ref, cbc_ref, valid)

        dt_e = jnp.dot(dtc_ref[0, pl.ds(r0, q), :], sel, preferred_element_type=F32, precision=HIGHEST)
        cs_e = jnp.dot(csc_ref[0, pl.ds(r0, q), :], sel, preferred_element_type=F32, precision=HIGHEST)
        cs_r = csr_ref[0, 0, c]
        a_last = cs_e[q - 1:q, :]

        x_dt = xs * dt_e
        xw = (x_dt * jnp.exp(a_last - cs_e)).astype(BF16)
        bm16 = bm.astype(BF16)
        cm16 = cm.astype(BF16)
        cb = lax.dot_general(cm16, bm16, (((1,), (1,)), ((), ())), preferred_element_type=F32)

        m_parts = []
        x_parts = []
        for r in range(hpg):
            col = cs_e[:, r * hd:r * hd + 1]
            seg = col - cs_r[r:r + 1, :]
            decay = jnp.exp(jnp.where(causal, seg, -jnp.inf))
            m_parts.append((cb * decay).astype(BF16))
            x_parts.append(jnp.where(lane_head == r, x_dt, 0.0).astype(BF16))
        m_cat = jnp.concatenate(m_parts, axis=1)
        x_bd = jnp.concatenate(x_parts, axis=0)
        y = jnp.dot(m_cat, x_bd, preferred_element_type=F32)

        st = state_ref[...]
        y = y + jnp.dot(cm16, st.astype(BF16), preferred_element_type=F32) * jnp.exp(cs_e)
        new = jnp.dot(bm.T.astype(BF16), xw, preferred_element_type=F32)
        state_ref[...] = st * jnp.exp(a_last) + new
        y = y + dsk_ref[...] * xs

        z = z_ref[0, pl.ds(r0, q), :]
        y = y * (z * _sigmoid(z))
        y = y * lax.rsqrt(jnp.mean(y * y, axis=-1, keepdims=True) + RMS_EPS) * ng_ref[...]
        o_ref[0, pl.ds(r0, q), :] = y.astype(o_ref.dtype)
        return carry

    lax.fori_loop(0, n_chunks, chunk, 0)


def _ssd(proj, dtc, csc, csr, conv_w, conv_b, d_exp, norm_g, pad, d_inner):
    bsz, lp, _ = proj.shape
    g = SSM_GROUPS
    n = SSM_STATE
    gw = d_inner // g
    hpg = gw // SSM_HEAD_DIM
    assert gw % 128 == 0 and n == 128 and dtc.shape[2] == 128
    taps = conv_w.shape[0]
    nc = lp // SSM_CHUNK
    zb = 0
    xb = d_inner // gw
    bb = 2 * d_inner // n
    cb_ = (2 * d_inner + g * n) // n
    cxb = 0
    cbb = d_inner // n
    ccb = (d_inner + g * n) // n
    seq = lambda w, off: pl.BlockSpec((1, lp, w), lambda b, j, off=off: (b, 0, off + j))
    cw = lambda w, off: pl.BlockSpec((taps, w), lambda b, j, off=off: (0, off + j))
    cbv = lambda w, off: pl.BlockSpec((1, w), lambda b, j, off=off: (0, off + j))
    return pl.pallas_call(
        functools.partial(_ssd_kernel, pad=pad, hpg=hpg, taps=taps),
        grid=(bsz, g),
        in_specs=[
            seq(gw, xb), seq(n, bb), seq(n, cb_), seq(gw, zb),
            pl.BlockSpec((1, lp, 128), lambda b, j: (b, 0, 0)),
            pl.BlockSpec((1, lp, 128), lambda b, j: (b, 0, 0)),
            pl.BlockSpec((1, 1, nc, hpg, SSM_CHUNK), lambda b, j: (b, j, 0, 0, 0)),
            cw(gw, cxb), cw(n, cbb), cw(n, ccb),
            cbv(gw, cxb), cbv(n, cbb), cbv(n, ccb),
            pl.BlockSpec((1, gw), lambda b, j: (0, j)),
            pl.BlockSpec((1, gw), lambda b, j: (0, j)),
        ],
        out_specs=pl.BlockSpec((1, lp, gw), lambda b, j: (b, 0, j)),
        out_shape=jax.ShapeDtypeStruct((bsz, lp, d_inner), BF16),
        scratch_shapes=[pltpu.VMEM((n, gw), F32), pltpu.VMEM((8, gw), F32),
                        pltpu.VMEM((8, n), F32), pltpu.VMEM((8, n), F32)],
        compiler_params=_cparams("parallel", "parallel"),
        name="ssm_ssd",
    )(proj, proj, proj, proj, dtc, csc, csr, conv_w, conv_w, conv_w, conv_b, conv_b, conv_b, d_exp, norm_g)


def _router_kernel(x_ref, wr_ref, br_ref, e_ref, gate_ref, rank_ref, cnt_ref, carry_ref):
    i = pl.program_id(0)

    @pl.when(i == 0)
    def _():
        carry_ref[...] = jnp.zeros(carry_ref.shape, F32)

    x = x_ref[...]
    tm = x.shape[0]
    ne = wr_ref.shape[0]
    logits = lax.dot_general(wr_ref[...], x, (((1,), (1,)), ((), ())),
                             preferred_element_type=F32, precision=HIGHEST) + br_ref[...]
    ids = lax.broadcasted_iota(I32, (ne, tm), 0)
    vals = logits
    top_v, top_e, sels = [], [], []
    for _ in range(TOP_K):
        m = jnp.max(vals, axis=0, keepdims=True)
        idx = jnp.min(jnp.where(vals == m, ids, ne), axis=0, keepdims=True)
        sel = ids == idx
        top_v.append(m)
        top_e.append(idx)
        sels.append(sel)
        vals = jnp.where(sel, -jnp.inf, vals)
    ex = [jnp.exp(v - top_v[0]) for v in top_v]
    den = ex[0]
    for t in ex[1:]:
        den = den + t
    e_ref[...] = jnp.concatenate(top_e, axis=0)
    gate_ref[...] = jnp.concatenate([t / den for t in ex], axis=0)

    onehot = sels[0].astype(F32)
    for s in sels[1:]:
        onehot = onehot + s.astype(F32)
    ji = lax.broadcasted_iota(I32, (tm, tm), 0)
    ti = lax.broadcasted_iota(I32, (tm, tm), 1)
    before = (ji < ti).astype(BF16)
    prefix = jnp.dot(onehot.astype(BF16), before, preferred_element_type=F32)
    prefix = prefix + carry_ref[:, 0:1]
    ranks = [jnp.sum(jnp.where(s, prefix, 0.0), axis=0, keepdims=True) for s in sels]
    rank_ref[...] = jnp.concatenate(ranks, axis=0).astype(I32)
    carry_ref[...] = carry_ref[...] + jnp.sum(onehot, axis=1, keepdims=True)
    cnt_ref[...] = carry_ref[...].astype(I32)


def _router(t, w_r, b_r, tm):
    nt, d = t.shape
    ne = w_r.shape[1]
    return pl.pallas_call(
        _router_kernel,
        grid=(nt // tm,),
        in_specs=[pl.BlockSpec((tm, d), lambda i: (i, 0)),
                  pl.BlockSpec((ne, d), lambda i: (0, 0)),
                  pl.BlockSpec((ne, 1), lambda i: (0, 0))],
        out_specs=[pl.BlockSpec((TOP_K, tm), lambda i: (0, i)),
                   pl.BlockSpec((TOP_K, tm), lambda i: (0, i)),
                   pl.BlockSpec((TOP_K, tm), lambda i: (0, i)),
                   pl.BlockSpec((ne, 128), lambda i: (0, 0))],
        out_shape=[jax.ShapeDtypeStruct((TOP_K, nt), I32),
                   jax.ShapeDtypeStruct((TOP_K, nt), F32),
                   jax.ShapeDtypeStruct((TOP_K, nt), I32),
                   jax.ShapeDtypeStruct((ne, 128), I32)],
        scratch_shapes=[pltpu.VMEM((ne, 128), F32)],
        compiler_params=_cparams("arbitrary"),
        name="moe_router",
    )(t, w_r.T, b_r[:, None])


def _dispatch_kernel(dest_hbm, t_ref, zeros_hbm, xbuf_hbm, idx_ref, rows_ref, isem, sem, *, tm, nseg):
    del zeros_hbm
    i = pl.program_id(0)
    n = TOP_K * tm
    cp = pltpu.make_async_copy(dest_hbm.at[pl.ds(pl.multiple_of(i * n, n), n)], idx_ref, isem)
    cp.start()
    x = t_ref[...]
    for c in range(nseg):
        rows_ref[pl.ds(c, tm, stride=nseg), :] = x[:, c * LANES:(c + 1) * LANES]
    cp.wait()

    def row_copy(k, r):
        src = rows_ref.at[pl.ds(pl.multiple_of(r * nseg, nseg), nseg), :]
        dst = xbuf_hbm.at[pl.ds(pl.multiple_of(idx_ref[k * tm + r] * nseg, nseg), nseg), :]
        return pltpu.make_async_copy(src, dst, sem)

    def issue(r, carry):
        for k in range(TOP_K):
            row_copy(k, r).start()
        return carry

    def drain(r, carry):
        for k in range(TOP_K):
            row_copy(k, r).wait()
        return carry

    lax.fori_loop(0, tm, issue, 0, unroll=DMA_UNROLL)
    lax.fori_loop(0, tm, drain, 0, unroll=DMA_UNROLL)


def _dispatch(t, dest_tiles, n_rows, tm):
    nt, d = t.shape
    nseg = d // LANES
    zeros = jnp.zeros((n_rows * nseg, LANES), t.dtype)
    return pl.pallas_call(
        functools.partial(_dispatch_kernel, tm=tm, nseg=nseg),
        grid=(nt // tm,),
        in_specs=[pl.BlockSpec(memory_space=pl.ANY),
                  pl.BlockSpec((tm, d), lambda i: (i, 0)),
                  pl.BlockSpec(memory_space=pl.ANY)],
        out_specs=pl.BlockSpec(memory_space=pl.ANY),
        out_shape=jax.ShapeDtypeStruct((n_rows * nseg, LANES), t.dtype),
        scratch_shapes=[pltpu.SMEM((TOP_K * tm,), I32), pltpu.VMEM((tm * nseg, LANES), t.dtype),
                        pltpu.SemaphoreType.DMA(()), pltpu.SemaphoreType.DMA(())],
        input_output_aliases={2: 0},
        compiler_params=pltpu.CompilerParams(dimension_semantics=("arbitrary",), has_side_effects=True,
                                             vmem_limit_bytes=VMEM_LIMIT),
        name="moe_dispatch",
    )(dest_tiles, t, zeros)


def _ffn_kernel(blk_e_ref, nused_ref, x_ref, w1_ref, b1g_ref, b1l_ref, w2_ref, b2_ref, y_ref,
                w1g_ref, w1l_ref, w2c_ref, *, nseg):
    i = pl.program_id(0)
    rows = x_ref.shape[0] // nseg
    e = blk_e_ref[i]
    e_prev = blk_e_ref[jnp.maximum(i - 1, 0)]

    @pl.when(jnp.logical_or(i == 0, e != e_prev))
    def _():
        w = 2 * LANES
        r = lax.broadcasted_iota(I32, (w, w), 0)
        c = lax.broadcasted_iota(I32, (w, w), 1)
        src_col = jnp.where(c < LANES, 2 * c, 2 * (c - LANES) + 1)
        perm = jnp.where(r == src_col, 1.0, 0.0).astype(BF16)
        for cc in range(w1_ref.shape[2] // w):
            chunk = w1_ref[0, :, cc * w:(cc + 1) * w].astype(BF16)
            res = jnp.dot(chunk, perm, preferred_element_type=F32).astype(BF16)
            w1g_ref[:, cc * LANES:(cc + 1) * LANES] = res[:, :LANES]
            w1l_ref[:, cc * LANES:(cc + 1) * LANES] = res[:, LANES:]
        w2c_ref[...] = w2_ref[0].astype(BF16)

    @pl.when(i < nused_ref[0])
    def _():
        x = jnp.concatenate([x_ref[pl.ds(c, rows, stride=nseg), :] for c in range(nseg)], axis=1).astype(BF16)
        hg = jnp.dot(x, w1g_ref[...], preferred_element_type=F32) + b1g_ref[0]
        hl = jnp.dot(x, w1l_ref[...], preferred_element_type=F32) + b1l_ref[0]
        xg = jnp.minimum(hg, SWIGLU_LIMIT)
        xl = jnp.clip(hl, -SWIGLU_LIMIT, SWIGLU_LIMIT)
        act = xg * _sigmoid(SWIGLU_ALPHA * xg) * (xl + 1.0)
        y = jnp.dot(act.astype(BF16), w2c_ref[...], preferred_element_type=F32) + b2_ref[0]
        for c in range(nseg):
            y_ref[pl.ds(c, rows, stride=nseg), :] = y[:, c * LANES:(c + 1) * LANES]

    @pl.when(i >= nused_ref[0])
    def _():
        y_ref[...] = jnp.zeros(y_ref.shape, F32)


def _ffn(x_buf, blk_e, n_used, w1, b1g, b1l, w2, b2):
    ne, d, f2 = w1.shape
    f = f2 // 2
    nseg = d // LANES
    n_rows = x_buf.shape[0] // nseg
    n_blk = n_rows // MOE_ROWS
    assert f2 % (2 * LANES) == 0
    wspec = lambda a, b: pl.BlockSpec((1, a, b), lambda i, be, nu: (be[i], 0, 0))
    return pl.pallas_call(
        functools.partial(_ffn_kernel, nseg=nseg),
        grid_spec=pltpu.PrefetchScalarGridSpec(
            num_scalar_prefetch=2,
            grid=(n_blk,),
            in_specs=[pl.BlockSpec((MOE_ROWS * nseg, LANES), lambda i, be, nu: (i, 0)),
                      wspec(d, f2), wspec(1, f), wspec(1, f), wspec(f, d), wspec(1, d)],
            out_specs=pl.BlockSpec((MOE_ROWS * nseg, LANES), lambda i, be, nu: (i, 0)),
            scratch_shapes=[pltpu.VMEM((d, f), BF16), pltpu.VMEM((d, f), BF16), pltpu.VMEM((f, d), BF16)],
        ),
        out_shape=jax.ShapeDtypeStruct((n_rows * nseg, LANES), F32),
        compiler_params=_cparams("arbitrary"),
        name="moe_ffn",
    )(blk_e, n_used, x_buf, w1, b1g, b1l, w2, b2)


def _combine_kernel(dest_hbm, ybuf_hbm, gate_ref, h_ref, g_ref, b_ref, o_ref, idx_ref, rows_ref, isem, sem,
                    *, tm, nseg, alpha):
    i = pl.program_id(0)
    n = TOP_K * tm
    cp = pltpu.make_async_copy(dest_hbm.at[pl.ds(pl.multiple_of(i * n, n), n)], idx_ref, isem)
    cp.start()
    cp.wait()

    def row_copy(k, r):
        src = ybuf_hbm.at[pl.ds(pl.multiple_of(idx_ref[k * tm + r] * nseg, nseg), nseg), :]
        dst = rows_ref.at[k, pl.ds(pl.multiple_of(r * nseg, nseg), nseg), :]
        return pltpu.make_async_copy(src, dst, sem)

    def issue(r, carry):
        for k in range(TOP_K):
            row_copy(k, r).start()
        return carry

    def drain(r, carry):
        for k in range(TOP_K):
            row_copy(k, r).wait()
        return carry

    lax.fori_loop(0, tm, issue, 0, unroll=DMA_UNROLL)
    lax.fori_loop(0, tm, drain, 0, unroll=DMA_UNROLL)
    gates = gate_ref[...]
    y = alpha * h_ref[...]
    for k in range(TOP_K):
        rows = jnp.concatenate([rows_ref[k, pl.ds(c, tm, stride=nseg), :] for c in range(nseg)], axis=1)
        y = y + gates[:, k:k + 1] * rows
    o_ref[...] = _layer_norm(y, g_ref[...], b_ref[...])


def _combine(y_buf, dest_tiles, gates_t, h, g, b, alpha, tm):
    nt, d = h.shape
    nseg = d // LANES
    return pl.pallas_call(
        functools.partial(_combine_kernel, tm=tm, nseg=nseg, alpha=alpha),
        grid=(nt // tm,),
        in_specs=[pl.BlockSpec(memory_space=pl.ANY),
                  pl.BlockSpec(memory_space=pl.ANY),
                  pl.BlockSpec((tm, TOP_K), lambda i: (i, 0)),
                  pl.BlockSpec((tm, d), lambda i: (i, 0)),
                  pl.BlockSpec((1, d), lambda i: (0, 0)),
                  pl.BlockSpec((1, d), lambda i: (0, 0))],
        out_specs=pl.BlockSpec((tm, d), lambda i: (i, 0)),
        out_shape=jax.ShapeDtypeStruct((nt, d), F32),
        scratch_shapes=[pltpu.SMEM((TOP_K * tm,), I32), pltpu.VMEM((TOP_K, tm * nseg, LANES), F32),
                        pltpu.SemaphoreType.DMA(()), pltpu.SemaphoreType.DMA(())],
        compiler_params=_cparams("arbitrary"),
        name="moe_combine",
    )(dest_tiles, y_buf, gates_t, h, g, b)


def _moe_layer(t, w_r, b_r, w1, b1, w2, b2, ln_g, ln_b, alpha):
    nt, d = t.shape
    ne = w_r.shape[1]
    tm_r = _pick(nt, (512, 256, 128))
    tm_d = _pick(nt, (512, 256, 128))
    top_e, gates, rank, cnt = _router(t, w_r, b_r, tm_r)

    counts = cnt[:, 0]
    padded = ((counts + MOE_ROWS - 1) // MOE_ROWS) * MOE_ROWS
    pend = jnp.cumsum(padded)
    poff = pend - padded
    expert_ids = jnp.arange(ne, dtype=I32)[:, None, None]
    dest = jnp.sum(jnp.where(top_e[None] == expert_ids, poff[:, None, None], 0), axis=0) + rank
    n_rows = nt * TOP_K + ne * MOE_ROWS
    n_blk = n_rows // MOE_ROWS
    blk_start = jnp.arange(n_blk, dtype=I32) * MOE_ROWS
    blk_e = jnp.minimum(jnp.sum(blk_start[:, None] >= pend[None, :], axis=1), ne - 1).astype(I32)
    n_used = (pend[-1:] // MOE_ROWS).astype(I32)
    dest_tiles = dest.astype(I32).reshape(TOP_K, nt // tm_d, tm_d).transpose(1, 0, 2).reshape(-1)

    x_buf = _dispatch(t, dest_tiles, n_rows, tm_d)
    y_buf = _ffn(x_buf, blk_e, n_used, w1, b1[:, None, 0::2], b1[:, None, 1::2], w2, b2[:, None, :])
    return _combine(y_buf, dest_tiles, gates.T, t, ln_g[None, :], ln_b[None, :], alpha, tm_d)


def _conformer_layer(h, w1, b1, dw, dwb, ln_g, ln_b, w2, b2, mix_g, mix_b, alpha, pad):
    bsz, lp, d = h.shape
    c = dw.shape[1]
    tl = _pick(lp, (544, 272, 384, 192, 128))
    u = _glu(h, w1[:, :c].astype(BF16), w1[:, c:].astype(BF16), b1[None, :c], b1[None, c:], pad, tl)
    return _conv_mix(u, h, dw, dwb[None, :], ln_g[None, :], ln_b[None, :], w2.astype(BF16), b2[None, :],
                     mix_g[None, :], mix_b[None, :], alpha, tl)


def _mamba_layer(h, w_in, conv_w, conv_b, dt_bias, a_log, d_skip, norm_g, w_out, mix_g, mix_b, alpha, pad):
    bsz, lp, d = h.shape
    nt = bsz * lp
    d_inner = w_out.shape[0]
    nh = dt_bias.shape[0]
    conv_dim = conv_w.shape[1]
    nproj = d_inner + conv_dim
    hpg = nh // SSM_GROUPS
    tm = _pick(nt, (1024, 512, 256, 128))
    proj = _mm(h.reshape(nt, d), w_in[:, :nproj].astype(BF16), tm, _pick(nproj, (2048, 1024, 512, 256, 128)))
    a_neg = -jnp.exp(a_log.astype(F32))
    dtc, csc, csr = _dt_prep(h, w_in[:, nproj:], dt_bias, a_neg, pad)
    nc = lp // SSM_CHUNK
    csr = csr.reshape(bsz, nc, SSM_GROUPS, hpg, SSM_CHUNK).transpose(0, 2, 1, 3, 4)
    d_exp = jnp.repeat(d_skip.astype(F32), SSM_HEAD_DIM)[None, :]
    y = _ssd(proj.reshape(bsz, lp, nproj), dtc, csc, csr, conv_w, conv_b[None, :], d_exp, norm_g[None, :],
             pad, d_inner)
    out = _mm_resid_ln(y.reshape(nt, d_inner), w_out.astype(BF16), h.reshape(nt, d),
                       mix_g[None, :], mix_b[None, :], alpha, _pick(nt, (512, 256, 128)))
    return out.reshape(bsz, lp, d)


def kernel(x, meta_tokens, conv_w1, conv_b1, conv_dw, conv_dwb, conv_ln_g, conv_ln_b, conv_w2, conv_b2,
           ssm_w_in, ssm_conv_w, ssm_conv_b, ssm_dt_bias, ssm_a_log, ssm_d, ssm_norm_g, ssm_w_out,
           moe_w_router, moe_b_router, moe_w1, moe_b1, moe_w2, moe_b2,
           ln_mix_g, ln_mix_b, ln_ffn_g, ln_ffn_b):
    bsz, seq, d = x.shape
    depth = ln_mix_g.shape[0]
    alpha = (2 * depth) ** 0.25
    pad = SSM_CHUNK - N_META
    lp = pad + N_META + seq
    assert lp % SSM_CHUNK == 0
    meta = jnp.broadcast_to(meta_tokens.astype(x.dtype)[None], (bsz, N_META, d))
    h = jnp.concatenate([jnp.zeros((bsz, pad, d), x.dtype), meta, x], axis=1)
    for i in range(depth):
        j = i // 2
        if i % 2 == 0:
            h = _conformer_layer(h, conv_w1[j], conv_b1[j], conv_dw[j], conv_dwb[j], conv_ln_g[j],
                                 conv_ln_b[j], conv_w2[j], conv_b2[j], ln_mix_g[i], ln_mix_b[i], alpha, pad)
        else:
            h = _mamba_layer(h, ssm_w_in[j], ssm_conv_w[j], ssm_conv_b[j], ssm_dt_bias[j], ssm_a_log[j],
                             ssm_d[j], ssm_norm_g[j], ssm_w_out[j], ln_mix_g[i], ln_mix_b[i], alpha, pad)
        t = _moe_layer(h.reshape(bsz * lp, d), moe_w_router[i], moe_b_router[i], moe_w1[i], moe_b1[i],
                       moe_w2[i], moe_b2[i], ln_ffn_g[i], ln_ffn_b[i], alpha)
        h = t.reshape(bsz, lp, d)
    return h[:, pad + N_META:]
```

```python
import functools

import jax
import jax.numpy as jnp
from jax import lax
from jax.experimental import pallas as pl
from jax.experimental.pallas import tpu as pltpu

F32 = jnp.float32
BF16 = jnp.bfloat16
I32 = jnp.int32
HIGHEST = lax.Precision.HIGHEST

N_META = 16
SSM_HEAD_DIM = 64
SSM_GROUPS = 8
SSM_STATE = 128
SSM_CHUNK = 128
TOP_K = 4
SWIGLU_ALPHA = 1.702
SWIGLU_LIMIT = 7.0
LN_EPS = 1e-5
RMS_EPS = 1e-5

CONV_HALO = 32
CONV_ROWS = 32
CONV_LANES = 256
MOE_ROWS = 256
DMA_UNROLL = 8
DMA_QUEUES = 2
LANES = 128
VMEM_LIMIT = 56 * 1024 * 1024


def _cparams(*sem):
    return pltpu.CompilerParams(dimension_semantics=tuple(sem), vmem_limit_bytes=VMEM_LIMIT)


def _pick(n, prefs):
    for p in prefs:
        if n % p == 0:
            return p
    raise ValueError(f"no tile for {n} in {prefs}")


def _layer_norm(y, g, b):
    mu = jnp.mean(y, axis=-1, keepdims=True)
    yc = y - mu
    var = jnp.mean(yc * yc, axis=-1, keepdims=True)
    return yc * lax.rsqrt(var + LN_EPS) * g + b


def _sigmoid(x):
    return 1.0 / (1.0 + jnp.exp(-x))


def _softplus(x):
    return jnp.maximum(x, 0.0) + jnp.log(1.0 + jnp.exp(-jnp.abs(x)))


def _glu_kernel(x_ref, wa_ref, wg_ref, ba_ref, bg_ref, o_ref, *, pad):
    j = pl.program_id(1)
    x = x_ref[0].astype(BF16)
    a = jnp.dot(x, wa_ref[...], preferred_element_type=F32) + ba_ref[...]
    g = jnp.dot(x, wg_ref[...], preferred_element_type=F32) + bg_ref[...]
    u = a * _sigmoid(g)
    row = lax.broadcasted_iota(I32, u.shape, 0)
    valid = jnp.logical_or(j > 0, row >= pad)
    o_ref[0] = jnp.where(valid, u, 0.0)


def _glu(h, wa, wg, ba, bg, pad, tl):
    bsz, lp, d = h.shape
    c = wa.shape[1]
    return pl.pallas_call(
        functools.partial(_glu_kernel, pad=pad),
        grid=(bsz, lp // tl),
        in_specs=[
            pl.BlockSpec((1, tl, d), lambda b, j: (b, j, 0)),
            pl.BlockSpec((d, c), lambda b, j: (0, 0)),
            pl.BlockSpec((d, c), lambda b, j: (0, 0)),
            pl.BlockSpec((1, c), lambda b, j: (0, 0)),
            pl.BlockSpec((1, c), lambda b, j: (0, 0)),
        ],
        out_specs=pl.BlockSpec((1, tl, c), lambda b, j: (b, j, 0)),
        out_shape=jax.ShapeDtypeStruct((bsz, lp, c), F32),
        compiler_params=_cparams("parallel", "parallel"),
        name="conf_glu",
    )(h, wa, wg, ba, bg)


def _conv_mix_kernel(ucur_ref, uprev_ref, h_ref, dw_ref, dwb_ref, g1_ref, b1_ref, w2_ref, b2_ref,
                     g2_ref, bb2_ref, o_ref, win_ref, act_ref, *, taps, alpha):
    j = pl.program_id(1)
    tl = ucur_ref.shape[1]

    @pl.when(j == 0)
    def _():
        win_ref[0:CONV_HALO, :] = jnp.zeros((CONV_HALO, win_ref.shape[1]), F32)

    @pl.when(j > 0)
    def _():
        win_ref[0:CONV_HALO, :] = uprev_ref[0, tl - CONV_HALO:tl, :]

    win_ref[CONV_HALO:CONV_HALO + tl, :] = ucur_ref[0]

    def chunk(c, carry):
        base = pl.multiple_of(c * CONV_ROWS, CONV_ROWS)
        n = CONV_HALO + CONV_ROWS
        parts = []
        for lc in range(win_ref.shape[1] // CONV_LANES):
            sl = slice(lc * CONV_LANES, (lc + 1) * CONV_LANES)
            win = win_ref[pl.ds(base, n), sl]
            acc = jnp.zeros((CONV_ROWS, CONV_LANES), F32) + dwb_ref[:, sl]
            for s in range(8):
                ws = win if s == 0 else pltpu.roll(win, n - s, 0)
                for k in range(taps):
                    off = CONV_HALO + k - (taps - 1)
                    if off % 8 == s:
                        acc = acc + ws[off - s:off - s + CONV_ROWS, :] * dw_ref[k:k + 1, sl]
            parts.append(acc)
        y = _layer_norm(jnp.concatenate(parts, axis=1), g1_ref[...], b1_ref[...])
        y = y * _sigmoid(y)
        act_ref[pl.ds(base, CONV_ROWS), :] = y.astype(BF16)
        return carry

    lax.fori_loop(0, tl // CONV_ROWS, chunk, 0)
    mix = jnp.dot(act_ref[...], w2_ref[...], preferred_element_type=F32) + b2_ref[...]
    y = alpha * h_ref[0] + mix
    o_ref[0] = _layer_norm(y, g2_ref[...], bb2_ref[...])


def _conv_mix(u, h, dw, dwb, g1, b1, w2, b2, g2, bb2, alpha, tl):
    bsz, lp, c = u.shape
    d = h.shape[2]
    taps = dw.shape[0]
    assert taps - 1 <= CONV_HALO and tl % CONV_ROWS == 0
    vec = lambda n: pl.BlockSpec((1, n), lambda b, j: (0, 0))
    return pl.pallas_call(
        functools.partial(_conv_mix_kernel, taps=taps, alpha=alpha),
        grid=(bsz, lp // tl),
        in_specs=[
            pl.BlockSpec((1, tl, c), lambda b, j: (b, j, 0)),
            pl.BlockSpec((1, tl, c), lambda b, j: (b, jnp.maximum(j - 1, 0), 0)),
            pl.BlockSpec((1, tl, d), lambda b, j: (b, j, 0)),
            pl.BlockSpec((taps, c), lambda b, j: (0, 0)),
            vec(c), vec(c), vec(c),
            pl.BlockSpec((c, d), lambda b, j: (0, 0)),
            vec(d), vec(d), vec(d),
        ],
        out_specs=pl.BlockSpec((1, tl, d), lambda b, j: (b, j, 0)),
        out_shape=jax.ShapeDtypeStruct((bsz, lp, d), F32),
        scratch_shapes=[pltpu.VMEM((CONV_HALO + tl, c), F32), pltpu.VMEM((tl, c), BF16)],
        compiler_params=_cparams("parallel", "parallel"),
        name="conf_conv_mix",
    )(u, u, h, dw, dwb, g1, b1, w2, b2, g2, bb2)


def _mm_kernel(x_ref, w_ref, o_ref):
    o_ref[...] = jnp.dot(x_ref[...].astype(BF16), w_ref[...], preferred_element_type=F32).astype(o_ref.dtype)


def _mm(x, w, tm, tn, out_dtype=F32):
    m, k = x.shape
    n = w.shape[1]
    return pl.pallas_call(
        _mm_kernel,
        grid=(n // tn, m // tm),
        in_specs=[pl.BlockSpec((tm, k), lambda jn, im: (im, 0)),
                  pl.BlockSpec((k, tn), lambda jn, im: (0, jn))],
        out_specs=pl.BlockSpec((tm, tn), lambda jn, im: (im, jn)),
        out_shape=jax.ShapeDtypeStruct((m, n), out_dtype),
        compiler_params=_cparams("parallel", "parallel"),
        name="mm",
    )(x, w)


def _mm_resid_ln_kernel(x_ref, w_ref, h_ref, g_ref, b_ref, o_ref, *, alpha):
    mix = jnp.dot(x_ref[...].astype(BF16), w_ref[...], preferred_element_type=F32)
    o_ref[...] = _layer_norm(alpha * h_ref[...] + mix, g_ref[...], b_ref[...])


def _mm_resid_ln(x, w, h, g, b, alpha, tm):
    m, k = x.shape
    d = w.shape[1]
    return pl.pallas_call(
        functools.partial(_mm_resid_ln_kernel, alpha=alpha),
        grid=(m // tm,),
        in_specs=[pl.BlockSpec((tm, k), lambda i: (i, 0)),
                  pl.BlockSpec((k, d), lambda i: (0, 0)),
                  pl.BlockSpec((tm, d), lambda i: (i, 0)),
                  pl.BlockSpec((1, d), lambda i: (0, 0)),
                  pl.BlockSpec((1, d), lambda i: (0, 0))],
        out_specs=pl.BlockSpec((tm, d), lambda i: (i, 0)),
        out_shape=jax.ShapeDtypeStruct((m, d), F32),
        compiler_params=_cparams("parallel"),
        name="mm_resid_ln",
    )(x, w, h, g, b)


def _dt_kernel(x_ref, wc_ref, wr_ref, bc_ref, br_ref, ac_ref, ar_ref, dtc_ref, csc_ref, csr_ref, *, pad):
    c = pl.program_id(1)
    x = x_ref[0]
    n = x.shape[0]
    raw_c = jnp.dot(x, wc_ref[...], preferred_element_type=F32, precision=HIGHEST)
    raw_r = lax.dot_general(wr_ref[...], x, (((1,), (1,)), ((), ())),
                            preferred_element_type=F32, precision=HIGHEST)
    row_c = lax.broadcasted_iota(I32, raw_c.shape, 0)
    row_r = lax.broadcasted_iota(I32, raw_r.shape, 1)
    first = c == 0
    dt_c = jnp.where(jnp.logical_and(first, row_c < pad), 0.0, _softplus(raw_c + bc_ref[...]))
    dt_r = jnp.where(jnp.logical_and(first, row_r < pad), 0.0, _softplus(raw_r + br_ref[...]))
    li = lax.broadcasted_iota(I32, (n, n), 0)
    si = lax.broadcasted_iota(I32, (n, n), 1)
    tri = (si <= li).astype(F32)
    tri_t = (li <= si).astype(F32)
    dtc_ref[0] = dt_c
    csc_ref[0] = jnp.dot(tri, dt_c * ac_ref[...], preferred_element_type=F32, precision=HIGHEST)
    csr_ref[0, 0] = jnp.dot(dt_r * ar_ref[...], tri_t, preferred_element_type=F32, precision=HIGHEST)


def _dt_prep(h, w_dt, dt_bias, a_neg, pad):
    bsz, lp, d = h.shape
    nh = w_dt.shape[1]
    nc = lp // SSM_CHUNK
    lanes = 128
    wc = jnp.zeros((d, lanes), F32).at[:, :nh].set(w_dt)
    bc = jnp.zeros((1, lanes), F32).at[0, :nh].set(dt_bias)
    ac = jnp.zeros((1, lanes), F32).at[0, :nh].set(a_neg)
    return pl.pallas_call(
        functools.partial(_dt_kernel, pad=pad),
        grid=(bsz, nc),
        in_specs=[
            pl.BlockSpec((1, SSM_CHUNK, d), lambda b, c: (b, c, 0)),
            pl.BlockSpec((d, lanes), lambda b, c: (0, 0)),
            pl.BlockSpec((nh, d), lambda b, c: (0, 0)),
            pl.BlockSpec((1, lanes), lambda b, c: (0, 0)),
            pl.BlockSpec((nh, 1), lambda b, c: (0, 0)),
            pl.BlockSpec((1, lanes), lambda b, c: (0, 0)),
            pl.BlockSpec((nh, 1), lambda b, c: (0, 0)),
        ],
        out_specs=[
            pl.BlockSpec((1, SSM_CHUNK, lanes), lambda b, c: (b, c, 0)),
            pl.BlockSpec((1, SSM_CHUNK, lanes), lambda b, c: (b, c, 0)),
            pl.BlockSpec((1, 1, nh, SSM_CHUNK), lambda b, c: (b, c, 0, 0)),
        ],
        out_shape=[
            jax.ShapeDtypeStruct((bsz, lp, lanes), F32),
            jax.ShapeDtypeStruct((bsz, lp, lanes), F32),
            jax.ShapeDtypeStruct((bsz, nc, nh, SSM_CHUNK), F32),
        ],
        compiler_params=_cparams("parallel", "parallel"),
        name="ssm_dt",
    )(h, wc, w_dt.T, bc, dt_bias[:, None], ac, a_neg[:, None])


def _ssd_kernel(xs_ref, bm_ref, cm_ref, z_ref, dtc_ref, csc_ref, csr_ref,
                cwx_ref, cwb_ref, cwc_ref, cbx_ref, cbb_ref, cbc_ref, dsk_ref, ng_ref,
                o_ref, state_ref, carx_ref, carb_ref, carc_ref, *, pad, hpg, taps):
    g = pl.program_id(1)
    q = SSM_CHUNK
    hd = SSM_HEAD_DIM
    gw = hpg * hd
    n_chunks = xs_ref.shape[1] // q

    state_ref[...] = jnp.zeros(state_ref.shape, F32)
    carx_ref[...] = jnp.zeros(carx_ref.shape, F32)
    carb_ref[...] = jnp.zeros(carb_ref.shape, F32)
    carc_ref[...] = jnp.zeros(carc_ref.shape, F32)

    hrow = lax.broadcasted_iota(I32, (128, gw), 0)
    hcol = lax.broadcasted_iota(I32, (128, gw), 1) // hd
    sel = jnp.where(hrow == g * hpg + hcol, 1.0, 0.0).astype(BF16)

    def select_heads(v):
        hi = v.astype(BF16)
        r1 = v - hi.astype(F32)
        mid = r1.astype(BF16)
        lo = (r1 - mid.astype(F32)).astype(BF16)
        out = jnp.dot(hi, sel, preferred_element_type=F32)
        out = out + jnp.dot(mid, sel, preferred_element_type=F32)
        return out + jnp.dot(lo, sel, preferred_element_type=F32)

    lane_head = lax.broadcasted_iota(I32, (q, gw), 1) // hd
    li = lax.broadcasted_iota(I32, (q, q), 0)
    si = lax.broadcasted_iota(I32, (q, q), 1)
    causal = si <= li
    row1 = lax.broadcasted_iota(I32, (q, 1), 0)

    def conv_act(raw, car_ref, w_ref, b_ref, valid):
        raw = jnp.where(valid, raw, 0.0)
        win = jnp.concatenate([car_ref[...], raw], axis=0)
        acc = jnp.zeros(raw.shape, F32) + b_ref[...]
        for k in range(taps):
            off = 8 + k - (taps - 1)
            acc = acc + win[off:off + q, :] * w_ref[k:k + 1, :]
        car_ref[...] = raw[q - 8:q, :]
        act = acc * _sigmoid(acc)
        return jnp.where(valid, act, 0.0)

    def chunk(c, carry):
        r0 = pl.multiple_of(c * q, q)
        valid = jnp.logical_or(c > 0, row1 >= pad)
        xs = conv_act(xs_ref[0, pl.ds(r0, q), :], carx_ref, cwx_ref, cbx_ref, valid)
        bm = conv_act(bm_ref[0, pl.ds(r0, q), :], carb_ref, cwb_ref, cbb_ref, valid)
        cm = conv_act(cm_ref[0, pl.ds(r0, q), :], carc_ref, cwc_ref, cbc_ref, valid)

        dt_e = select_heads(dtc_ref[0, pl.ds(r0, q), :])
        cs_e = select_heads(csc_ref[0, pl.ds(r0, q), :])
        cs_r = csr_ref[0, 0, c]
        a_last = cs_e[q - 1:q, :]

        x_dt = xs * dt_e
        xw = (x_dt * jnp.exp(a_last - cs_e)).astype(BF16)
        bm16 = bm.astype(BF16)
        cm16 = cm.astype(BF16)
        cb = lax.dot_general(cm16, bm16, (((1,), (1,)), ((), ())), preferred_element_type=F32)

        m_parts = []
        x_parts = []
        for r in range(hpg):
            col = cs_e[:, r * hd:r * hd + 1]
            seg = col - cs_r[r:r + 1, :]
            decay = jnp.exp(jnp.where(causal, seg, -jnp.inf))
            m_parts.append((cb * decay).astype(BF16))
            x_parts.append(jnp.where(lane_head == r, x_dt, 0.0).astype(BF16))
        m_cat = jnp.concatenate(m_parts, axis=1)
        x_bd = jnp.concatenate(x_parts, axis=0)
        y = jnp.dot(m_cat, x_bd, preferred_element_type=F32)

        st = state_ref[...]
        y = y + jnp.dot(cm16, st.astype(BF16), preferred_element_type=F32) * jnp.exp(cs_e)
        new = jnp.dot(bm.T.astype(BF16), xw, preferred_element_type=F32)
        state_ref[...] = st * jnp.exp(a_last) + new
        y = y + dsk_ref[...] * xs

        z = z_ref[0, pl.ds(r0, q), :]
        y = y * (z * _sigmoid(z))
        y = y * lax.rsqrt(jnp.mean(y * y, axis=-1, keepdims=True) + RMS_EPS) * ng_ref[...]
        o_ref[0, pl.ds(r0, q), :] = y.astype(o_ref.dtype)
        return carry

    lax.fori_loop(0, n_chunks, chunk, 0)


def _ssd(proj, dtc, csc, csr, conv_w, conv_b, d_exp, norm_g, pad, d_inner):
    bsz, lp, _ = proj.shape
    g = SSM_GROUPS
    n = SSM_STATE
    gw = d_inner // g
    hpg = gw // SSM_HEAD_DIM
    assert gw % 128 == 0 and n == 128 and dtc.shape[2] == 128
    taps = conv_w.shape[0]
    nc = lp // SSM_CHUNK
    zb = 0
    xb = d_inner // gw
    bb = 2 * d_inner // n
    cb_ = (2 * d_inner + g * n) // n
    cxb = 0
    cbb = d_inner // n
    ccb = (d_inner + g * n) // n
    seq = lambda w, off: pl.BlockSpec((1, lp, w), lambda b, j, off=off: (b, 0, off + j))
    cw = lambda w, off: pl.BlockSpec((taps, w), lambda b, j, off=off: (0, off + j))
    cbv = lambda w, off: pl.BlockSpec((1, w), lambda b, j, off=off: (0, off + j))
    return pl.pallas_call(
        functools.partial(_ssd_kernel, pad=pad, hpg=hpg, taps=taps),
        grid=(bsz, g),
        in_specs=[
            seq(gw, xb), seq(n, bb), seq(n, cb_), seq(gw, zb),
            pl.BlockSpec((1, lp, 128), lambda b, j: (b, 0, 0)),
            pl.BlockSpec((1, lp, 128), lambda b, j: (b, 0, 0)),
            pl.BlockSpec((1, 1, nc, hpg, SSM_CHUNK), lambda b, j: (b, j, 0, 0, 0)),
            cw(gw, cxb), cw(n, cbb), cw(n, ccb),
            cbv(gw, cxb), cbv(n, cbb), cbv(n, ccb),
            pl.BlockSpec((1, gw), lambda b, j: (0, j)),
            pl.BlockSpec((1, gw), lambda b, j: (0, j)),
        ],
        out_specs=pl.BlockSpec((1, lp, gw), lambda b, j: (b, 0, j)),
        out_shape=jax.ShapeDtypeStruct((bsz, lp, d_inner), BF16),
        scratch_shapes=[pltpu.VMEM((n, gw), F32), pltpu.VMEM((8, gw), F32),
                        pltpu.VMEM((8, n), F32), pltpu.VMEM((8, n), F32)],
        compiler_params=_cparams("parallel", "parallel"),
        name="ssm_ssd",
    )(proj, proj, proj, proj, dtc, csc, csr, conv_w, conv_w, conv_w, conv_b, conv_b, conv_b, d_exp, norm_g)


def _router_kernel(x_ref, wr_ref, br_ref, e_ref, gate_ref, rank_ref, cnt_ref, carry_ref):
    i = pl.program_id(0)

    @pl.when(i == 0)
    def _():
        carry_ref[...] = jnp.zeros(carry_ref.shape, F32)

    x = x_ref[...]
    tm = x.shape[0]
    ne = wr_ref.shape[0]
    logits = lax.dot_general(wr_ref[...], x, (((1,), (1,)), ((), ())),
                             preferred_element_type=F32, precision=HIGHEST) + br_ref[...]
    ids = lax.broadcasted_iota(I32, (ne, tm), 0)
    vals = logits
    top_v, top_e, sels = [], [], []
    for _ in range(TOP_K):
        m = jnp.max(vals, axis=0, keepdims=True)
        idx = jnp.min(jnp.where(vals == m, ids, ne), axis=0, keepdims=True)
        sel = ids == idx
        top_v.append(m)
        top_e.append(idx)
        sels.append(sel)
        vals = jnp.where(sel, -jnp.inf, vals)
    ex = [jnp.exp(v - top_v[0]) for v in top_v]
    den = ex[0]
    for t in ex[1:]:
        den = den + t
    e_ref[...] = jnp.concatenate(top_e, axis=0)
    gate_ref[...] = jnp.concatenate([t / den for t in ex], axis=0)

    onehot = sels[0].astype(F32)
    for s in sels[1:]:
        onehot = onehot + s.astype(F32)
    ji = lax.broadcasted_iota(I32, (tm, tm), 0)
    ti = lax.broadcasted_iota(I32, (tm, tm), 1)
    before = (ji < ti).astype(BF16)
    prefix = jnp.dot(onehot.astype(BF16), before, preferred_element_type=F32)
    prefix = prefix + carry_ref[:, 0:1]
    ranks = [jnp.sum(jnp.where(s, prefix, 0.0), axis=0, keepdims=True) for s in sels]
    rank_ref[...] = jnp.concatenate(ranks, axis=0).astype(I32)
    carry_ref[...] = carry_ref[...] + jnp.sum(onehot, axis=1, keepdims=True)
    cnt_ref[...] = carry_ref[...].astype(I32)


def _router(t, w_r, b_r, tm):
    nt, d = t.shape
    ne = w_r.shape[1]
    return pl.pallas_call(
        _router_kernel,
        grid=(nt // tm,),
        in_specs=[pl.BlockSpec((tm, d), lambda i: (i, 0)),
                  pl.BlockSpec((ne, d), lambda i: (0, 0)),
                  pl.BlockSpec((ne, 1), lambda i: (0, 0))],
        out_specs=[pl.BlockSpec((TOP_K, tm), lambda i: (0, i)),
                   pl.BlockSpec((TOP_K, tm), lambda i: (0, i)),
                   pl.BlockSpec((TOP_K, tm), lambda i: (0, i)),
                   pl.BlockSpec((ne, 128), lambda i: (0, 0))],
        out_shape=[jax.ShapeDtypeStruct((TOP_K, nt), I32),
                   jax.ShapeDtypeStruct((TOP_K, nt), F32),
                   jax.ShapeDtypeStruct((TOP_K, nt), I32),
                   jax.ShapeDtypeStruct((ne, 128), I32)],
        scratch_shapes=[pltpu.VMEM((ne, 128), F32)],
        compiler_params=_cparams("arbitrary"),
        name="moe_router",
    )(t, w_r.T, b_r[:, None])


def _dispatch_kernel(pend_ref, padded_ref, nused_ref, dest_hbm, t_ref, xbuf_hbm,
                     idx_ref, rows_ref, zero_ref, isem, sem, zsem, *, tm, nseg, n_blk):
    i = pl.program_id(0)
    n = TOP_K * tm
    blk = MOE_ROWS * nseg

    @pl.when(i == 0)
    def _():
        zero_ref[...] = jnp.zeros(zero_ref.shape, zero_ref.dtype)

        def group_fill(e):
            return pltpu.make_async_copy(
                zero_ref, xbuf_hbm.at[pl.ds(pl.multiple_of((pend_ref[e] - MOE_ROWS) * nseg, blk), blk), :], zsem)

        def tail_fill(j):
            return pltpu.make_async_copy(zero_ref, xbuf_hbm.at[pl.ds(pl.multiple_of(j * blk, blk), blk), :], zsem)

        def start_group(e, carry):
            @pl.when(padded_ref[e] > 0)
            def _():
                group_fill(e).start()
            return carry

        def wait_group(e, carry):
            @pl.when(padded_ref[e] > 0)
            def _():
                group_fill(e).wait()
            return carry

        def start_tail(j, carry):
            tail_fill(j).start()
            return carry

        def wait_tail(j, carry):
            tail_fill(j).wait()
            return carry

        n_exp = pend_ref.shape[0]
        lax.fori_loop(0, n_exp, start_group, 0)
        lax.fori_loop(nused_ref[0], n_blk, start_tail, 0)
        lax.fori_loop(0, n_exp, wait_group, 0)
        lax.fori_loop(nused_ref[0], n_blk, wait_tail, 0)

    cp = pltpu.make_async_copy(dest_hbm.at[pl.ds(pl.multiple_of(i * n, n), n)], idx_ref, isem)
    cp.start()
    x = t_ref[...]
    for c in range(nseg):
        rows_ref[pl.ds(c, tm, stride=nseg), :] = x[:, c * LANES:(c + 1) * LANES]
    cp.wait()

    def row_copy(k, r):
        src = rows_ref.at[pl.ds(pl.multiple_of(r * nseg, nseg), nseg), :]
        dst = xbuf_hbm.at[pl.ds(pl.multiple_of(idx_ref[k * tm + r] * nseg, nseg), nseg), :]
        return pltpu.make_async_copy(src, dst, sem)

    def issue(r, carry):
        for k in range(TOP_K):
            row_copy(k, r).start(priority=k % DMA_QUEUES)
        return carry

    def drain(r, carry):
        for k in range(TOP_K):
            row_copy(k, r).wait()
        return carry

    lax.fori_loop(0, tm, issue, 0, unroll=DMA_UNROLL)
    lax.fori_loop(0, tm, drain, 0, unroll=DMA_UNROLL)


def _dispatch(t, dest_tiles, pend, padded, n_used, n_rows, tm):
    nt, d = t.shape
    nseg = d // LANES
    return pl.pallas_call(
        functools.partial(_dispatch_kernel, tm=tm, nseg=nseg, n_blk=n_rows // MOE_ROWS),
        grid_spec=pltpu.PrefetchScalarGridSpec(
            num_scalar_prefetch=3,
            grid=(nt // tm,),
            in_specs=[pl.BlockSpec(memory_space=pl.ANY),
                      pl.BlockSpec((tm, d), lambda i, *_: (i, 0))],
            out_specs=pl.BlockSpec(memory_space=pl.ANY),
            scratch_shapes=[pltpu.SMEM((TOP_K * tm,), I32), pltpu.VMEM((tm * nseg, LANES), t.dtype),
                            pltpu.VMEM((MOE_ROWS * nseg, LANES), t.dtype),
                            pltpu.SemaphoreType.DMA(()), pltpu.SemaphoreType.DMA(()),
                            pltpu.SemaphoreType.DMA(())],
        ),
        out_shape=jax.ShapeDtypeStruct((n_rows * nseg, LANES), t.dtype),
        compiler_params=pltpu.CompilerParams(dimension_semantics=("arbitrary",), has_side_effects=True,
                                             vmem_limit_bytes=VMEM_LIMIT),
        name="moe_dispatch",
    )(pend, padded, n_used, dest_tiles, t)


def _ffn_kernel(blk_e_ref, nused_ref, x_ref, w1_ref, b1g_ref, b1l_ref, w2_ref, b2_ref, y_ref,
                w1g_ref, w1l_ref, w2c_ref, *, nseg):
    i = pl.program_id(0)
    rows = x_ref.shape[0] // nseg
    e = blk_e_ref[i]
    e_prev = blk_e_ref[jnp.maximum(i - 1, 0)]

    @pl.when(jnp.logical_or(i == 0, e != e_prev))
    def _():
        w = 2 * LANES
        r = lax.broadcasted_iota(I32, (w, w), 0)
        c = lax.broadcasted_iota(I32, (w, w), 1)
        src_col = jnp.where(c < LANES, 2 * c, 2 * (c - LANES) + 1)
        perm = jnp.where(r == src_col, 1.0, 0.0).astype(BF16)
        for cc in range(w1_ref.shape[3] // w):
            chunk = w1_ref[0, 0, :, cc * w:(cc + 1) * w].astype(BF16)
            res = jnp.dot(chunk, perm, preferred_element_type=F32).astype(BF16)
            w1g_ref[:, cc * LANES:(cc + 1) * LANES] = res[:, :LANES]
            w1l_ref[:, cc * LANES:(cc + 1) * LANES] = res[:, LANES:]
        w2c_ref[...] = w2_ref[0, 0].astype(BF16)

    @pl.when(i < nused_ref[0])
    def _():
        x = jnp.concatenate([x_ref[pl.ds(c, rows, stride=nseg), :] for c in range(nseg)], axis=1).astype(BF16)
        hg = jnp.dot(x, w1g_ref[...], preferred_element_type=F32) + b1g_ref[0]
        hl = jnp.dot(x, w1l_ref[...], preferred_element_type=F32) + b1l_ref[0]
        xg = jnp.minimum(hg, SWIGLU_LIMIT)
        xl = jnp.clip(hl, -SWIGLU_LIMIT, SWIGLU_LIMIT)
        act = xg * _sigmoid(SWIGLU_ALPHA * xg) * (xl + 1.0)
        y = jnp.dot(act.astype(BF16), w2c_ref[...], preferred_element_type=F32) + b2_ref[0]
        for c in range(nseg):
            y_ref[pl.ds(c, rows, stride=nseg), :] = y[:, c * LANES:(c + 1) * LANES]

    @pl.when(i >= nused_ref[0])
    def _():
        y_ref[...] = jnp.zeros(y_ref.shape, F32)


def _ffn(x_buf, blk_e, n_used, layer, w1, b1g, b1l, w2, b2):
    _, ne, d, f2 = w1.shape
    f = f2 // 2
    nseg = d // LANES
    n_rows = x_buf.shape[0] // nseg
    n_blk = n_rows // MOE_ROWS
    assert f2 % (2 * LANES) == 0
    wspec = lambda a, b: pl.BlockSpec((1, a, b), lambda i, be, nu: (be[i], 0, 0))
    wfull = lambda a, b: pl.BlockSpec((1, 1, a, b), lambda i, be, nu: (layer, be[i], 0, 0))
    return pl.pallas_call(
        functools.partial(_ffn_kernel, nseg=nseg),
        grid_spec=pltpu.PrefetchScalarGridSpec(
            num_scalar_prefetch=2,
            grid=(n_blk,),
            in_specs=[pl.BlockSpec((MOE_ROWS * nseg, LANES), lambda i, be, nu: (i, 0)),
                      wfull(d, f2), wspec(1, f), wspec(1, f), wfull(f, d), wspec(1, d)],
            out_specs=pl.BlockSpec((MOE_ROWS * nseg, LANES), lambda i, be, nu: (i, 0)),
            scratch_shapes=[pltpu.VMEM((d, f), BF16), pltpu.VMEM((d, f), BF16), pltpu.VMEM((f, d), BF16)],
        ),
        out_shape=jax.ShapeDtypeStruct((n_rows * nseg, LANES), F32),
        compiler_params=_cparams("arbitrary"),
        name="moe_ffn",
    )(blk_e, n_used, x_buf, w1, b1g, b1l, w2, b2)


def _combine_kernel(dest_hbm, ybuf_hbm, gate_ref, h_ref, g_ref, b_ref, o_ref, idx_ref, rows_ref, isem, sem,
                    *, tm, nseg, alpha):
    i = pl.program_id(0)
    n = TOP_K * tm

    def row_copy(slot, k, r):
        src = ybuf_hbm.at[pl.ds(pl.multiple_of(idx_ref[slot * n + k * tm + r] * nseg, nseg), nseg), :]
        dst = rows_ref.at[slot * TOP_K + k, pl.ds(pl.multiple_of(r * nseg, nseg), nseg), :]
        return pltpu.make_async_copy(src, dst, sem.at[slot])

    def gather_tile(tile, slot):
        cp = pltpu.make_async_copy(dest_hbm.at[pl.ds(pl.multiple_of(tile * n, n), n)],
                                   idx_ref.at[pl.ds(slot * n, n)], isem)
        cp.start()
        cp.wait()

        def issue(r, carry):
            for k in range(TOP_K):
                row_copy(slot, k, r).start(priority=k % DMA_QUEUES)
            return carry

        lax.fori_loop(0, tm, issue, 0, unroll=DMA_UNROLL)

    @pl.when(i == 0)
    def _():
        gather_tile(0, 0)

    for nxt in range(2):
        @pl.when(jnp.logical_and(i + 1 < pl.num_programs(0), (i + 1) % 2 == nxt))
        def _():
            gather_tile(i + 1, nxt)

    slot = i % 2

    def drain(r, carry):
        for k in range(TOP_K):
            row_copy(slot, k, r).wait()
        return carry

    lax.fori_loop(0, tm, drain, 0, unroll=DMA_UNROLL)
    gates = gate_ref[...]
    y = alpha * h_ref[...]
    for k in range(TOP_K):
        rows = jnp.concatenate([rows_ref[slot * TOP_K + k, pl.ds(c, tm, stride=nseg), :] for c in range(nseg)],
                               axis=1)
        y = y + gates[:, k:k + 1] * rows
    o_ref[...] = _layer_norm(y, g_ref[...], b_ref[...])


def _combine(y_buf, dest_tiles, gates_t, h, g, b, alpha, tm):
    nt, d = h.shape
    nseg = d // LANES
    return pl.pallas_call(
        functools.partial(_combine_kernel, tm=tm, nseg=nseg, alpha=alpha),
        grid=(nt // tm,),
        in_specs=[pl.BlockSpec(memory_space=pl.ANY),
                  pl.BlockSpec(memory_space=pl.ANY),
                  pl.BlockSpec((tm, TOP_K), lambda i: (i, 0)),
                  pl.BlockSpec((tm, d), lambda i: (i, 0)),
                  pl.BlockSpec((1, d), lambda i: (0, 0)),
                  pl.BlockSpec((1, d), lambda i: (0, 0))],
        out_specs=pl.BlockSpec((tm, d), lambda i: (i, 0)),
        out_shape=jax.ShapeDtypeStruct((nt, d), F32),
        scratch_shapes=[pltpu.SMEM((2 * TOP_K * tm,), I32), pltpu.VMEM((2 * TOP_K, tm * nseg, LANES), F32),
                        pltpu.SemaphoreType.DMA(()), pltpu.SemaphoreType.DMA((2,))],
        compiler_params=_cparams("arbitrary"),
        name="moe_combine",
    )(dest_tiles, y_buf, gates_t, h, g, b)


def _moe_layer(t, layer, w_r, b_r, w1, b1, w2, b2, ln_g, ln_b, alpha):
    nt, d = t.shape
    ne = w_r.shape[1]
    tm_r = _pick(nt, (512, 256, 128))
    tm_d = _pick(nt, (512, 256, 128))
    top_e, gates, rank, cnt = _router(t, w_r, b_r, tm_r)

    counts = cnt[:, 0]
    padded = ((counts + MOE_ROWS - 1) // MOE_ROWS) * MOE_ROWS
    pend = jnp.cumsum(padded)
    poff = pend - padded
    expert_ids = jnp.arange(ne, dtype=I32)[:, None, None]
    dest = jnp.sum(jnp.where(top_e[None] == expert_ids, poff[:, None, None], 0), axis=0) + rank
    n_rows = nt * TOP_K + ne * MOE_ROWS
    n_blk = n_rows // MOE_ROWS
    blk_start = jnp.arange(n_blk, dtype=I32) * MOE_ROWS
    blk_e = jnp.minimum(jnp.sum(blk_start[:, None] >= pend[None, :], axis=1), ne - 1).astype(I32)
    n_used = (pend[-1:] // MOE_ROWS).astype(I32)
    dest_tiles = dest.astype(I32).reshape(TOP_K, nt // tm_d, tm_d).transpose(1, 0, 2).reshape(-1)

    x_buf = _dispatch(t, dest_tiles, pend.astype(I32), padded.astype(I32), n_used, n_rows, tm_d)
    y_buf = _ffn(x_buf, blk_e, n_used, layer, w1, b1[:, None, 0::2], b1[:, None, 1::2], w2, b2[:, None, :])
    return _combine(y_buf, dest_tiles, gates.T, t, ln_g[None, :], ln_b[None, :], alpha, tm_d)


def _conformer_layer(h, w1, b1, dw, dwb, ln_g, ln_b, w2, b2, mix_g, mix_b, alpha, pad):
    bsz, lp, d = h.shape
    c = dw.shape[1]
    tl = _pick(lp, (544, 272, 384, 192, 128))
    u = _glu(h, w1[:, :c].astype(BF16), w1[:, c:].astype(BF16), b1[None, :c], b1[None, c:], pad, tl)
    return _conv_mix(u, h, dw, dwb[None, :], ln_g[None, :], ln_b[None, :], w2.astype(BF16), b2[None, :],
                     mix_g[None, :], mix_b[None, :], alpha, tl)


def _mamba_layer(h, w_in, conv_w, conv_b, dt_bias, a_log, d_skip, norm_g, w_out, mix_g, mix_b, alpha, pad):
    bsz, lp, d = h.shape
    nt = bsz * lp
    d_inner = w_out.shape[0]
    nh = dt_bias.shape[0]
    conv_dim = conv_w.shape[1]
    nproj = d_inner + conv_dim
    hpg = nh // SSM_GROUPS
    tm = _pick(nt, (1024, 512, 256, 128))
    proj = _mm(h.reshape(nt, d), w_in[:, :nproj].astype(BF16), tm, _pick(nproj, (2048, 1024, 512, 256, 128)))
    a_neg = -jnp.exp(a_log.astype(F32))
    dtc, csc, csr = _dt_prep(h, w_in[:, nproj:], dt_bias, a_neg, pad)
    nc = lp // SSM_CHUNK
    csr = csr.reshape(bsz, nc, SSM_GROUPS, hpg, SSM_CHUNK).transpose(0, 2, 1, 3, 4)
    d_exp = jnp.repeat(d_skip.astype(F32), SSM_HEAD_DIM)[None, :]
    y = _ssd(proj.reshape(bsz, lp, nproj), dtc, csc, csr, conv_w, conv_b[None, :], d_exp, norm_g[None, :],
             pad, d_inner)
    out = _mm_resid_ln(y.reshape(nt, d_inner), w_out.astype(BF16), h.reshape(nt, d),
                       mix_g[None, :], mix_b[None, :], alpha, _pick(nt, (512, 256, 128)))
    return out.reshape(bsz, lp, d)


def kernel(x, meta_tokens, conv_w1, conv_b1, conv_dw, conv_dwb, conv_ln_g, conv_ln_b, conv_w2, conv_b2,
           ssm_w_in, ssm_conv_w, ssm_conv_b, ssm_dt_bias, ssm_a_log, ssm_d, ssm_norm_g, ssm_w_out,
           moe_w_router, moe_b_router, moe_w1, moe_b1, moe_w2, moe_b2,
           ln_mix_g, ln_mix_b, ln_ffn_g, ln_ffn_b):
    bsz, seq, d = x.shape
    depth = ln_mix_g.shape[0]
    alpha = (2 * depth) ** 0.25
    pad = SSM_CHUNK - N_META
    lp = pad + N_META + seq
    assert lp % SSM_CHUNK == 0
    meta = jnp.broadcast_to(meta_tokens.astype(x.dtype)[None], (bsz, N_META, d))
    h = jnp.concatenate([jnp.zeros((bsz, pad, d), x.dtype), meta, x], axis=1)
    for i in range(depth):
        j = i // 2
        if i % 2 == 0:
            h = _conformer_layer(h, conv_w1[j], conv_b1[j], conv_dw[j], conv_dwb[j], conv_ln_g[j],
                                 conv_ln_b[j], conv_w2[j], conv_b2[j], ln_mix_g[i], ln_mix_b[i], alpha, pad)
        else:
            h = _mamba_layer(h, ssm_w_in[j], ssm_conv_w[j], ssm_conv_b[j], ssm_dt_bias[j], ssm_a_log[j],
                             ssm_d[j], ssm_norm_g[j], ssm_w_out[j], ln_mix_g[i], ln_mix_b[i], alpha, pad)
        t = _moe_layer(h.reshape(bsz * lp, d), i, moe_w_router[i], moe_b_router[i], moe_w1, moe_b1[i],
                       moe_w2, moe_b2[i], ln_ffn_g[i], ln_ffn_b[i], alpha)
        h = t.reshape(bsz, lp, d)
    return h[:, pad + N_META:]
```

```python
import functools

import jax
import jax.numpy as jnp
from jax import lax
from jax.experimental import pallas as pl
from jax.experimental.pallas import tpu as pltpu

F32 = jnp.float32
BF16 = jnp.bfloat16
I32 = jnp.int32
HIGHEST = lax.Precision.HIGHEST

N_META = 16
SSM_HEAD_DIM = 64
SSM_GROUPS = 8
SSM_STATE = 128
SSM_CHUNK = 128
TOP_K = 4
SWIGLU_ALPHA = 1.702
SWIGLU_LIMIT = 7.0
LN_EPS = 1e-5
RMS_EPS = 1e-5

CONV_HALO = 32
CONV_ROWS = 32
CONV_LANES = 256
MOE_ROWS = 512
DMA_UNROLL = 8
DMA_QUEUES = 2
LANES = 128
VMEM_LIMIT = 56 * 1024 * 1024


def _cparams(*sem):
    return pltpu.CompilerParams(dimension_semantics=tuple(sem), vmem_limit_bytes=VMEM_LIMIT)


def _pick(n, prefs):
    for p in prefs:
        if n % p == 0:
            return p
    raise ValueError(f"no tile for {n} in {prefs}")


def _layer_norm(y, g, b):
    mu = jnp.mean(y, axis=-1, keepdims=True)
    yc = y - mu
    var = jnp.mean(yc * yc, axis=-1, keepdims=True)
    return yc * lax.rsqrt(var + LN_EPS) * g + b


def _sigmoid(x):
    return 1.0 / (1.0 + jnp.exp(-x))


def _softplus(x):
    return jnp.maximum(x, 0.0) + jnp.log(1.0 + jnp.exp(-jnp.abs(x)))


def _glu_kernel(x_ref, wa_ref, wg_ref, ba_ref, bg_ref, o_ref, *, pad):
    j = pl.program_id(1)
    x = x_ref[0].astype(BF16)
    a = jnp.dot(x, wa_ref[...], preferred_element_type=F32) + ba_ref[...]
    g = jnp.dot(x, wg_ref[...], preferred_element_type=F32) + bg_ref[...]
    u = a * _sigmoid(g)
    row = lax.broadcasted_iota(I32, u.shape, 0)
    valid = jnp.logical_or(j > 0, row >= pad)
    o_ref[0] = jnp.where(valid, u, 0.0)


def _glu(h, wa, wg, ba, bg, pad, tl):
    bsz, lp, d = h.shape
    c = wa.shape[1]
    return pl.pallas_call(
        functools.partial(_glu_kernel, pad=pad),
        grid=(bsz, lp // tl),
        in_specs=[
            pl.BlockSpec((1, tl, d), lambda b, j: (b, j, 0)),
            pl.BlockSpec((d, c), lambda b, j: (0, 0)),
            pl.BlockSpec((d, c), lambda b, j: (0, 0)),
            pl.BlockSpec((1, c), lambda b, j: (0, 0)),
            pl.BlockSpec((1, c), lambda b, j: (0, 0)),
        ],
        out_specs=pl.BlockSpec((1, tl, c), lambda b, j: (b, j, 0)),
        out_shape=jax.ShapeDtypeStruct((bsz, lp, c), F32),
        compiler_params=_cparams("parallel", "parallel"),
        name="conf_glu",
    )(h, wa, wg, ba, bg)


def _conv_mix_kernel(ucur_ref, uprev_ref, h_ref, dw_ref, dwb_ref, g1_ref, b1_ref, w2_ref, b2_ref,
                     g2_ref, bb2_ref, o_ref, win_ref, act_ref, *, taps, alpha):
    j = pl.program_id(1)
    tl = ucur_ref.shape[1]

    @pl.when(j == 0)
    def _():
        win_ref[0:CONV_HALO, :] = jnp.zeros((CONV_HALO, win_ref.shape[1]), F32)

    @pl.when(j > 0)
    def _():
        win_ref[0:CONV_HALO, :] = uprev_ref[0, tl - CONV_HALO:tl, :]

    win_ref[CONV_HALO:CONV_HALO + tl, :] = ucur_ref[0]

    def chunk(c, carry):
        base = pl.multiple_of(c * CONV_ROWS, CONV_ROWS)
        n = CONV_HALO + CONV_ROWS
        parts = []
        for lc in range(win_ref.shape[1] // CONV_LANES):
            sl = slice(lc * CONV_LANES, (lc + 1) * CONV_LANES)
            win = win_ref[pl.ds(base, n), sl]
            acc = jnp.zeros((CONV_ROWS, CONV_LANES), F32) + dwb_ref[:, sl]
            for s in range(8):
                ws = win if s == 0 else pltpu.roll(win, n - s, 0)
                for k in range(taps):
                    off = CONV_HALO + k - (taps - 1)
                    if off % 8 == s:
                        acc = acc + ws[off - s:off - s + CONV_ROWS, :] * dw_ref[k:k + 1, sl]
            parts.append(acc)
        y = _layer_norm(jnp.concatenate(parts, axis=1), g1_ref[...], b1_ref[...])
        y = y * _sigmoid(y)
        act_ref[pl.ds(base, CONV_ROWS), :] = y.astype(BF16)
        return carry

    lax.fori_loop(0, tl // CONV_ROWS, chunk, 0)
    mix = jnp.dot(act_ref[...], w2_ref[...], preferred_element_type=F32) + b2_ref[...]
    y = alpha * h_ref[0] + mix
    o_ref[0] = _layer_norm(y, g2_ref[...], bb2_ref[...])


def _conv_mix(u, h, dw, dwb, g1, b1, w2, b2, g2, bb2, alpha, tl):
    bsz, lp, c = u.shape
    d = h.shape[2]
    taps = dw.shape[0]
    assert taps - 1 <= CONV_HALO and tl % CONV_ROWS == 0
    vec = lambda n: pl.BlockSpec((1, n), lambda b, j: (0, 0))
    return pl.pallas_call(
        functools.partial(_conv_mix_kernel, taps=taps, alpha=alpha),
        grid=(bsz, lp // tl),
        in_specs=[
            pl.BlockSpec((1, tl, c), lambda b, j: (b, j, 0)),
            pl.BlockSpec((1, tl, c), lambda b, j: (b, jnp.maximum(j - 1, 0), 0)),
            pl.BlockSpec((1, tl, d), lambda b, j: (b, j, 0)),
            pl.BlockSpec((taps, c), lambda b, j: (0, 0)),
            vec(c), vec(c), vec(c),
            pl.BlockSpec((c, d), lambda b, j: (0, 0)),
            vec(d), vec(d), vec(d),
        ],
        out_specs=pl.BlockSpec((1, tl, d), lambda b, j: (b, j, 0)),
        out_shape=jax.ShapeDtypeStruct((bsz, lp, d), F32),
        scratch_shapes=[pltpu.VMEM((CONV_HALO + tl, c), F32), pltpu.VMEM((tl, c), BF16)],
        compiler_params=_cparams("parallel", "parallel"),
        name="conf_conv_mix",
    )(u, u, h, dw, dwb, g1, b1, w2, b2, g2, bb2)


def _mm_kernel(x_ref, w_ref, o_ref):
    o_ref[...] = jnp.dot(x_ref[...].astype(BF16), w_ref[...], preferred_element_type=F32).astype(o_ref.dtype)


def _mm(x, w, tm, tn, out_dtype=F32):
    m, k = x.shape
    n = w.shape[1]
    return pl.pallas_call(
        _mm_kernel,
        grid=(n // tn, m // tm),
        in_specs=[pl.BlockSpec((tm, k), lambda jn, im: (im, 0)),
                  pl.BlockSpec((k, tn), lambda jn, im: (0, jn))],
        out_specs=pl.BlockSpec((tm, tn), lambda jn, im: (im, jn)),
        out_shape=jax.ShapeDtypeStruct((m, n), out_dtype),
        compiler_params=_cparams("parallel", "parallel"),
        name="mm",
    )(x, w)


def _mm_resid_ln_kernel(x_ref, w_ref, h_ref, g_ref, b_ref, o_ref, *, alpha):
    mix = jnp.dot(x_ref[...].astype(BF16), w_ref[...], preferred_element_type=F32)
    o_ref[...] = _layer_norm(alpha * h_ref[...] + mix, g_ref[...], b_ref[...])


def _mm_resid_ln(x, w, h, g, b, alpha, tm):
    m, k = x.shape
    d = w.shape[1]
    return pl.pallas_call(
        functools.partial(_mm_resid_ln_kernel, alpha=alpha),
        grid=(m // tm,),
        in_specs=[pl.BlockSpec((tm, k), lambda i: (i, 0)),
                  pl.BlockSpec((k, d), lambda i: (0, 0)),
                  pl.BlockSpec((tm, d), lambda i: (i, 0)),
                  pl.BlockSpec((1, d), lambda i: (0, 0)),
                  pl.BlockSpec((1, d), lambda i: (0, 0))],
        out_specs=pl.BlockSpec((tm, d), lambda i: (i, 0)),
        out_shape=jax.ShapeDtypeStruct((m, d), F32),
        compiler_params=_cparams("parallel"),
        name="mm_resid_ln",
    )(x, w, h, g, b)


def _dt_kernel(x_ref, wc_ref, bc_ref, ac_ref, dtc_ref, csc_ref, csr_ref, *, pad):
    c = pl.program_id(1)
    x = x_ref[0]
    n = x.shape[0]
    nh = csr_ref.shape[2]
    raw_c = jnp.dot(x, wc_ref[...], preferred_element_type=F32, precision=HIGHEST)
    row_c = lax.broadcasted_iota(I32, raw_c.shape, 0)
    dt_c = jnp.where(jnp.logical_and(c == 0, row_c < pad), 0.0, _softplus(raw_c + bc_ref[...]))
    li = lax.broadcasted_iota(I32, (n, n), 0)
    si = lax.broadcasted_iota(I32, (n, n), 1)
    tri = jnp.where(si <= li, 1.0, 0.0)
    cs_c = jnp.dot(tri, dt_c * ac_ref[...], preferred_element_type=F32, precision=HIGHEST)
    dtc_ref[0] = dt_c
    csc_ref[0] = cs_c
    csr_ref[0, 0] = cs_c.T[:nh, :]


def _dt_prep(h, w_dt, dt_bias, a_neg, pad):
    bsz, lp, d = h.shape
    nh = w_dt.shape[1]
    nc = lp // SSM_CHUNK
    lanes = 128
    wc = jnp.zeros((d, lanes), F32).at[:, :nh].set(w_dt)
    bc = jnp.zeros((1, lanes), F32).at[0, :nh].set(dt_bias)
    ac = jnp.zeros((1, lanes), F32).at[0, :nh].set(a_neg)
    return pl.pallas_call(
        functools.partial(_dt_kernel, pad=pad),
        grid=(bsz, nc),
        in_specs=[
            pl.BlockSpec((1, SSM_CHUNK, d), lambda b, c: (b, c, 0)),
            pl.BlockSpec((d, lanes), lambda b, c: (0, 0)),
            pl.BlockSpec((1, lanes), lambda b, c: (0, 0)),
            pl.BlockSpec((1, lanes), lambda b, c: (0, 0)),
        ],
        out_specs=[
            pl.BlockSpec((1, SSM_CHUNK, lanes), lambda b, c: (b, c, 0)),
            pl.BlockSpec((1, SSM_CHUNK, lanes), lambda b, c: (b, c, 0)),
            pl.BlockSpec((1, 1, nh, SSM_CHUNK), lambda b, c: (b, c, 0, 0)),
        ],
        out_shape=[
            jax.ShapeDtypeStruct((bsz, lp, lanes), F32),
            jax.ShapeDtypeStruct((bsz, lp, lanes), F32),
            jax.ShapeDtypeStruct((bsz, nc, nh, SSM_CHUNK), F32),
        ],
        compiler_params=_cparams("parallel", "parallel"),
        name="ssm_dt",
    )(h, wc, bc, ac)


def _ssd_kernel(xs_ref, bm_ref, cm_ref, z_ref, dtc_ref, csc_ref, csr_ref,
                cwx_ref, cwb_ref, cwc_ref, cbx_ref, cbb_ref, cbc_ref, dsk_ref, ng_ref,
                o_ref, state_ref, carx_ref, carb_ref, carc_ref, *, pad, hpg, taps):
    g = pl.program_id(1)
    q = SSM_CHUNK
    hd = SSM_HEAD_DIM
    gw = hpg * hd
    n_chunks = xs_ref.shape[1] // q

    state_ref[...] = jnp.zeros(state_ref.shape, F32)
    carx_ref[...] = jnp.zeros(carx_ref.shape, F32)
    carb_ref[...] = jnp.zeros(carb_ref.shape, F32)
    carc_ref[...] = jnp.zeros(carc_ref.shape, F32)

    hrow = lax.broadcasted_iota(I32, (128, gw), 0)
    hcol = lax.broadcasted_iota(I32, (128, gw), 1) // hd
    sel = jnp.where(hrow == g * hpg + hcol, 1.0, 0.0).astype(BF16)

    def select_heads(v):
        hi = v.astype(BF16)
        r1 = v - hi.astype(F32)
        mid = r1.astype(BF16)
        lo = (r1 - mid.astype(F32)).astype(BF16)
        out = jnp.dot(hi, sel, preferred_element_type=F32)
        out = out + jnp.dot(mid, sel, preferred_element_type=F32)
        return out + jnp.dot(lo, sel, preferred_element_type=F32)

    lane_head = lax.broadcasted_iota(I32, (q, gw), 1) // hd
    li = lax.broadcasted_iota(I32, (q, q), 0)
    si = lax.broadcasted_iota(I32, (q, q), 1)
    causal = si <= li
    row1 = lax.broadcasted_iota(I32, (q, 1), 0)

    def conv_act(raw, car_ref, w_ref, b_ref, valid):
        raw = jnp.where(valid, raw, 0.0)
        win = jnp.concatenate([car_ref[...], raw], axis=0)
        acc = jnp.zeros(raw.shape, F32) + b_ref[...]
        for k in range(taps):
            off = 8 + k - (taps - 1)
            acc = acc + win[off:off + q, :] * w_ref[k:k + 1, :]
        car_ref[...] = raw[q - 8:q, :]
        act = acc * _sigmoid(acc)
        return jnp.where(valid, act, 0.0)

    def chunk(c, carry):
        r0 = pl.multiple_of(c * q, q)
        valid = jnp.logical_or(c > 0, row1 >= pad)
        xs = conv_act(xs_ref[0, pl.ds(r0, q), :], carx_ref, cwx_ref, cbx_ref, valid)
        bm = conv_act(bm_ref[0, pl.ds(r0, q), :], carb_ref, cwb_ref, cbb_ref, valid)
        cm = conv_act(cm_ref[0, pl.ds(r0, q), :], carc_ref, cwc_ref, cbc_ref, valid)

        dt_e = select_heads(dtc_ref[0, pl.ds(r0, q), :])
        cs_e = select_heads(csc_ref[0, pl.ds(r0, q), :])
        cs_r = csr_ref[0, 0, c]
        a_last = cs_e[q - 1:q, :]

        x_dt = xs * dt_e
        xw = (x_dt * jnp.exp(a_last - cs_e)).astype(BF16)
        bm16 = bm.astype(BF16)
        cm16 = cm.astype(BF16)
        cb = lax.dot_general(cm16, bm16, (((1,), (1,)), ((), ())), preferred_element_type=F32)

        m_parts = []
        x_parts = []
        for r in range(hpg):
            col = cs_e[:, r * hd:r * hd + 1]
            seg = col - cs_r[r:r + 1, :]
            decay = jnp.exp(jnp.where(causal, seg, -jnp.inf))
            m_parts.append((cb * decay).astype(BF16))
            x_parts.append(jnp.where(lane_head == r, x_dt, 0.0).astype(BF16))
        m_cat = jnp.concatenate(m_parts, axis=1)
        x_bd = jnp.concatenate(x_parts, axis=0)
        y = jnp.dot(m_cat, x_bd, preferred_element_type=F32)

        st = state_ref[...]
        y = y + jnp.dot(cm16, st.astype(BF16), preferred_element_type=F32) * jnp.exp(cs_e)
        new = jnp.dot(bm.T.astype(BF16), xw, preferred_element_type=F32)
        state_ref[...] = st * jnp.exp(a_last) + new
        y = y + dsk_ref[...] * xs

        z = z_ref[0, pl.ds(r0, q), :]
        y = y * (z * _sigmoid(z))
        y = y * lax.rsqrt(jnp.mean(y * y, axis=-1, keepdims=True) + RMS_EPS) * ng_ref[...]
        o_ref[0, pl.ds(r0, q), :] = y.astype(o_ref.dtype)
        return carry

    lax.fori_loop(0, n_chunks, chunk, 0)


def _ssd(proj, dtc, csc, csr, conv_w, conv_b, d_exp, norm_g, pad, d_inner):
    bsz, lp, _ = proj.shape
    g = SSM_GROUPS
    n = SSM_STATE
    gw = d_inner // g
    hpg = gw // SSM_HEAD_DIM
    assert gw % 128 == 0 and n == 128 and dtc.shape[2] == 128
    taps = conv_w.shape[0]
    nc = lp // SSM_CHUNK
    zb = 0
    xb = d_inner // gw
    bb = 2 * d_inner // n
    cb_ = (2 * d_inner + g * n) // n
    cxb = 0
    cbb = d_inner // n
    ccb = (d_inner + g * n) // n
    seq = lambda w, off: pl.BlockSpec((1, lp, w), lambda b, j, off=off: (b, 0, off + j))
    cw = lambda w, off: pl.BlockSpec((taps, w), lambda b, j, off=off: (0, off + j))
    cbv = lambda w, off: pl.BlockSpec((1, w), lambda b, j, off=off: (0, off + j))
    return pl.pallas_call(
        functools.partial(_ssd_kernel, pad=pad, hpg=hpg, taps=taps),
        grid=(bsz, g),
        in_specs=[
            seq(gw, xb), seq(n, bb), seq(n, cb_), seq(gw, zb),
            pl.BlockSpec((1, lp, 128), lambda b, j: (b, 0, 0)),
            pl.BlockSpec((1, lp, 128), lambda b, j: (b, 0, 0)),
            pl.BlockSpec((1, 1, nc, hpg, SSM_CHUNK), lambda b, j: (b, j, 0, 0, 0)),
            cw(gw, cxb), cw(n, cbb), cw(n, ccb),
            cbv(gw, cxb), cbv(n, cbb), cbv(n, ccb),
            pl.BlockSpec((1, gw), lambda b, j: (0, j)),
            pl.BlockSpec((1, gw), lambda b, j: (0, j)),
        ],
        out_specs=pl.BlockSpec((1, lp, gw), lambda b, j: (b, 0, j)),
        out_shape=jax.ShapeDtypeStruct((bsz, lp, d_inner), BF16),
        scratch_shapes=[pltpu.VMEM((n, gw), F32), pltpu.VMEM((8, gw), F32),
                        pltpu.VMEM((8, n), F32), pltpu.VMEM((8, n), F32)],
        compiler_params=_cparams("parallel", "parallel"),
        name="ssm_ssd",
    )(proj, proj, proj, proj, dtc, csc, csr, conv_w, conv_w, conv_w, conv_b, conv_b, conv_b, d_exp, norm_g)


def _pad_chunk(tile, q, tm, chunks_per_row):
    return lax.rem(tile * (tm // SSM_CHUNK) + q, chunks_per_row) == 0


def _for_real_groups(tile, tm, chunks_per_row, pad, body):
    per_chunk = SSM_CHUNK // DMA_UNROLL
    for q in range(tm // SSM_CHUNK):
        skip = jnp.where(_pad_chunk(tile, q, tm, chunks_per_row), pad // DMA_UNROLL, 0)
        lax.fori_loop(q * per_chunk + skip, (q + 1) * per_chunk, body, 0)


def _router_kernel(x_ref, wr_ref, br_ref, e_ref, gate_ref, rank_ref, cnt_ref, carry_ref, *, chunks_per_row, pad):
    i = pl.program_id(0)

    @pl.when(i == 0)
    def _():
        carry_ref[...] = jnp.zeros(carry_ref.shape, F32)

    x = x_ref[...]
    tm = x.shape[0]
    ne = wr_ref.shape[0]
    logits = lax.dot_general(wr_ref[...], x, (((1,), (1,)), ((), ())),
                             preferred_element_type=F32, precision=HIGHEST) + br_ref[...]
    ids = lax.broadcasted_iota(I32, (ne, tm), 0)
    vals = logits
    top_v, top_e, sels = [], [], []
    for _ in range(TOP_K):
        m = jnp.max(vals, axis=0, keepdims=True)
        idx = jnp.min(jnp.where(vals == m, ids, ne), axis=0, keepdims=True)
        sel = ids == idx
        top_v.append(m)
        top_e.append(idx)
        sels.append(sel)
        vals = jnp.where(sel, -jnp.inf, vals)
    ex = [jnp.exp(v - top_v[0]) for v in top_v]
    den = ex[0]
    for t in ex[1:]:
        den = den + t
    e_ref[...] = jnp.concatenate(top_e, axis=0)
    gate_ref[...] = jnp.concatenate([t / den for t in ex], axis=0)

    onehot = sels[0].astype(F32)
    for s in sels[1:]:
        onehot = onehot + s.astype(F32)
    lane = lax.broadcasted_iota(I32, (1, tm), 1)
    real = jnp.ones((1, tm), F32)
    for q in range(tm // SSM_CHUNK):
        inert = jnp.logical_and(_pad_chunk(i, q, tm, chunks_per_row),
                                jnp.logical_and(lane >= q * SSM_CHUNK, lane < q * SSM_CHUNK + pad))
        real = jnp.where(inert, 0.0, real)
    onehot = onehot * real
    ji = lax.broadcasted_iota(I32, (tm, tm), 0)
    ti = lax.broadcasted_iota(I32, (tm, tm), 1)
    before = (ji < ti).astype(BF16)
    prefix = jnp.dot(onehot.astype(BF16), before, preferred_element_type=F32)
    prefix = prefix + carry_ref[:, 0:1]
    ranks = [jnp.sum(jnp.where(s, prefix, 0.0), axis=0, keepdims=True) for s in sels]
    rank_ref[...] = jnp.concatenate(ranks, axis=0).astype(I32)
    carry_ref[...] = carry_ref[...] + jnp.sum(onehot, axis=1, keepdims=True)
    cnt_ref[...] = carry_ref[...].astype(I32)


def _router(t, w_r, b_r, tm, chunks_per_row, pad):
    nt, d = t.shape
    ne = w_r.shape[1]
    assert tm % SSM_CHUNK == 0
    return pl.pallas_call(
        functools.partial(_router_kernel, chunks_per_row=chunks_per_row, pad=pad),
        grid=(nt // tm,),
        in_specs=[pl.BlockSpec((tm, d), lambda i: (i, 0)),
                  pl.BlockSpec((ne, d), lambda i: (0, 0)),
                  pl.BlockSpec((ne, 1), lambda i: (0, 0))],
        out_specs=[pl.BlockSpec((TOP_K, tm), lambda i: (0, i)),
                   pl.BlockSpec((TOP_K, tm), lambda i: (0, i)),
                   pl.BlockSpec((TOP_K, tm), lambda i: (0, i)),
                   pl.BlockSpec((ne, 128), lambda i: (0, 0))],
        out_shape=[jax.ShapeDtypeStruct((TOP_K, nt), I32),
                   jax.ShapeDtypeStruct((TOP_K, nt), F32),
                   jax.ShapeDtypeStruct((TOP_K, nt), I32),
                   jax.ShapeDtypeStruct((ne, 128), I32)],
        scratch_shapes=[pltpu.VMEM((ne, 128), F32)],
        compiler_params=_cparams("arbitrary"),
        name="moe_router",
    )(t, w_r.T, b_r[:, None])


def _dispatch_kernel(pend_ref, padded_ref, nused_ref, dest_hbm, t_ref, xbuf_hbm,
                     idx_ref, rows_ref, zero_ref, isem, sem, zsem, *, tm, nseg, n_blk, chunks_per_row, pad):
    i = pl.program_id(0)
    n = TOP_K * tm
    blk = MOE_ROWS * nseg

    @pl.when(i == 0)
    def _():
        zero_ref[...] = jnp.zeros(zero_ref.shape, zero_ref.dtype)

        def group_fill(e):
            return pltpu.make_async_copy(
                zero_ref, xbuf_hbm.at[pl.ds(pl.multiple_of((pend_ref[e] - MOE_ROWS) * nseg, blk), blk), :], zsem)

        def tail_fill(j):
            return pltpu.make_async_copy(zero_ref, xbuf_hbm.at[pl.ds(pl.multiple_of(j * blk, blk), blk), :], zsem)

        def start_group(e, carry):
            @pl.when(padded_ref[e] > 0)
            def _():
                group_fill(e).start()
            return carry

        def wait_group(e, carry):
            @pl.when(padded_ref[e] > 0)
            def _():
                group_fill(e).wait()
            return carry

        def start_tail(j, carry):
            tail_fill(j).start()
            return carry

        def wait_tail(j, carry):
            tail_fill(j).wait()
            return carry

        n_exp = pend_ref.shape[0]
        lax.fori_loop(0, n_exp, start_group, 0)
        lax.fori_loop(nused_ref[0], n_blk, start_tail, 0)
        lax.fori_loop(0, n_exp, wait_group, 0)
        lax.fori_loop(nused_ref[0], n_blk, wait_tail, 0)

    last = pl.num_programs(0) - 1

    def row_copy(slot, k, r):
        src = rows_ref.at[slot, pl.ds(pl.multiple_of(r * nseg, nseg), nseg), :]
        dst = xbuf_hbm.at[pl.ds(pl.multiple_of(idx_ref[slot * n + k * tm + r] * nseg, nseg), nseg), :]
        return pltpu.make_async_copy(src, dst, sem.at[slot])

    def drain_slot(slot, tile):
        def drain(grp, carry):
            for u in range(DMA_UNROLL):
                for k in range(TOP_K):
                    row_copy(slot, k, grp * DMA_UNROLL + u).wait()
            return carry

        _for_real_groups(tile, tm, chunks_per_row, pad, drain)

    for slot in range(2):
        @pl.when(i % 2 == slot)
        def _():
            @pl.when(i >= 2)
            def _():
                drain_slot(slot, i - 2)

            cp = pltpu.make_async_copy(dest_hbm.at[pl.ds(pl.multiple_of(i * n, n), n)],
                                       idx_ref.at[pl.ds(slot * n, n)], isem)
            cp.start()
            x = t_ref[...]
            for c in range(nseg):
                rows_ref[slot, pl.ds(c, tm, stride=nseg), :] = x[:, c * LANES:(c + 1) * LANES]
            cp.wait()

            def issue(grp, carry):
                for u in range(DMA_UNROLL):
                    for k in range(TOP_K):
                        row_copy(slot, k, grp * DMA_UNROLL + u).start(priority=k % DMA_QUEUES)
                return carry

            _for_real_groups(i, tm, chunks_per_row, pad, issue)

            @pl.when(i == last)
            def _():
                drain_slot(slot, i)

                @pl.when(i >= 1)
                def _():
                    drain_slot(1 - slot, i - 1)


def _dispatch(t, dest_tiles, pend, padded, n_used, n_rows, tm, chunks_per_row, pad):
    nt, d = t.shape
    nseg = d // LANES
    return pl.pallas_call(
        functools.partial(_dispatch_kernel, tm=tm, nseg=nseg, n_blk=n_rows // MOE_ROWS,
                          chunks_per_row=chunks_per_row, pad=pad),
        grid_spec=pltpu.PrefetchScalarGridSpec(
            num_scalar_prefetch=3,
            grid=(nt // tm,),
            in_specs=[pl.BlockSpec(memory_space=pl.ANY),
                      pl.BlockSpec((tm, d), lambda i, *_: (i, 0))],
            out_specs=pl.BlockSpec(memory_space=pl.ANY),
            scratch_shapes=[pltpu.SMEM((2 * TOP_K * tm,), I32), pltpu.VMEM((2, tm * nseg, LANES), t.dtype),
                            pltpu.VMEM((MOE_ROWS * nseg, LANES), t.dtype),
                            pltpu.SemaphoreType.DMA(()), pltpu.SemaphoreType.DMA((2,)),
                            pltpu.SemaphoreType.DMA(())],
        ),
        out_shape=jax.ShapeDtypeStruct((n_rows * nseg, LANES), t.dtype),
        compiler_params=pltpu.CompilerParams(dimension_semantics=("arbitrary",), has_side_effects=True,
                                             vmem_limit_bytes=VMEM_LIMIT),
        name="moe_dispatch",
    )(pend, padded, n_used, dest_tiles, t)


def _ffn_kernel(blk_e_ref, nused_ref, x_ref, w1_ref, b1g_ref, b1l_ref, w2_ref, b2_ref, y_ref,
                w1g_ref, w1l_ref, w2c_ref, *, nseg):
    i = pl.program_id(0)
    rows = x_ref.shape[0] // nseg
    e = blk_e_ref[i]
    e_prev = blk_e_ref[jnp.maximum(i - 1, 0)]

    @pl.when(jnp.logical_or(i == 0, e != e_prev))
    def _():
        w = 2 * LANES
        r = lax.broadcasted_iota(I32, (w, w), 0)
        c = lax.broadcasted_iota(I32, (w, w), 1)
        src_col = jnp.where(c < LANES, 2 * c, 2 * (c - LANES) + 1)
        perm = jnp.where(r == src_col, 1.0, 0.0).astype(BF16)
        for cc in range(w1_ref.shape[3] // w):
            chunk = w1_ref[0, 0, :, cc * w:(cc + 1) * w].astype(BF16)
            res = jnp.dot(chunk, perm, preferred_element_type=F32).astype(BF16)
            w1g_ref[:, cc * LANES:(cc + 1) * LANES] = res[:, :LANES]
            w1l_ref[:, cc * LANES:(cc + 1) * LANES] = res[:, LANES:]
        w2c_ref[...] = w2_ref[0, 0].astype(BF16)

    @pl.when(i < nused_ref[0])
    def _():
        x = jnp.concatenate([x_ref[pl.ds(c, rows, stride=nseg), :] for c in range(nseg)], axis=1).astype(BF16)
        hg = jnp.dot(x, w1g_ref[...], preferred_element_type=F32) + b1g_ref[0]
        hl = jnp.dot(x, w1l_ref[...], preferred_element_type=F32) + b1l_ref[0]
        xg = jnp.minimum(hg, SWIGLU_LIMIT)
        xl = jnp.clip(hl, -SWIGLU_LIMIT, SWIGLU_LIMIT)
        act = xg * _sigmoid(SWIGLU_ALPHA * xg) * (xl + 1.0)
        y = jnp.dot(act.astype(BF16), w2c_ref[...], preferred_element_type=F32) + b2_ref[0]
        for c in range(nseg):
            y_ref[pl.ds(c, rows, stride=nseg), :] = y[:, c * LANES:(c + 1) * LANES]

    @pl.when(i >= nused_ref[0])
    def _():
        y_ref[...] = jnp.zeros(y_ref.shape, F32)


def _ffn(x_buf, blk_e, n_used, layer, w1, b1g, b1l, w2, b2):
    _, ne, d, f2 = w1.shape
    f = f2 // 2
    nseg = d // LANES
    n_rows = x_buf.shape[0] // nseg
    n_blk = n_rows // MOE_ROWS
    assert f2 % (2 * LANES) == 0
    wspec = lambda a, b: pl.BlockSpec((1, a, b), lambda i, be, nu: (be[i], 0, 0))
    wfull = lambda a, b: pl.BlockSpec((1, 1, a, b), lambda i, be, nu: (layer, be[i], 0, 0))
    return pl.pallas_call(
        functools.partial(_ffn_kernel, nseg=nseg),
        grid_spec=pltpu.PrefetchScalarGridSpec(
            num_scalar_prefetch=2,
            grid=(n_blk,),
            in_specs=[pl.BlockSpec((MOE_ROWS * nseg, LANES), lambda i, be, nu: (i, 0)),
                      wfull(d, f2), wspec(1, f), wspec(1, f), wfull(f, d), wspec(1, d)],
            out_specs=pl.BlockSpec((MOE_ROWS * nseg, LANES), lambda i, be, nu: (i, 0)),
            scratch_shapes=[pltpu.VMEM((d, f), BF16), pltpu.VMEM((d, f), BF16), pltpu.VMEM((f, d), BF16)],
        ),
        out_shape=jax.ShapeDtypeStruct((n_rows * nseg, LANES), F32),
        compiler_params=_cparams("arbitrary"),
        name="moe_ffn",
    )(blk_e, n_used, x_buf, w1, b1g, b1l, w2, b2)


def _combine_kernel(dest_hbm, ybuf_hbm, gate_ref, h_ref, g_ref, b_ref, o_ref, idx_ref, rows_ref, isem, sem,
                    *, tm, nseg, alpha, chunks_per_row, pad):
    i = pl.program_id(0)
    n = TOP_K * tm

    def row_copy(slot, k, r):
        src = ybuf_hbm.at[pl.ds(pl.multiple_of(idx_ref[slot * n + k * tm + r] * nseg, nseg), nseg), :]
        dst = rows_ref.at[slot * TOP_K + k, pl.ds(pl.multiple_of(r * nseg, nseg), nseg), :]
        return pltpu.make_async_copy(src, dst, sem.at[slot])

    def gather_tile(tile, slot):
        cp = pltpu.make_async_copy(dest_hbm.at[pl.ds(pl.multiple_of(tile * n, n), n)],
                                   idx_ref.at[pl.ds(slot * n, n)], isem)
        cp.start()
        cp.wait()

        def issue(grp, carry):
            for u in range(DMA_UNROLL):
                for k in range(TOP_K):
                    row_copy(slot, k, grp * DMA_UNROLL + u).start(priority=k % DMA_QUEUES)
            return carry

        _for_real_groups(tile, tm, chunks_per_row, pad, issue)

    @pl.when(i == 0)
    def _():
        rows_ref[...] = jnp.zeros(rows_ref.shape, F32)
        gather_tile(0, 0)

    for nxt in range(2):
        @pl.when(jnp.logical_and(i + 1 < pl.num_programs(0), (i + 1) % 2 == nxt))
        def _():
            gather_tile(i + 1, nxt)

    slot = i % 2

    def drain(grp, carry):
        for u in range(DMA_UNROLL):
            for k in range(TOP_K):
                row_copy(slot, k, grp * DMA_UNROLL + u).wait()
        return carry

    _for_real_groups(i, tm, chunks_per_row, pad, drain)
    gates = gate_ref[...]
    y = alpha * h_ref[...]
    for k in range(TOP_K):
        rows = jnp.concatenate([rows_ref[slot * TOP_K + k, pl.ds(c, tm, stride=nseg), :] for c in range(nseg)],
                               axis=1)
        y = y + gates[:, k:k + 1] * rows
    o_ref[...] = _layer_norm(y, g_ref[...], b_ref[...])


def _combine(y_buf, dest_tiles, gates_t, h, g, b, alpha, tm, chunks_per_row, pad):
    nt, d = h.shape
    nseg = d // LANES
    return pl.pallas_call(
        functools.partial(_combine_kernel, tm=tm, nseg=nseg, alpha=alpha, chunks_per_row=chunks_per_row, pad=pad),
        grid=(nt // tm,),
        in_specs=[pl.BlockSpec(memory_space=pl.ANY),
                  pl.BlockSpec(memory_space=pl.ANY),
                  pl.BlockSpec((tm, TOP_K), lambda i: (i, 0)),
                  pl.BlockSpec((tm, d), lambda i: (i, 0)),
                  pl.BlockSpec((1, d), lambda i: (0, 0)),
                  pl.BlockSpec((1, d), lambda i: (0, 0))],
        out_specs=pl.BlockSpec((tm, d), lambda i: (i, 0)),
        out_shape=jax.ShapeDtypeStruct((nt, d), F32),
        scratch_shapes=[pltpu.SMEM((2 * TOP_K * tm,), I32), pltpu.VMEM((2 * TOP_K, tm * nseg, LANES), F32),
                        pltpu.SemaphoreType.DMA(()), pltpu.SemaphoreType.DMA((2,))],
        compiler_params=_cparams("arbitrary"),
        name="moe_combine",
    )(dest_tiles, y_buf, gates_t, h, g, b)


def _moe_layer(t, layer, lp, pad, w_r, b_r, w1, b1, w2, b2, ln_g, ln_b, alpha):
    nt, d = t.shape
    ne = w_r.shape[1]
    tm_r = _pick(nt, (512, 256, 128))
    tm_d = _pick(nt, (512, 256, 128))
    cpr = lp // SSM_CHUNK
    assert lp % SSM_CHUNK == 0 and pad % DMA_UNROLL == 0 and pad < SSM_CHUNK
    top_e, gates, rank, cnt = _router(t, w_r, b_r, tm_r, cpr, pad)

    counts = cnt[:, 0]
    padded = ((counts + MOE_ROWS - 1) // MOE_ROWS) * MOE_ROWS
    pend = jnp.cumsum(padded)
    poff = pend - padded
    expert_ids = jnp.arange(ne, dtype=I32)[:, None, None]
    dest = jnp.sum(jnp.where(top_e[None] == expert_ids, poff[:, None, None], 0), axis=0) + rank
    n_real = (nt - (nt // lp) * pad) * TOP_K
    n_rows = -(-n_real // MOE_ROWS) * MOE_ROWS + ne * MOE_ROWS
    n_blk = n_rows // MOE_ROWS
    blk_start = jnp.arange(n_blk, dtype=I32) * MOE_ROWS
    blk_e = jnp.minimum(jnp.sum(blk_start[:, None] >= pend[None, :], axis=1), ne - 1).astype(I32)
    n_used = (pend[-1:] // MOE_ROWS).astype(I32)
    dest_tiles = dest.astype(I32).reshape(TOP_K, nt // tm_d, tm_d).transpose(1, 0, 2).reshape(-1)

    x_buf = _dispatch(t, dest_tiles, pend.astype(I32), padded.astype(I32), n_used, n_rows, tm_d, cpr, pad)
    y_buf = _ffn(x_buf, blk_e, n_used, layer, w1, b1[:, None, 0::2], b1[:, None, 1::2], w2, b2[:, None, :])
    return _combine(y_buf, dest_tiles, gates.T, t, ln_g[None, :], ln_b[None, :], alpha, tm_d, cpr, pad)


def _conformer_layer(h, w1, b1, dw, dwb, ln_g, ln_b, w2, b2, mix_g, mix_b, alpha, pad):
    bsz, lp, d = h.shape
    c = dw.shape[1]
    tl = _pick(lp, (544, 272, 384, 192, 128))
    u = _glu(h, w1[:, :c].astype(BF16), w1[:, c:].astype(BF16), b1[None, :c], b1[None, c:], pad, tl)
    return _conv_mix(u, h, dw, dwb[None, :], ln_g[None, :], ln_b[None, :], w2.astype(BF16), b2[None, :],
                     mix_g[None, :], mix_b[None, :], alpha, tl)


def _mamba_layer(h, w_in, conv_w, conv_b, dt_bias, a_log, d_skip, norm_g, w_out, mix_g, mix_b, alpha, pad):
    bsz, lp, d = h.shape
    nt = bsz * lp
    d_inner = w_out.shape[0]
    nh = dt_bias.shape[0]
    conv_dim = conv_w.shape[1]
    nproj = d_inner + conv_dim
    hpg = nh // SSM_GROUPS
    tm = _pick(nt, (1024, 512, 256, 128))
    proj = _mm(h.reshape(nt, d), w_in[:, :nproj].astype(BF16), tm, _pick(nproj, (2048, 1024, 512, 256, 128)))
    a_neg = -jnp.exp(a_log.astype(F32))
    dtc, csc, csr = _dt_prep(h, w_in[:, nproj:], dt_bias, a_neg, pad)
    nc = lp // SSM_CHUNK
    csr = csr.reshape(bsz, nc, SSM_GROUPS, hpg, SSM_CHUNK).transpose(0, 2, 1, 3, 4)
    d_exp = jnp.repeat(d_skip.astype(F32), SSM_HEAD_DIM)[None, :]
    y = _ssd(proj.reshape(bsz, lp, nproj), dtc, csc, csr, conv_w, conv_b[None, :], d_exp, norm_g[None, :],
             pad, d_inner)
    out = _mm_resid_ln(y.reshape(nt, d_inner), w_out.astype(BF16), h.reshape(nt, d),
                       mix_g[None, :], mix_b[None, :], alpha, _pick(nt, (512, 256, 128)))
    return out.reshape(bsz, lp, d)


def kernel(x, meta_tokens, conv_w1, conv_b1, conv_dw, conv_dwb, conv_ln_g, conv_ln_b, conv_w2, conv_b2,
           ssm_w_in, ssm_conv_w, ssm_conv_b, ssm_dt_bias, ssm_a_log, ssm_d, ssm_norm_g, ssm_w_out,
           moe_w_router, moe_b_router, moe_w1, moe_b1, moe_w2, moe_b2,
           ln_mix_g, ln_mix_b, ln_ffn_g, ln_ffn_b):
    bsz, seq, d = x.shape
    depth = ln_mix_g.shape[0]
    alpha = (2 * depth) ** 0.25
    pad = SSM_CHUNK - N_META
    lp = pad + N_META + seq
    assert lp % SSM_CHUNK == 0
    meta = jnp.broadcast_to(meta_tokens.astype(x.dtype)[None], (bsz, N_META, d))
    h = jnp.concatenate([jnp.zeros((bsz, pad, d), x.dtype), meta, x], axis=1)
    for i in range(depth):
        j = i // 2
        if i % 2 == 0:
            h = _conformer_layer(h, conv_w1[j], conv_b1[j], conv_dw[j], conv_dwb[j], conv_ln_g[j],
                                 conv_ln_b[j], conv_w2[j], conv_b2[j], ln_mix_g[i], ln_mix_b[i], alpha, pad)
        else:
            h = _mamba_layer(h, ssm_w_in[j], ssm_conv_w[j], ssm_conv_b[j], ssm_dt_bias[j], ssm_a_log[j],
                             ssm_d[j], ssm_norm_g[j], ssm_w_out[j], ln_mix_g[i], ln_mix_b[i], alpha, pad)
        t = _moe_layer(h.reshape(bsz * lp, d), i, lp, pad, moe_w_router[i], moe_b_router[i], moe_w1, moe_b1[i],
                       moe_w2, moe_b2[i], ln_ffn_g[i], ln_ffn_b[i], alpha)
        h = t.reshape(bsz, lp, d)
    return h[:, pad + N_META:]
```

```python
import functools

import jax
import jax.numpy as jnp
from jax import lax
from jax.experimental import pallas as pl
from jax.experimental.pallas import tpu as pltpu

F32 = jnp.float32
BF16 = jnp.bfloat16
I32 = jnp.int32
HIGHEST = lax.Precision.HIGHEST

N_META = 16
SSM_HEAD_DIM = 64
SSM_GROUPS = 8
SSM_STATE = 128
SSM_CHUNK = 128
TOP_K = 4
SWIGLU_ALPHA = 1.702
SWIGLU_LIMIT = 7.0
LN_EPS = 1e-5
RMS_EPS = 1e-5

CONV_HALO = 32
CONV_ROWS = 32
CONV_LANES = 256
MOE_ROWS = 512
DMA_UNROLL = 8
DMA_QUEUES = 2
LANES = 128
VMEM_LIMIT = 56 * 1024 * 1024


def _cparams(*sem):
    return pltpu.CompilerParams(dimension_semantics=tuple(sem), vmem_limit_bytes=VMEM_LIMIT)


def _pick(n, prefs):
    for p in prefs:
        if n % p == 0:
            return p
    raise ValueError(f"no tile for {n} in {prefs}")


def _layer_norm(y, g, b):
    mu = jnp.mean(y, axis=-1, keepdims=True)
    yc = y - mu
    var = jnp.mean(yc * yc, axis=-1, keepdims=True)
    return yc * lax.rsqrt(var + LN_EPS) * g + b


def _sigmoid(x):
    return 1.0 / (1.0 + jnp.exp(-x))


def _softplus(x):
    return jnp.maximum(x, 0.0) + jnp.log(1.0 + jnp.exp(-jnp.abs(x)))


def _glu_kernel(x_ref, wa_ref, wg_ref, ba_ref, bg_ref, o_ref, *, pad):
    j = pl.program_id(1)
    x = x_ref[0].astype(BF16)
    a = jnp.dot(x, wa_ref[...], preferred_element_type=F32) + ba_ref[...]
    g = jnp.dot(x, wg_ref[...], preferred_element_type=F32) + bg_ref[...]
    u = a * _sigmoid(g)
    row = lax.broadcasted_iota(I32, u.shape, 0)
    valid = jnp.logical_or(j > 0, row >= pad)
    o_ref[0] = jnp.where(valid, u, 0.0)


def _glu(h, wa, wg, ba, bg, pad, tl):
    bsz, lp, d = h.shape
    c = wa.shape[1]
    return pl.pallas_call(
        functools.partial(_glu_kernel, pad=pad),
        grid=(bsz, lp // tl),
        in_specs=[
            pl.BlockSpec((1, tl, d), lambda b, j: (b, j, 0)),
            pl.BlockSpec((d, c), lambda b, j: (0, 0)),
            pl.BlockSpec((d, c), lambda b, j: (0, 0)),
            pl.BlockSpec((1, c), lambda b, j: (0, 0)),
            pl.BlockSpec((1, c), lambda b, j: (0, 0)),
        ],
        out_specs=pl.BlockSpec((1, tl, c), lambda b, j: (b, j, 0)),
        out_shape=jax.ShapeDtypeStruct((bsz, lp, c), F32),
        compiler_params=_cparams("parallel", "parallel"),
        name="conf_glu",
    )(h, wa, wg, ba, bg)


def _conv_mix_kernel(ucur_ref, uprev_ref, h_ref, dw_ref, dwb_ref, g1_ref, b1_ref, w2_ref, b2_ref,
                     g2_ref, bb2_ref, o_ref, win_ref, act_ref, *, taps, alpha):
    j = pl.program_id(1)
    tl = ucur_ref.shape[1]

    @pl.when(j == 0)
    def _():
        win_ref[0:CONV_HALO, :] = jnp.zeros((CONV_HALO, win_ref.shape[1]), F32)

    @pl.when(j > 0)
    def _():
        win_ref[0:CONV_HALO, :] = uprev_ref[0, tl - CONV_HALO:tl, :]

    win_ref[CONV_HALO:CONV_HALO + tl, :] = ucur_ref[0]

    def chunk(c, carry):
        base = pl.multiple_of(c * CONV_ROWS, CONV_ROWS)
        n = CONV_HALO + CONV_ROWS
        parts = []
        for lc in range(win_ref.shape[1] // CONV_LANES):
            sl = slice(lc * CONV_LANES, (lc + 1) * CONV_LANES)
            win = win_ref[pl.ds(base, n), sl]
            acc = jnp.zeros((CONV_ROWS, CONV_LANES), F32) + dwb_ref[:, sl]
            for s in range(8):
                ws = win if s == 0 else pltpu.roll(win, n - s, 0)
                for k in range(taps):
                    off = CONV_HALO + k - (taps - 1)
                    if off % 8 == s:
                        acc = acc + ws[off - s:off - s + CONV_ROWS, :] * dw_ref[k:k + 1, sl]
            parts.append(acc)
        y = _layer_norm(jnp.concatenate(parts, axis=1), g1_ref[...], b1_ref[...])
        y = y * _sigmoid(y)
        act_ref[pl.ds(base, CONV_ROWS), :] = y.astype(BF16)
        return carry

    lax.fori_loop(0, tl // CONV_ROWS, chunk, 0, unroll=2)
    mix = jnp.dot(act_ref[...], w2_ref[...], preferred_element_type=F32) + b2_ref[...]
    y = alpha * h_ref[0] + mix
    o_ref[0] = _layer_norm(y, g2_ref[...], bb2_ref[...])


def _conv_mix(u, h, dw, dwb, g1, b1, w2, b2, g2, bb2, alpha, tl):
    bsz, lp, c = u.shape
    d = h.shape[2]
    taps = dw.shape[0]
    assert taps - 1 <= CONV_HALO and tl % CONV_ROWS == 0
    vec = lambda n: pl.BlockSpec((1, n), lambda b, j: (0, 0))
    return pl.pallas_call(
        functools.partial(_conv_mix_kernel, taps=taps, alpha=alpha),
        grid=(bsz, lp // tl),
        in_specs=[
            pl.BlockSpec((1, tl, c), lambda b, j: (b, j, 0)),
            pl.BlockSpec((1, tl, c), lambda b, j: (b, jnp.maximum(j - 1, 0), 0)),
            pl.BlockSpec((1, tl, d), lambda b, j: (b, j, 0)),
            pl.BlockSpec((taps, c), lambda b, j: (0, 0)),
            vec(c), vec(c), vec(c),
            pl.BlockSpec((c, d), lambda b, j: (0, 0)),
            vec(d), vec(d), vec(d),
        ],
        out_specs=pl.BlockSpec((1, tl, d), lambda b, j: (b, j, 0)),
        out_shape=jax.ShapeDtypeStruct((bsz, lp, d), F32),
        scratch_shapes=[pltpu.VMEM((CONV_HALO + tl, c), F32), pltpu.VMEM((tl, c), BF16)],
        compiler_params=_cparams("parallel", "parallel"),
        name="conf_conv_mix",
    )(u, u, h, dw, dwb, g1, b1, w2, b2, g2, bb2)


def _mm_kernel(x_ref, w_ref, o_ref):
    o_ref[...] = jnp.dot(x_ref[...].astype(BF16), w_ref[...], preferred_element_type=F32).astype(o_ref.dtype)


def _mm(x, w, tm, tn, out_dtype=F32):
    m, k = x.shape
    n = w.shape[1]
    return pl.pallas_call(
        _mm_kernel,
        grid=(n // tn, m // tm),
        in_specs=[pl.BlockSpec((tm, k), lambda jn, im: (im, 0)),
                  pl.BlockSpec((k, tn), lambda jn, im: (0, jn))],
        out_specs=pl.BlockSpec((tm, tn), lambda jn, im: (im, jn)),
        out_shape=jax.ShapeDtypeStruct((m, n), out_dtype),
        compiler_params=_cparams("parallel", "parallel"),
        name="mm",
    )(x, w)


def _mm_resid_ln_kernel(x_ref, w_ref, h_ref, g_ref, b_ref, o_ref, *, alpha):
    mix = jnp.dot(x_ref[...].astype(BF16), w_ref[...], preferred_element_type=F32)
    o_ref[...] = _layer_norm(alpha * h_ref[...] + mix, g_ref[...], b_ref[...])


def _mm_resid_ln(x, w, h, g, b, alpha, tm):
    m, k = x.shape
    d = w.shape[1]
    return pl.pallas_call(
        functools.partial(_mm_resid_ln_kernel, alpha=alpha),
        grid=(m // tm,),
        in_specs=[pl.BlockSpec((tm, k), lambda i: (i, 0)),
                  pl.BlockSpec((k, d), lambda i: (0, 0)),
                  pl.BlockSpec((tm, d), lambda i: (i, 0)),
                  pl.BlockSpec((1, d), lambda i: (0, 0)),
                  pl.BlockSpec((1, d), lambda i: (0, 0))],
        out_specs=pl.BlockSpec((tm, d), lambda i: (i, 0)),
        out_shape=jax.ShapeDtypeStruct((m, d), F32),
        compiler_params=_cparams("parallel"),
        name="mm_resid_ln",
    )(x, w, h, g, b)


def _dt_kernel(x_ref, wc_ref, bc_ref, ac_ref, dtc_ref, csc_ref, csr_ref, *, pad):
    c = pl.program_id(1)
    x = x_ref[0]
    n = x.shape[0]
    nh = csr_ref.shape[2]
    raw_c = jnp.dot(x, wc_ref[...], preferred_element_type=F32, precision=HIGHEST)
    row_c = lax.broadcasted_iota(I32, raw_c.shape, 0)
    dt_c = jnp.where(jnp.logical_and(c == 0, row_c < pad), 0.0, _softplus(raw_c + bc_ref[...]))
    li = lax.broadcasted_iota(I32, (n, n), 0)
    si = lax.broadcasted_iota(I32, (n, n), 1)
    tri = jnp.where(si <= li, 1.0, 0.0)
    cs_c = jnp.dot(tri, dt_c * ac_ref[...], preferred_element_type=F32, precision=HIGHEST)
    dtc_ref[0] = dt_c
    csc_ref[0] = cs_c
    csr_ref[0, 0] = cs_c.T[:nh, :]


def _dt_prep(h, w_dt, dt_bias, a_neg, pad):
    bsz, lp, d = h.shape
    nh = w_dt.shape[1]
    nc = lp // SSM_CHUNK
    lanes = 128
    wc = jnp.zeros((d, lanes), F32).at[:, :nh].set(w_dt)
    bc = jnp.zeros((1, lanes), F32).at[0, :nh].set(dt_bias)
    ac = jnp.zeros((1, lanes), F32).at[0, :nh].set(a_neg)
    return pl.pallas_call(
        functools.partial(_dt_kernel, pad=pad),
        grid=(bsz, nc),
        in_specs=[
            pl.BlockSpec((1, SSM_CHUNK, d), lambda b, c: (b, c, 0)),
            pl.BlockSpec((d, lanes), lambda b, c: (0, 0)),
            pl.BlockSpec((1, lanes), lambda b, c: (0, 0)),
            pl.BlockSpec((1, lanes), lambda b, c: (0, 0)),
        ],
        out_specs=[
            pl.BlockSpec((1, SSM_CHUNK, lanes), lambda b, c: (b, c, 0)),
            pl.BlockSpec((1, SSM_CHUNK, lanes), lambda b, c: (b, c, 0)),
            pl.BlockSpec((1, 1, nh, SSM_CHUNK), lambda b, c: (b, c, 0, 0)),
        ],
        out_shape=[
            jax.ShapeDtypeStruct((bsz, lp, lanes), F32),
            jax.ShapeDtypeStruct((bsz, lp, lanes), F32),
            jax.ShapeDtypeStruct((bsz, nc, nh, SSM_CHUNK), F32),
        ],
        compiler_params=_cparams("parallel", "parallel"),
        name="ssm_dt",
    )(h, wc, bc, ac)


def _ssd_kernel(xs_ref, bm_ref, cm_ref, z_ref, dtc_ref, csc_ref, csr_ref,
                cwx_ref, cwb_ref, cwc_ref, cbx_ref, cbb_ref, cbc_ref, dsk_ref, ng_ref,
                o_ref, state_ref, carx_ref, carb_ref, carc_ref, *, pad, hpg, taps):
    g = pl.program_id(1)
    q = SSM_CHUNK
    hd = SSM_HEAD_DIM
    gw = hpg * hd
    n_chunks = xs_ref.shape[1] // q

    state_ref[...] = jnp.zeros(state_ref.shape, F32)
    carx_ref[...] = jnp.zeros(carx_ref.shape, F32)
    carb_ref[...] = jnp.zeros(carb_ref.shape, F32)
    carc_ref[...] = jnp.zeros(carc_ref.shape, F32)

    hrow = lax.broadcasted_iota(I32, (128, gw), 0)
    hcol = lax.broadcasted_iota(I32, (128, gw), 1) // hd
    sel = jnp.where(hrow == g * hpg + hcol, 1.0, 0.0).astype(BF16)

    def select_heads(v):
        hi = v.astype(BF16)
        r1 = v - hi.astype(F32)
        mid = r1.astype(BF16)
        lo = (r1 - mid.astype(F32)).astype(BF16)
        out = jnp.dot(hi, sel, preferred_element_type=F32)
        out = out + jnp.dot(mid, sel, preferred_element_type=F32)
        return out + jnp.dot(lo, sel, preferred_element_type=F32)

    lane_head = lax.broadcasted_iota(I32, (q, gw), 1) // hd
    li = lax.broadcasted_iota(I32, (q, q), 0)
    si = lax.broadcasted_iota(I32, (q, q), 1)
    causal = si <= li
    row1 = lax.broadcasted_iota(I32, (q, 1), 0)

    def conv_act(raw, car_ref, w_ref, b_ref, valid):
        raw = jnp.where(valid, raw, 0.0)
        win = jnp.concatenate([car_ref[...], raw], axis=0)
        acc = jnp.zeros(raw.shape, F32) + b_ref[...]
        for k in range(taps):
            off = 8 + k - (taps - 1)
            acc = acc + win[off:off + q, :] * w_ref[k:k + 1, :]
        car_ref[...] = raw[q - 8:q, :]
        act = acc * _sigmoid(acc)
        return jnp.where(valid, act, 0.0)

    def chunk(c, carry):
        r0 = pl.multiple_of(c * q, q)
        valid = jnp.logical_or(c > 0, row1 >= pad)
        xs = conv_act(xs_ref[0, pl.ds(r0, q), :], carx_ref, cwx_ref, cbx_ref, valid)
        bm = conv_act(bm_ref[0, pl.ds(r0, q), :], carb_ref, cwb_ref, cbb_ref, valid)
        cm = conv_act(cm_ref[0, pl.ds(r0, q), :], carc_ref, cwc_ref, cbc_ref, valid)

        dt_e = select_heads(dtc_ref[0, pl.ds(r0, q), :])
        cs_e = select_heads(csc_ref[0, pl.ds(r0, q), :])
        cs_r = csr_ref[0, 0, c]
        a_last = cs_e[q - 1:q, :]

        x_dt = xs * dt_e
        xw = (x_dt * jnp.exp(a_last - cs_e)).astype(BF16)
        bm16 = bm.astype(BF16)
        cm16 = cm.astype(BF16)
        cb = lax.dot_general(cm16, bm16, (((1,), (1,)), ((), ())), preferred_element_type=F32)

        m_parts = []
        x_parts = []
        for r in range(hpg):
            col = cs_e[:, r * hd:r * hd + 1]
            seg = col - cs_r[r:r + 1, :]
            decay = jnp.exp(jnp.where(causal, seg, -jnp.inf))
            m_parts.append((cb * decay).astype(BF16))
            x_parts.append(jnp.where(lane_head == r, x_dt, 0.0).astype(BF16))
        m_cat = jnp.concatenate(m_parts, axis=1)
        x_bd = jnp.concatenate(x_parts, axis=0)
        y = jnp.dot(m_cat, x_bd, preferred_element_type=F32)

        st = state_ref[...]
        y = y + jnp.dot(cm16, st.astype(BF16), preferred_element_type=F32) * jnp.exp(cs_e)
        new = jnp.dot(bm.T.astype(BF16), xw, preferred_element_type=F32)
        state_ref[...] = st * jnp.exp(a_last) + new
        y = y + dsk_ref[...] * xs

        z = z_ref[0, pl.ds(r0, q), :]
        y = y * (z * _sigmoid(z))
        y = y * lax.rsqrt(jnp.mean(y * y, axis=-1, keepdims=True) + RMS_EPS) * ng_ref[...]
        o_ref[0, pl.ds(r0, q), :] = y.astype(o_ref.dtype)
        return carry

    lax.fori_loop(0, n_chunks, chunk, 0, unroll=4)


def _ssd(proj, dtc, csc, csr, conv_w, conv_b, d_exp, norm_g, pad, d_inner):
    bsz, lp, _ = proj.shape
    g = SSM_GROUPS
    n = SSM_STATE
    gw = d_inner // g
    hpg = gw // SSM_HEAD_DIM
    assert gw % 128 == 0 and n == 128 and dtc.shape[2] == 128
    taps = conv_w.shape[0]
    nc = lp // SSM_CHUNK
    zb = 0
    xb = d_inner // gw
    bb = 2 * d_inner // n
    cb_ = (2 * d_inner + g * n) // n
    cxb = 0
    cbb = d_inner // n
    ccb = (d_inner + g * n) // n
    seq = lambda w, off: pl.BlockSpec((1, lp, w), lambda b, j, off=off: (b, 0, off + j))
    cw = lambda w, off: pl.BlockSpec((taps, w), lambda b, j, off=off: (0, off + j))
    cbv = lambda w, off: pl.BlockSpec((1, w), lambda b, j, off=off: (0, off + j))
    return pl.pallas_call(
        functools.partial(_ssd_kernel, pad=pad, hpg=hpg, taps=taps),
        grid=(bsz, g),
        in_specs=[
            seq(gw, xb), seq(n, bb), seq(n, cb_), seq(gw, zb),
            pl.BlockSpec((1, lp, 128), lambda b, j: (b, 0, 0)),
            pl.BlockSpec((1, lp, 128), lambda b, j: (b, 0, 0)),
            pl.BlockSpec((1, 1, nc, hpg, SSM_CHUNK), lambda b, j: (b, j, 0, 0, 0)),
            cw(gw, cxb), cw(n, cbb), cw(n, ccb),
            cbv(gw, cxb), cbv(n, cbb), cbv(n, ccb),
            pl.BlockSpec((1, gw), lambda b, j: (0, j)),
            pl.BlockSpec((1, gw), lambda b, j: (0, j)),
        ],
        out_specs=pl.BlockSpec((1, lp, gw), lambda b, j: (b, 0, j)),
        out_shape=jax.ShapeDtypeStruct((bsz, lp, d_inner), BF16),
        scratch_shapes=[pltpu.VMEM((n, gw), F32), pltpu.VMEM((8, gw), F32),
                        pltpu.VMEM((8, n), F32), pltpu.VMEM((8, n), F32)],
        compiler_params=_cparams("parallel", "parallel"),
        name="ssm_ssd",
    )(proj, proj, proj, proj, dtc, csc, csr, conv_w, conv_w, conv_w, conv_b, conv_b, conv_b, d_exp, norm_g)


def _pad_chunk(tile, q, tm, chunks_per_row):
    return lax.rem(tile * (tm // SSM_CHUNK) + q, chunks_per_row) == 0


def _for_real_groups(tile, tm, chunks_per_row, pad, body):
    per_chunk = SSM_CHUNK // DMA_UNROLL
    for q in range(tm // SSM_CHUNK):
        skip = jnp.where(_pad_chunk(tile, q, tm, chunks_per_row), pad // DMA_UNROLL, 0)
        lax.fori_loop(q * per_chunk + skip, (q + 1) * per_chunk, body, 0)


def _router_kernel(x_ref, wr_ref, br_ref, e_ref, gate_ref, rank_ref, cnt_ref, carry_ref, *, chunks_per_row, pad):
    i = pl.program_id(0)

    @pl.when(i == 0)
    def _():
        carry_ref[...] = jnp.zeros(carry_ref.shape, F32)

    x = x_ref[...]
    tm = x.shape[0]
    ne = wr_ref.shape[0]
    logits = lax.dot_general(wr_ref[...], x, (((1,), (1,)), ((), ())),
                             preferred_element_type=F32, precision=HIGHEST) + br_ref[...]
    ids = lax.broadcasted_iota(I32, (ne, tm), 0)
    vals = logits
    top_v, top_e, sels = [], [], []
    for _ in range(TOP_K):
        m = jnp.max(vals, axis=0, keepdims=True)
        idx = jnp.min(jnp.where(vals == m, ids, ne), axis=0, keepdims=True)
        sel = ids == idx
        top_v.append(m)
        top_e.append(idx)
        sels.append(sel)
        vals = jnp.where(sel, -jnp.inf, vals)
    ex = [jnp.exp(v - top_v[0]) for v in top_v]
    den = ex[0]
    for t in ex[1:]:
        den = den + t
    e_ref[...] = jnp.concatenate(top_e, axis=0)
    gate_ref[...] = jnp.concatenate([t / den for t in ex], axis=0)

    onehot = sels[0].astype(F32)
    for s in sels[1:]:
        onehot = onehot + s.astype(F32)
    lane = lax.broadcasted_iota(I32, (1, tm), 1)
    real = jnp.ones((1, tm), F32)
    for q in range(tm // SSM_CHUNK):
        inert = jnp.logical_and(_pad_chunk(i, q, tm, chunks_per_row),
                                jnp.logical_and(lane >= q * SSM_CHUNK, lane < q * SSM_CHUNK + pad))
        real = jnp.where(inert, 0.0, real)
    onehot = onehot * real
    ji = lax.broadcasted_iota(I32, (tm, tm), 0)
    ti = lax.broadcasted_iota(I32, (tm, tm), 1)
    before = (ji < ti).astype(BF16)
    prefix = jnp.dot(onehot.astype(BF16), before, preferred_element_type=F32)
    prefix = prefix + carry_ref[:, 0:1]
    ranks = [jnp.sum(jnp.where(s, prefix, 0.0), axis=0, keepdims=True) for s in sels]
    rank_ref[...] = jnp.concatenate(ranks, axis=0).astype(I32)
    carry_ref[...] = carry_ref[...] + jnp.sum(onehot, axis=1, keepdims=True)
    cnt_ref[...] = carry_ref[...].astype(I32)


def _router(t, w_r, b_r, tm, chunks_per_row, pad):
    nt, d = t.shape
    ne = w_r.shape[1]
    assert tm % SSM_CHUNK == 0
    return pl.pallas_call(
        functools.partial(_router_kernel, chunks_per_row=chunks_per_row, pad=pad),
        grid=(nt // tm,),
        in_specs=[pl.BlockSpec((tm, d), lambda i: (i, 0)),
                  pl.BlockSpec((ne, d), lambda i: (0, 0)),
                  pl.BlockSpec((ne, 1), lambda i: (0, 0))],
        out_specs=[pl.BlockSpec((TOP_K, tm), lambda i: (0, i)),
                   pl.BlockSpec((TOP_K, tm), lambda i: (0, i)),
                   pl.BlockSpec((TOP_K, tm), lambda i: (0, i)),
                   pl.BlockSpec((ne, 128), lambda i: (0, 0))],
        out_shape=[jax.ShapeDtypeStruct((TOP_K, nt), I32),
                   jax.ShapeDtypeStruct((TOP_K, nt), F32),
                   jax.ShapeDtypeStruct((TOP_K, nt), I32),
                   jax.ShapeDtypeStruct((ne, 128), I32)],
        scratch_shapes=[pltpu.VMEM((ne, 128), F32)],
        compiler_params=_cparams("arbitrary"),
        name="moe_router",
    )(t, w_r.T, b_r[:, None])


def _dispatch_kernel(pend_ref, padded_ref, nused_ref, dest_hbm, t_ref, xbuf_hbm,
                     idx_ref, rows_ref, zero_ref, isem, sem, zsem, *, tm, nseg, n_blk, chunks_per_row, pad):
    i = pl.program_id(0)
    n = TOP_K * tm
    blk = MOE_ROWS * nseg

    @pl.when(i == 0)
    def _():
        zero_ref[...] = jnp.zeros(zero_ref.shape, zero_ref.dtype)

        def group_fill(e):
            return pltpu.make_async_copy(
                zero_ref, xbuf_hbm.at[pl.ds(pl.multiple_of((pend_ref[e] - MOE_ROWS) * nseg, blk), blk), :], zsem)

        def tail_fill(j):
            return pltpu.make_async_copy(zero_ref, xbuf_hbm.at[pl.ds(pl.multiple_of(j * blk, blk), blk), :], zsem)

        def start_group(e, carry):
            @pl.when(padded_ref[e] > 0)
            def _():
                group_fill(e).start()
            return carry

        def wait_group(e, carry):
            @pl.when(padded_ref[e] > 0)
            def _():
                group_fill(e).wait()
            return carry

        def start_tail(j, carry):
            tail_fill(j).start()
            return carry

        def wait_tail(j, carry):
            tail_fill(j).wait()
            return carry

        n_exp = pend_ref.shape[0]
        lax.fori_loop(0, n_exp, start_group, 0)
        lax.fori_loop(nused_ref[0], n_blk, start_tail, 0)
        lax.fori_loop(0, n_exp, wait_group, 0)
        lax.fori_loop(nused_ref[0], n_blk, wait_tail, 0)

    last = pl.num_programs(0) - 1

    def row_copy(slot, k, r):
        src = rows_ref.at[slot, pl.ds(pl.multiple_of(r * nseg, nseg), nseg), :]
        dst = xbuf_hbm.at[pl.ds(pl.multiple_of(idx_ref[slot * n + k * tm + r] * nseg, nseg), nseg), :]
        return pltpu.make_async_copy(src, dst, sem.at[slot])

    def drain_slot(slot, tile):
        def drain(grp, carry):
            for u in range(DMA_UNROLL):
                for k in range(TOP_K):
                    row_copy(slot, k, grp * DMA_UNROLL + u).wait()
            return carry

        _for_real_groups(tile, tm, chunks_per_row, pad, drain)

    for slot in range(2):
        @pl.when(i % 2 == slot)
        def _():
            @pl.when(i >= 2)
            def _():
                drain_slot(slot, i - 2)

            cp = pltpu.make_async_copy(dest_hbm.at[pl.ds(pl.multiple_of(i * n, n), n)],
                                       idx_ref.at[pl.ds(slot * n, n)], isem)
            cp.start()
            x = t_ref[...]
            for c in range(nseg):
                rows_ref[slot, pl.ds(c, tm, stride=nseg), :] = x[:, c * LANES:(c + 1) * LANES]
            cp.wait()

            def issue(grp, carry):
                for u in range(DMA_UNROLL):
                    for k in range(TOP_K):
                        row_copy(slot, k, grp * DMA_UNROLL + u).start(priority=k % DMA_QUEUES)
                return carry

            _for_real_groups(i, tm, chunks_per_row, pad, issue)

            @pl.when(i == last)
            def _():
                drain_slot(slot, i)

                @pl.when(i >= 1)
                def _():
                    drain_slot(1 - slot, i - 1)


def _dispatch(t, dest_tiles, pend, padded, n_used, n_rows, tm, chunks_per_row, pad):
    nt, d = t.shape
    nseg = d // LANES
    return pl.pallas_call(
        functools.partial(_dispatch_kernel, tm=tm, nseg=nseg, n_blk=n_rows // MOE_ROWS,
                          chunks_per_row=chunks_per_row, pad=pad),
        grid_spec=pltpu.PrefetchScalarGridSpec(
            num_scalar_prefetch=3,
            grid=(nt // tm,),
            in_specs=[pl.BlockSpec(memory_space=pl.ANY),
                      pl.BlockSpec((tm, d), lambda i, *_: (i, 0))],
            out_specs=pl.BlockSpec(memory_space=pl.ANY),
            scratch_shapes=[pltpu.SMEM((2 * TOP_K * tm,), I32), pltpu.VMEM((2, tm * nseg, LANES), t.dtype),
                            pltpu.VMEM((MOE_ROWS * nseg, LANES), t.dtype),
                            pltpu.SemaphoreType.DMA(()), pltpu.SemaphoreType.DMA((2,)),
                            pltpu.SemaphoreType.DMA(())],
        ),
        out_shape=jax.ShapeDtypeStruct((n_rows * nseg, LANES), t.dtype),
        compiler_params=pltpu.CompilerParams(dimension_semantics=("arbitrary",), has_side_effects=True,
                                             vmem_limit_bytes=VMEM_LIMIT),
        name="moe_dispatch",
    )(pend, padded, n_used, dest_tiles, t)


def _ffn_kernel(blk_e_ref, nused_ref, x_ref, w1_ref, b1g_ref, b1l_ref, w2_ref, b2_ref, y_ref,
                w1g_ref, w1l_ref, w2c_ref, *, nseg):
    i = pl.program_id(0)
    rows = x_ref.shape[0] // nseg
    e = blk_e_ref[i]
    e_prev = blk_e_ref[jnp.maximum(i - 1, 0)]

    @pl.when(jnp.logical_or(i == 0, e != e_prev))
    def _():
        w = 2 * LANES
        r = lax.broadcasted_iota(I32, (w, w), 0)
        c = lax.broadcasted_iota(I32, (w, w), 1)
        src_col = jnp.where(c < LANES, 2 * c, 2 * (c - LANES) + 1)
        perm = jnp.where(r == src_col, 1.0, 0.0).astype(BF16)
        for cc in range(w1_ref.shape[3] // w):
            chunk = w1_ref[0, 0, :, cc * w:(cc + 1) * w].astype(BF16)
            res = jnp.dot(chunk, perm, preferred_element_type=F32).astype(BF16)
            w1g_ref[:, cc * LANES:(cc + 1) * LANES] = res[:, :LANES]
            w1l_ref[:, cc * LANES:(cc + 1) * LANES] = res[:, LANES:]
        w2c_ref[...] = w2_ref[0, 0].astype(BF16)

    @pl.when(i < nused_ref[0])
    def _():
        x = jnp.concatenate([x_ref[pl.ds(c, rows, stride=nseg), :] for c in range(nseg)], axis=1).astype(BF16)
        hg = jnp.dot(x, w1g_ref[...], preferred_element_type=F32) + b1g_ref[0]
        hl = jnp.dot(x, w1l_ref[...], preferred_element_type=F32) + b1l_ref[0]
        xg = jnp.minimum(hg, SWIGLU_LIMIT)
        xl = jnp.clip(hl, -SWIGLU_LIMIT, SWIGLU_LIMIT)
        act = xg * _sigmoid(SWIGLU_ALPHA * xg) * (xl + 1.0)
        y = jnp.dot(act.astype(BF16), w2c_ref[...], preferred_element_type=F32) + b2_ref[0]
        for c in range(nseg):
            y_ref[pl.ds(c, rows, stride=nseg), :] = y[:, c * LANES:(c + 1) * LANES]

    @pl.when(i >= nused_ref[0])
    def _():
        y_ref[...] = jnp.zeros(y_ref.shape, F32)


def _ffn(x_buf, blk_e, n_used, layer, w1, b1g, b1l, w2, b2):
    _, ne, d, f2 = w1.shape
    f = f2 // 2
    nseg = d // LANES
    n_rows = x_buf.shape[0] // nseg
    n_blk = n_rows // MOE_ROWS
    assert f2 % (2 * LANES) == 0
    wspec = lambda a, b: pl.BlockSpec((1, a, b), lambda i, be, nu: (be[i], 0, 0))
    wfull = lambda a, b: pl.BlockSpec((1, 1, a, b), lambda i, be, nu: (layer, be[i], 0, 0))
    return pl.pallas_call(
        functools.partial(_ffn_kernel, nseg=nseg),
        grid_spec=pltpu.PrefetchScalarGridSpec(
            num_scalar_prefetch=2,
            grid=(n_blk,),
            in_specs=[pl.BlockSpec((MOE_ROWS * nseg, LANES), lambda i, be, nu: (i, 0)),
                      wfull(d, f2), wspec(1, f), wspec(1, f), wfull(f, d), wspec(1, d)],
            out_specs=pl.BlockSpec((MOE_ROWS * nseg, LANES), lambda i, be, nu: (i, 0)),
            scratch_shapes=[pltpu.VMEM((d, f), BF16), pltpu.VMEM((d, f), BF16), pltpu.VMEM((f, d), BF16)],
        ),
        out_shape=jax.ShapeDtypeStruct((n_rows * nseg, LANES), F32),
        compiler_params=_cparams("arbitrary"),
        name="moe_ffn",
    )(blk_e, n_used, x_buf, w1, b1g, b1l, w2, b2)


def _combine_kernel(dest_hbm, ybuf_hbm, gate_ref, h_ref, g_ref, b_ref, o_ref, idx_ref, rows_ref, isem, sem,
                    *, tm, nseg, alpha, chunks_per_row, pad):
    i = pl.program_id(0)
    n = TOP_K * tm

    def row_copy(slot, k, r):
        src = ybuf_hbm.at[pl.ds(pl.multiple_of(idx_ref[slot * n + k * tm + r] * nseg, nseg), nseg), :]
        dst = rows_ref.at[slot * TOP_K + k, pl.ds(pl.multiple_of(r * nseg, nseg), nseg), :]
        return pltpu.make_async_copy(src, dst, sem.at[slot])

    def gather_tile(tile, slot):
        cp = pltpu.make_async_copy(dest_hbm.at[pl.ds(pl.multiple_of(tile * n, n), n)],
                                   idx_ref.at[pl.ds(slot * n, n)], isem)
        cp.start()
        cp.wait()

        def issue(grp, carry):
            for u in range(DMA_UNROLL):
                for k in range(TOP_K):
                    row_copy(slot, k, grp * DMA_UNROLL + u).start(priority=k % DMA_QUEUES)
            return carry

        _for_real_groups(tile, tm, chunks_per_row, pad, issue)

    @pl.when(i == 0)
    def _():
        rows_ref[...] = jnp.zeros(rows_ref.shape, F32)
        gather_tile(0, 0)

    for nxt in range(2):
        @pl.when(jnp.logical_and(i + 1 < pl.num_programs(0), (i + 1) % 2 == nxt))
        def _():
            gather_tile(i + 1, nxt)

    slot = i % 2

    def drain(grp, carry):
        for u in range(DMA_UNROLL):
            for k in range(TOP_K):
                row_copy(slot, k, grp * DMA_UNROLL + u).wait()
        return carry

    _for_real_groups(i, tm, chunks_per_row, pad, drain)
    gates = gate_ref[...]
    y = alpha * h_ref[...]
    for k in range(TOP_K):
        rows = jnp.concatenate([rows_ref[slot * TOP_K + k, pl.ds(c, tm, stride=nseg), :] for c in range(nseg)],
                               axis=1)
        y = y + gates[:, k:k + 1] * rows
    o_ref[...] = _layer_norm(y, g_ref[...], b_ref[...])


def _combine(y_buf, dest_tiles, gates_t, h, g, b, alpha, tm, chunks_per_row, pad):
    nt, d = h.shape
    nseg = d // LANES
    return pl.pallas_call(
        functools.partial(_combine_kernel, tm=tm, nseg=nseg, alpha=alpha, chunks_per_row=chunks_per_row, pad=pad),
        grid=(nt // tm,),
        in_specs=[pl.BlockSpec(memory_space=pl.ANY),
                  pl.BlockSpec(memory_space=pl.ANY),
                  pl.BlockSpec((tm, TOP_K), lambda i: (i, 0)),
                  pl.BlockSpec((tm, d), lambda i: (i, 0)),
                  pl.BlockSpec((1, d), lambda i: (0, 0)),
                  pl.BlockSpec((1, d), lambda i: (0, 0))],
        out_specs=pl.BlockSpec((tm, d), lambda i: (i, 0)),
        out_shape=jax.ShapeDtypeStruct((nt, d), F32),
        scratch_shapes=[pltpu.SMEM((2 * TOP_K * tm,), I32), pltpu.VMEM((2 * TOP_K, tm * nseg, LANES), F32),
                        pltpu.SemaphoreType.DMA(()), pltpu.SemaphoreType.DMA((2,))],
        compiler_params=_cparams("arbitrary"),
        name="moe_combine",
    )(dest_tiles, y_buf, gates_t, h, g, b)


def _moe_layer(t, layer, lp, pad, w_r, b_r, w1, b1, w2, b2, ln_g, ln_b, alpha):
    nt, d = t.shape
    ne = w_r.shape[1]
    tm_r = _pick(nt, (512, 256, 128))
    tm_d = _pick(nt, (512, 256, 128))
    cpr = lp // SSM_CHUNK
    assert lp % SSM_CHUNK == 0 and pad % DMA_UNROLL == 0 and pad < SSM_CHUNK
    top_e, gates, rank, cnt = _router(t, w_r, b_r, tm_r, cpr, pad)

    counts = cnt[:, 0]
    padded = ((counts + MOE_ROWS - 1) // MOE_ROWS) * MOE_ROWS
    pend = jnp.cumsum(padded)
    poff = pend - padded
    expert_ids = jnp.arange(ne, dtype=I32)[:, None, None]
    dest = jnp.sum(jnp.where(top_e[None] == expert_ids, poff[:, None, None], 0), axis=0) + rank
    n_real = (nt - (nt // lp) * pad) * TOP_K
    n_rows = -(-n_real // MOE_ROWS) * MOE_ROWS + ne * MOE_ROWS
    n_blk = n_rows // MOE_ROWS
    blk_start = jnp.arange(n_blk, dtype=I32) * MOE_ROWS
    blk_e = jnp.minimum(jnp.sum(blk_start[:, None] >= pend[None, :], axis=1), ne - 1).astype(I32)
    n_used = (pend[-1:] // MOE_ROWS).astype(I32)
    dest_tiles = dest.astype(I32).reshape(TOP_K, nt // tm_d, tm_d).transpose(1, 0, 2).reshape(-1)

    x_buf = _dispatch(t, dest_tiles, pend.astype(I32), padded.astype(I32), n_used, n_rows, tm_d, cpr, pad)
    y_buf = _ffn(x_buf, blk_e, n_used, layer, w1, b1[:, None, 0::2], b1[:, None, 1::2], w2, b2[:, None, :])
    return _combine(y_buf, dest_tiles, gates.T, t, ln_g[None, :], ln_b[None, :], alpha, tm_d, cpr, pad)


def _conformer_layer(h, w1, b1, dw, dwb, ln_g, ln_b, w2, b2, mix_g, mix_b, alpha, pad):
    bsz, lp, d = h.shape
    c = dw.shape[1]
    tl = _pick(lp, (544, 272, 384, 192, 128))
    u = _glu(h, w1[:, :c].astype(BF16), w1[:, c:].astype(BF16), b1[None, :c], b1[None, c:], pad, tl)
    return _conv_mix(u, h, dw, dwb[None, :], ln_g[None, :], ln_b[None, :], w2.astype(BF16), b2[None, :],
                     mix_g[None, :], mix_b[None, :], alpha, tl)


def _mamba_layer(h, w_in, conv_w, conv_b, dt_bias, a_log, d_skip, norm_g, w_out, mix_g, mix_b, alpha, pad):
    bsz, lp, d = h.shape
    nt = bsz * lp
    d_inner = w_out.shape[0]
    nh = dt_bias.shape[0]
    conv_dim = conv_w.shape[1]
    nproj = d_inner + conv_dim
    hpg = nh // SSM_GROUPS
    tm = _pick(nt, (1024, 512, 256, 128))
    proj = _mm(h.reshape(nt, d), w_in[:, :nproj].astype(BF16), tm, _pick(nproj, (2048, 1024, 512, 256, 128)))
    a_neg = -jnp.exp(a_log.astype(F32))
    dtc, csc, csr = _dt_prep(h, w_in[:, nproj:], dt_bias, a_neg, pad)
    nc = lp // SSM_CHUNK
    csr = csr.reshape(bsz, nc, SSM_GROUPS, hpg, SSM_CHUNK).transpose(0, 2, 1, 3, 4)
    d_exp = jnp.repeat(d_skip.astype(F32), SSM_HEAD_DIM)[None, :]
    y = _ssd(proj.reshape(bsz, lp, nproj), dtc, csc, csr, conv_w, conv_b[None, :], d_exp, norm_g[None, :],
             pad, d_inner)
    out = _mm_resid_ln(y.reshape(nt, d_inner), w_out.astype(BF16), h.reshape(nt, d),
                       mix_g[None, :], mix_b[None, :], alpha, _pick(nt, (512, 256, 128)))
    return out.reshape(bsz, lp, d)


def kernel(x, meta_tokens, conv_w1, conv_b1, conv_dw, conv_dwb, conv_ln_g, conv_ln_b, conv_w2, conv_b2,
           ssm_w_in, ssm_conv_w, ssm_conv_b, ssm_dt_bias, ssm_a_log, ssm_d, ssm_norm_g, ssm_w_out,
           moe_w_router, moe_b_router, moe_w1, moe_b1, moe_w2, moe_b2,
           ln_mix_g, ln_mix_b, ln_ffn_g, ln_ffn_b):
    bsz, seq, d = x.shape
    depth = ln_mix_g.shape[0]
    alpha = (2 * depth) ** 0.25
    pad = SSM_CHUNK - N_META
    lp = pad + N_META + seq
    assert lp % SSM_CHUNK == 0
    meta = jnp.broadcast_to(meta_tokens.astype(x.dtype)[None], (bsz, N_META, d))
    h = jnp.concatenate([jnp.zeros((bsz, pad, d), x.dtype), meta, x], axis=1)
    for i in range(depth):
        j = i // 2
        if i % 2 == 0:
            h = _conformer_layer(h, conv_w1[j], conv_b1[j], conv_dw[j], conv_dwb[j], conv_ln_g[j],
                                 conv_ln_b[j], conv_w2[j], conv_b2[j], ln_mix_g[i], ln_mix_b[i], alpha, pad)
        else:
            h = _mamba_layer(h, ssm_w_in[j], ssm_conv_w[j], ssm_conv_b[j], ssm_dt_bias[j], ssm_a_log[j],
                             ssm_d[j], ssm_norm_g[j], ssm_w_out[j], ln_mix_g[i], ln_mix_b[i], alpha, pad)
        t = _moe_layer(h.reshape(bsz * lp, d), i, lp, pad, moe_w_router[i], moe_b_router[i], moe_w1, moe_b1[i],
                       moe_w2, moe_b2[i], ln_ffn_g[i], ln_ffn_b[i], alpha)
        h = t.reshape(bsz, lp, d)
    return h[:, pad + N_META:]
```

```python
import functools

import jax
import jax.numpy as jnp
from jax import lax
from jax.experimental import pallas as pl
from jax.experimental.pallas import tpu as pltpu

F32 = jnp.float32
BF16 = jnp.bfloat16
I32 = jnp.int32
HIGHEST = lax.Precision.HIGHEST

N_META = 16
SSM_HEAD_DIM = 64
SSM_GROUPS = 8
SSM_STATE = 128
SSM_CHUNK = 128
TOP_K = 4
SWIGLU_ALPHA = 1.702
SWIGLU_LIMIT = 7.0
LN_EPS = 1e-5
RMS_EPS = 1e-5

CONV_HALO = 32
CONV_ROWS = 32
CONV_LANES = 256
MOE_ROWS = 512
DMA_UNROLL = 8
DMA_QUEUES = 2
LANES = 128
VMEM_LIMIT = 56 * 1024 * 1024


def _cparams(*sem):
    return pltpu.CompilerParams(dimension_semantics=tuple(sem), vmem_limit_bytes=VMEM_LIMIT)


def _pick(n, prefs):
    for p in prefs:
        if n % p == 0:
            return p
    raise ValueError(f"no tile for {n} in {prefs}")


def _layer_norm(y, g, b):
    mu = jnp.mean(y, axis=-1, keepdims=True)
    yc = y - mu
    var = jnp.mean(yc * yc, axis=-1, keepdims=True)
    return yc * lax.rsqrt(var + LN_EPS) * g + b


def _silu(x):
    h = 0.5 * x
    return h + h * jnp.tanh(h)


def _sigmoid(x):
    return 1.0 / (1.0 + jnp.exp(-x))


def _softplus(x):
    return jnp.maximum(x, 0.0) + jnp.log(1.0 + jnp.exp(-jnp.abs(x)))


def _glu_kernel(x_ref, wa_ref, wg_ref, ba_ref, bg_ref, o_ref, *, pad):
    j = pl.program_id(1)
    x = x_ref[0].astype(BF16)
    a = jnp.dot(x, wa_ref[...], preferred_element_type=F32) + ba_ref[...]
    g = jnp.dot(x, wg_ref[...], preferred_element_type=F32) + bg_ref[...]
    u = a * _sigmoid(g)
    row = lax.broadcasted_iota(I32, u.shape, 0)
    valid = jnp.logical_or(j > 0, row >= pad)
    o_ref[0] = jnp.where(valid, u, 0.0)


def _glu(h, wa, wg, ba, bg, pad, tl):
    bsz, lp, d = h.shape
    c = wa.shape[1]
    return pl.pallas_call(
        functools.partial(_glu_kernel, pad=pad),
        grid=(bsz, lp // tl),
        in_specs=[
            pl.BlockSpec((1, tl, d), lambda b, j: (b, j, 0)),
            pl.BlockSpec((d, c), lambda b, j: (0, 0)),
            pl.BlockSpec((d, c), lambda b, j: (0, 0)),
            pl.BlockSpec((1, c), lambda b, j: (0, 0)),
            pl.BlockSpec((1, c), lambda b, j: (0, 0)),
        ],
        out_specs=pl.BlockSpec((1, tl, c), lambda b, j: (b, j, 0)),
        out_shape=jax.ShapeDtypeStruct((bsz, lp, c), F32),
        compiler_params=_cparams("parallel", "parallel"),
        name="conf_glu",
    )(h, wa, wg, ba, bg)


def _conv_mix_kernel(ucur_ref, uprev_ref, h_ref, dw_ref, dwb_ref, g1_ref, b1_ref, w2_ref, b2_ref,
                     g2_ref, bb2_ref, o_ref, win_ref, act_ref, *, taps, alpha):
    j = pl.program_id(1)
    tl = ucur_ref.shape[1]

    @pl.when(j == 0)
    def _():
        win_ref[0:CONV_HALO, :] = jnp.zeros((CONV_HALO, win_ref.shape[1]), F32)

    @pl.when(j > 0)
    def _():
        win_ref[0:CONV_HALO, :] = uprev_ref[0, tl - CONV_HALO:tl, :]

    win_ref[CONV_HALO:CONV_HALO + tl, :] = ucur_ref[0]

    def chunk(c, carry):
        base = pl.multiple_of(c * CONV_ROWS, CONV_ROWS)
        n = CONV_HALO + CONV_ROWS
        parts = []
        for lc in range(win_ref.shape[1] // CONV_LANES):
            sl = slice(lc * CONV_LANES, (lc + 1) * CONV_LANES)
            win = win_ref[pl.ds(base, n), sl]
            acc = jnp.zeros((CONV_ROWS, CONV_LANES), F32) + dwb_ref[:, sl]
            for s in range(8):
                ws = win if s == 0 else pltpu.roll(win, n - s, 0)
                for k in range(taps):
                    off = CONV_HALO + k - (taps - 1)
                    if off % 8 == s:
                        acc = acc + ws[off - s:off - s + CONV_ROWS, :] * dw_ref[k:k + 1, sl]
            parts.append(acc)
        y = _layer_norm(jnp.concatenate(parts, axis=1), g1_ref[...], b1_ref[...])
        y = _silu(y)
        act_ref[pl.ds(base, CONV_ROWS), :] = y.astype(BF16)
        return carry

    lax.fori_loop(0, tl // CONV_ROWS, chunk, 0, unroll=4)
    mix = jnp.dot(act_ref[...], w2_ref[...], preferred_element_type=F32) + b2_ref[...]
    y = alpha * h_ref[0] + mix
    o_ref[0] = _layer_norm(y, g2_ref[...], bb2_ref[...])


def _conv_mix(u, h, dw, dwb, g1, b1, w2, b2, g2, bb2, alpha, tl):
    bsz, lp, c = u.shape
    d = h.shape[2]
    taps = dw.shape[0]
    assert taps - 1 <= CONV_HALO and tl % CONV_ROWS == 0
    vec = lambda n: pl.BlockSpec((1, n), lambda b, j: (0, 0))
    return pl.pallas_call(
        functools.partial(_conv_mix_kernel, taps=taps, alpha=alpha),
        grid=(bsz, lp // tl),
        in_specs=[
            pl.BlockSpec((1, tl, c), lambda b, j: (b, j, 0)),
            pl.BlockSpec((1, tl, c), lambda b, j: (b, jnp.maximum(j - 1, 0), 0)),
            pl.BlockSpec((1, tl, d), lambda b, j: (b, j, 0)),
            pl.BlockSpec((taps, c), lambda b, j: (0, 0)),
            vec(c), vec(c), vec(c),
            pl.BlockSpec((c, d), lambda b, j: (0, 0)),
            vec(d), vec(d), vec(d),
        ],
        out_specs=pl.BlockSpec((1, tl, d), lambda b, j: (b, j, 0)),
        out_shape=jax.ShapeDtypeStruct((bsz, lp, d), F32),
        scratch_shapes=[pltpu.VMEM((CONV_HALO + tl, c), F32), pltpu.VMEM((tl, c), BF16)],
        compiler_params=_cparams("parallel", "parallel"),
        name="conf_conv_mix",
    )(u, u, h, dw, dwb, g1, b1, w2, b2, g2, bb2)


def _mm_kernel(x_ref, w_ref, o_ref):
    o_ref[...] = jnp.dot(x_ref[...].astype(BF16), w_ref[...], preferred_element_type=F32).astype(o_ref.dtype)


def _mm(x, w, tm, tn, out_dtype=F32):
    m, k = x.shape
    n = w.shape[1]
    return pl.pallas_call(
        _mm_kernel,
        grid=(n // tn, m // tm),
        in_specs=[pl.BlockSpec((tm, k), lambda jn, im: (im, 0)),
                  pl.BlockSpec((k, tn), lambda jn, im: (0, jn))],
        out_specs=pl.BlockSpec((tm, tn), lambda jn, im: (im, jn)),
        out_shape=jax.ShapeDtypeStruct((m, n), out_dtype),
        compiler_params=_cparams("parallel", "parallel"),
        name="mm",
    )(x, w)


def _mm_resid_ln_kernel(x_ref, w_ref, h_ref, g_ref, b_ref, o_ref, *, alpha):
    mix = jnp.dot(x_ref[...].astype(BF16), w_ref[...], preferred_element_type=F32)
    o_ref[...] = _layer_norm(alpha * h_ref[...] + mix, g_ref[...], b_ref[...])


def _mm_resid_ln(x, w, h, g, b, alpha, tm):
    m, k = x.shape
    d = w.shape[1]
    return pl.pallas_call(
        functools.partial(_mm_resid_ln_kernel, alpha=alpha),
        grid=(m // tm,),
        in_specs=[pl.BlockSpec((tm, k), lambda i: (i, 0)),
                  pl.BlockSpec((k, d), lambda i: (0, 0)),
                  pl.BlockSpec((tm, d), lambda i: (i, 0)),
                  pl.BlockSpec((1, d), lambda i: (0, 0)),
                  pl.BlockSpec((1, d), lambda i: (0, 0))],
        out_specs=pl.BlockSpec((tm, d), lambda i: (i, 0)),
        out_shape=jax.ShapeDtypeStruct((m, d), F32),
        compiler_params=_cparams("parallel"),
        name="mm_resid_ln",
    )(x, w, h, g, b)


def _dt_kernel(x_ref, wc_ref, bc_ref, ac_ref, dtc_ref, csc_ref, csr_ref, *, pad):
    c = pl.program_id(1)
    x = x_ref[0]
    n = x.shape[0]
    nh = csr_ref.shape[2]
    raw_c = jnp.dot(x, wc_ref[...], preferred_element_type=F32, precision=HIGHEST)
    row_c = lax.broadcasted_iota(I32, raw_c.shape, 0)
    dt_c = jnp.where(jnp.logical_and(c == 0, row_c < pad), 0.0, _softplus(raw_c + bc_ref[...]))
    li = lax.broadcasted_iota(I32, (n, n), 0)
    si = lax.broadcasted_iota(I32, (n, n), 1)
    tri = jnp.where(si <= li, 1.0, 0.0)
    cs_c = jnp.dot(tri, dt_c * ac_ref[...], preferred_element_type=F32, precision=HIGHEST)
    dtc_ref[0] = dt_c
    csc_ref[0] = cs_c
    csr_ref[0, 0] = cs_c.T[:nh, :]


def _dt_prep(h, w_dt, dt_bias, a_neg, pad):
    bsz, lp, d = h.shape
    nh = w_dt.shape[1]
    nc = lp // SSM_CHUNK
    lanes = 128
    wc = jnp.zeros((d, lanes), F32).at[:, :nh].set(w_dt)
    bc = jnp.zeros((1, lanes), F32).at[0, :nh].set(dt_bias)
    ac = jnp.zeros((1, lanes), F32).at[0, :nh].set(a_neg)
    return pl.pallas_call(
        functools.partial(_dt_kernel, pad=pad),
        grid=(bsz, nc),
        in_specs=[
            pl.BlockSpec((1, SSM_CHUNK, d), lambda b, c: (b, c, 0)),
            pl.BlockSpec((d, lanes), lambda b, c: (0, 0)),
            pl.BlockSpec((1, lanes), lambda b, c: (0, 0)),
            pl.BlockSpec((1, lanes), lambda b, c: (0, 0)),
        ],
        out_specs=[
            pl.BlockSpec((1, SSM_CHUNK, lanes), lambda b, c: (b, c, 0)),
            pl.BlockSpec((1, SSM_CHUNK, lanes), lambda b, c: (b, c, 0)),
            pl.BlockSpec((1, 1, nh, SSM_CHUNK), lambda b, c: (b, c, 0, 0)),
        ],
        out_shape=[
            jax.ShapeDtypeStruct((bsz, lp, lanes), F32),
            jax.ShapeDtypeStruct((bsz, lp, lanes), F32),
            jax.ShapeDtypeStruct((bsz, nc, nh, SSM_CHUNK), F32),
        ],
        compiler_params=_cparams("parallel", "parallel"),
        name="ssm_dt",
    )(h, wc, bc, ac)


def _ssd_kernel(xs_ref, bm_ref, cm_ref, z_ref, dtc_ref, csc_ref, csr_ref,
                cwx_ref, cwb_ref, cwc_ref, cbx_ref, cbb_ref, cbc_ref, dsk_ref, ng_ref,
                o_ref, state_ref, carx_ref, carb_ref, carc_ref, *, pad, hpg, taps):
    g = pl.program_id(1)
    q = SSM_CHUNK
    hd = SSM_HEAD_DIM
    gw = hpg * hd
    n_chunks = xs_ref.shape[1] // q

    state_ref[...] = jnp.zeros(state_ref.shape, F32)
    carx_ref[...] = jnp.zeros(carx_ref.shape, F32)
    carb_ref[...] = jnp.zeros(carb_ref.shape, F32)
    carc_ref[...] = jnp.zeros(carc_ref.shape, F32)

    hrow = lax.broadcasted_iota(I32, (128, gw), 0)
    hcol = lax.broadcasted_iota(I32, (128, gw), 1) // hd
    sel = jnp.where(hrow == g * hpg + hcol, 1.0, 0.0).astype(BF16)

    def select_heads(v):
        hi = v.astype(BF16)
        r1 = v - hi.astype(F32)
        mid = r1.astype(BF16)
        lo = (r1 - mid.astype(F32)).astype(BF16)
        out = jnp.dot(hi, sel, preferred_element_type=F32)
        out = out + jnp.dot(mid, sel, preferred_element_type=F32)
        return out + jnp.dot(lo, sel, preferred_element_type=F32)

    lane_head = lax.broadcasted_iota(I32, (q, gw), 1) // hd
    li = lax.broadcasted_iota(I32, (q, q), 0)
    si = lax.broadcasted_iota(I32, (q, q), 1)
    causal = si <= li
    row1 = lax.broadcasted_iota(I32, (q, 1), 0)

    def conv_act(raw, car_ref, w_ref, b_ref, valid):
        if valid is not None:
            raw = jnp.where(valid, raw, 0.0)
        win = jnp.concatenate([car_ref[...], raw], axis=0)
        acc = win * w_ref[0:1, :]
        for k in range(1, taps):
            acc = pltpu.roll(acc, 1, 0) + win * w_ref[k:k + 1, :]
        acc = acc[8:8 + q, :] + b_ref[...]
        car_ref[...] = raw[q - 8:q, :]
        act = _silu(acc)
        return act if valid is None else jnp.where(valid, act, 0.0)

    def chunk(c, first):
        r0 = 0 if first else pl.multiple_of(c * q, q)
        valid = (row1 >= pad) if first else None
        xs = conv_act(xs_ref[0, pl.ds(r0, q), :], carx_ref, cwx_ref, cbx_ref, valid)
        bm = conv_act(bm_ref[0, pl.ds(r0, q), :], carb_ref, cwb_ref, cbb_ref, valid)
        cm = conv_act(cm_ref[0, pl.ds(r0, q), :], carc_ref, cwc_ref, cbc_ref, valid)

        dt_e = select_heads(dtc_ref[0, pl.ds(r0, q), :])
        cs_e = select_heads(csc_ref[0, pl.ds(r0, q), :])
        cs_r = csr_ref[0, 0, c]
        a_last = cs_e[q - 1:q, :]

        x_dt = xs * dt_e
        xw = (x_dt * jnp.exp(a_last - cs_e)).astype(BF16)
        bm16 = bm.astype(BF16)
        cm16 = cm.astype(BF16)
        cb = lax.dot_general(cm16, bm16, (((1,), (1,)), ((), ())), preferred_element_type=F32)

        m_parts = []
        x_parts = []
        for r in range(hpg):
            col = cs_e[:, r * hd:r * hd + 1]
            seg = col - cs_r[r:r + 1, :]
            decay = jnp.exp(jnp.where(causal, seg, -jnp.inf))
            m_parts.append((cb * decay).astype(BF16))
            x_parts.append(jnp.where(lane_head == r, x_dt, 0.0).astype(BF16))
        m_cat = jnp.concatenate(m_parts, axis=1)
        x_bd = jnp.concatenate(x_parts, axis=0)
        y = jnp.dot(m_cat, x_bd, preferred_element_type=F32)

        st = state_ref[...]
        y = y + jnp.dot(cm16, st.astype(BF16), preferred_element_type=F32) * jnp.exp(cs_e)
        new = jnp.dot(bm.T.astype(BF16), xw, preferred_element_type=F32)
        state_ref[...] = st * jnp.exp(a_last) + new
        y = y + dsk_ref[...] * xs

        z = z_ref[0, pl.ds(r0, q), :]
        y = y * _silu(z)
        y = y * lax.rsqrt(jnp.mean(y * y, axis=-1, keepdims=True) + RMS_EPS) * ng_ref[...]
        o_ref[0, pl.ds(r0, q), :] = y.astype(o_ref.dtype)

    def later_chunk(c, carry):
        chunk(c, False)
        return carry

    chunk(0, True)
    lax.fori_loop(1, n_chunks, later_chunk, 0, unroll=4)


def _ssd(proj, dtc, csc, csr, conv_w, conv_b, d_exp, norm_g, pad, d_inner):
    bsz, lp, _ = proj.shape
    g = SSM_GROUPS
    n = SSM_STATE
    gw = d_inner // g
    hpg = gw // SSM_HEAD_DIM
    assert gw % 128 == 0 and n == 128 and dtc.shape[2] == 128
    taps = conv_w.shape[0]
    nc = lp // SSM_CHUNK
    zb = 0
    xb = d_inner // gw
    bb = 2 * d_inner // n
    cb_ = (2 * d_inner + g * n) // n
    cxb = 0
    cbb = d_inner // n
    ccb = (d_inner + g * n) // n
    seq = lambda w, off: pl.BlockSpec((1, lp, w), lambda b, j, off=off: (b, 0, off + j))
    cw = lambda w, off: pl.BlockSpec((taps, w), lambda b, j, off=off: (0, off + j))
    cbv = lambda w, off: pl.BlockSpec((1, w), lambda b, j, off=off: (0, off + j))
    return pl.pallas_call(
        functools.partial(_ssd_kernel, pad=pad, hpg=hpg, taps=taps),
        grid=(bsz, g),
        in_specs=[
            seq(gw, xb), seq(n, bb), seq(n, cb_), seq(gw, zb),
            pl.BlockSpec((1, lp, 128), lambda b, j: (b, 0, 0)),
            pl.BlockSpec((1, lp, 128), lambda b, j: (b, 0, 0)),
            pl.BlockSpec((1, 1, nc, hpg, SSM_CHUNK), lambda b, j: (b, j, 0, 0, 0)),
            cw(gw, cxb), cw(n, cbb), cw(n, ccb),
            cbv(gw, cxb), cbv(n, cbb), cbv(n, ccb),
            pl.BlockSpec((1, gw), lambda b, j: (0, j)),
            pl.BlockSpec((1, gw), lambda b, j: (0, j)),
        ],
        out_specs=pl.BlockSpec((1, lp, gw), lambda b, j: (b, 0, j)),
        out_shape=jax.ShapeDtypeStruct((bsz, lp, d_inner), BF16),
        scratch_shapes=[pltpu.VMEM((n, gw), F32), pltpu.VMEM((8, gw), F32),
                        pltpu.VMEM((8, n), F32), pltpu.VMEM((8, n), F32)],
        compiler_params=_cparams("parallel", "parallel"),
        name="ssm_ssd",
    )(proj, proj, proj, proj, dtc, csc, csr, conv_w, conv_w, conv_w, conv_b, conv_b, conv_b, d_exp, norm_g)


def _pad_chunk(tile, q, tm, chunks_per_row):
    return lax.rem(tile * (tm // SSM_CHUNK) + q, chunks_per_row) == 0


def _for_real_groups(tile, tm, chunks_per_row, pad, body):
    per_chunk = SSM_CHUNK // DMA_UNROLL
    for q in range(tm // SSM_CHUNK):
        skip = jnp.where(_pad_chunk(tile, q, tm, chunks_per_row), pad // DMA_UNROLL, 0)
        lax.fori_loop(q * per_chunk + skip, (q + 1) * per_chunk, body, 0)


def _router_kernel(x_ref, wr_ref, br_ref, e_ref, gate_ref, rank_ref, cnt_ref, carry_ref, *, chunks_per_row, pad):
    i = pl.program_id(0)

    @pl.when(i == 0)
    def _():
        carry_ref[...] = jnp.zeros(carry_ref.shape, F32)

    x = x_ref[...]
    tm = x.shape[0]
    ne = wr_ref.shape[0]
    logits = lax.dot_general(wr_ref[...], x, (((1,), (1,)), ((), ())),
                             preferred_element_type=F32, precision=HIGHEST) + br_ref[...]
    ids = lax.broadcasted_iota(I32, (ne, tm), 0)
    vals = logits
    top_v, top_e, sels = [], [], []
    for _ in range(TOP_K):
        m = jnp.max(vals, axis=0, keepdims=True)
        idx = jnp.min(jnp.where(vals == m, ids, ne), axis=0, keepdims=True)
        sel = ids == idx
        top_v.append(m)
        top_e.append(idx)
        sels.append(sel)
        vals = jnp.where(sel, -jnp.inf, vals)
    ex = [jnp.exp(v - top_v[0]) for v in top_v]
    den = ex[0]
    for t in ex[1:]:
        den = den + t
    e_ref[...] = jnp.concatenate(top_e, axis=0)
    gate_ref[...] = jnp.concatenate([t / den for t in ex], axis=0)

    onehot = sels[0].astype(F32)
    for s in sels[1:]:
        onehot = onehot + s.astype(F32)
    lane = lax.broadcasted_iota(I32, (1, tm), 1)
    real = jnp.ones((1, tm), F32)
    for q in range(tm // SSM_CHUNK):
        inert = jnp.logical_and(_pad_chunk(i, q, tm, chunks_per_row),
                                jnp.logical_and(lane >= q * SSM_CHUNK, lane < q * SSM_CHUNK + pad))
        real = jnp.where(inert, 0.0, real)
    onehot = onehot * real
    ji = lax.broadcasted_iota(I32, (tm, tm), 0)
    ti = lax.broadcasted_iota(I32, (tm, tm), 1)
    before = (ji < ti).astype(BF16)
    prefix = jnp.dot(onehot.astype(BF16), before, preferred_element_type=F32)
    prefix = prefix + carry_ref[:, 0:1]
    ranks = [jnp.sum(jnp.where(s, prefix, 0.0), axis=0, keepdims=True) for s in sels]
    rank_ref[...] = jnp.concatenate(ranks, axis=0).astype(I32)
    carry_ref[...] = carry_ref[...] + jnp.sum(onehot, axis=1, keepdims=True)
    cnt_ref[...] = carry_ref[...].astype(I32)


def _router(t, w_r, b_r, tm, chunks_per_row, pad):
    nt, d = t.shape
    ne = w_r.shape[1]
    assert tm % SSM_CHUNK == 0
    return pl.pallas_call(
        functools.partial(_router_kernel, chunks_per_row=chunks_per_row, pad=pad),
        grid=(nt // tm,),
        in_specs=[pl.BlockSpec((tm, d), lambda i: (i, 0)),
                  pl.BlockSpec((ne, d), lambda i: (0, 0)),
                  pl.BlockSpec((ne, 1), lambda i: (0, 0))],
        out_specs=[pl.BlockSpec((TOP_K, tm), lambda i: (0, i)),
                   pl.BlockSpec((TOP_K, tm), lambda i: (0, i)),
                   pl.BlockSpec((TOP_K, tm), lambda i: (0, i)),
                   pl.BlockSpec((ne, 128), lambda i: (0, 0))],
        out_shape=[jax.ShapeDtypeStruct((TOP_K, nt), I32),
                   jax.ShapeDtypeStruct((TOP_K, nt), F32),
                   jax.ShapeDtypeStruct((TOP_K, nt), I32),
                   jax.ShapeDtypeStruct((ne, 128), I32)],
        scratch_shapes=[pltpu.VMEM((ne, 128), F32)],
        compiler_params=_cparams("arbitrary"),
        name="moe_router",
    )(t, w_r.T, b_r[:, None])


def _dispatch_kernel(pend_ref, padded_ref, nused_ref, dest_hbm, t_ref, xbuf_hbm,
                     idx_ref, rows_ref, zero_ref, isem, sem, zsem, *, tm, nseg, n_blk, chunks_per_row, pad):
    i = pl.program_id(0)
    n = TOP_K * tm
    blk = MOE_ROWS * nseg

    @pl.when(i == 0)
    def _():
        zero_ref[...] = jnp.zeros(zero_ref.shape, zero_ref.dtype)

        def group_fill(e):
            return pltpu.make_async_copy(
                zero_ref, xbuf_hbm.at[pl.ds(pl.multiple_of((pend_ref[e] - MOE_ROWS) * nseg, blk), blk), :], zsem)

        def tail_fill(j):
            return pltpu.make_async_copy(zero_ref, xbuf_hbm.at[pl.ds(pl.multiple_of(j * blk, blk), blk), :], zsem)

        def start_group(e, carry):
            @pl.when(padded_ref[e] > 0)
            def _():
                group_fill(e).start()
            return carry

        def wait_group(e, carry):
            @pl.when(padded_ref[e] > 0)
            def _():
                group_fill(e).wait()
            return carry

        def start_tail(j, carry):
            tail_fill(j).start()
            return carry

        def wait_tail(j, carry):
            tail_fill(j).wait()
            return carry

        n_exp = pend_ref.shape[0]
        lax.fori_loop(0, n_exp, start_group, 0)
        lax.fori_loop(nused_ref[0], n_blk, start_tail, 0)
        lax.fori_loop(0, n_exp, wait_group, 0)
        lax.fori_loop(nused_ref[0], n_blk, wait_tail, 0)

    last = pl.num_programs(0) - 1

    def row_copy(slot, k, r):
        src = rows_ref.at[slot, pl.ds(pl.multiple_of(r * nseg, nseg), nseg), :]
        dst = xbuf_hbm.at[pl.ds(pl.multiple_of(idx_ref[slot * n + k * tm + r] * nseg, nseg), nseg), :]
        return pltpu.make_async_copy(src, dst, sem.at[slot])

    def drain_slot(slot, tile):
        def drain(grp, carry):
            for u in range(DMA_UNROLL):
                for k in range(TOP_K):
                    row_copy(slot, k, grp * DMA_UNROLL + u).wait()
            return carry

        _for_real_groups(tile, tm, chunks_per_row, pad, drain)

    for slot in range(2):
        @pl.when(i % 2 == slot)
        def _():
            @pl.when(i >= 2)
            def _():
                drain_slot(slot, i - 2)

            cp = pltpu.make_async_copy(dest_hbm.at[pl.ds(pl.multiple_of(i * n, n), n)],
                                       idx_ref.at[pl.ds(slot * n, n)], isem)
            cp.start()
            x = t_ref[...]
            for c in range(nseg):
                rows_ref[slot, pl.ds(c, tm, stride=nseg), :] = x[:, c * LANES:(c + 1) * LANES]
            cp.wait()

            def issue(grp, carry):
                for u in range(DMA_UNROLL):
                    for k in range(TOP_K):
                        row_copy(slot, k, grp * DMA_UNROLL + u).start(priority=k % DMA_QUEUES)
                return carry

            _for_real_groups(i, tm, chunks_per_row, pad, issue)

            @pl.when(i == last)
            def _():
                drain_slot(slot, i)

                @pl.when(i >= 1)
                def _():
                    drain_slot(1 - slot, i - 1)


def _dispatch(t, dest_tiles, pend, padded, n_used, n_rows, tm, chunks_per_row, pad):
    nt, d = t.shape
    nseg = d // LANES
    return pl.pallas_call(
        functools.partial(_dispatch_kernel, tm=tm, nseg=nseg, n_blk=n_rows // MOE_ROWS,
                          chunks_per_row=chunks_per_row, pad=pad),
        grid_spec=pltpu.PrefetchScalarGridSpec(
            num_scalar_prefetch=3,
            grid=(nt // tm,),
            in_specs=[pl.BlockSpec(memory_space=pl.ANY),
                      pl.BlockSpec((tm, d), lambda i, *_: (i, 0))],
            out_specs=pl.BlockSpec(memory_space=pl.ANY),
            scratch_shapes=[pltpu.SMEM((2 * TOP_K * tm,), I32), pltpu.VMEM((2, tm * nseg, LANES), t.dtype),
                            pltpu.VMEM((MOE_ROWS * nseg, LANES), t.dtype),
                            pltpu.SemaphoreType.DMA(()), pltpu.SemaphoreType.DMA((2,)),
                            pltpu.SemaphoreType.DMA(())],
        ),
        out_shape=jax.ShapeDtypeStruct((n_rows * nseg, LANES), t.dtype),
        compiler_params=pltpu.CompilerParams(dimension_semantics=("arbitrary",), has_side_effects=True,
                                             vmem_limit_bytes=VMEM_LIMIT),
        name="moe_dispatch",
    )(pend, padded, n_used, dest_tiles, t)


def _ffn_kernel(blk_e_ref, nused_ref, x_ref, w1_ref, b1g_ref, b1l_ref, w2_ref, b2_ref, y_ref,
                w1g_ref, w1l_ref, w2c_ref, *, nseg):
    i = pl.program_id(0)
    rows = x_ref.shape[0] // nseg
    e = blk_e_ref[i]
    e_prev = blk_e_ref[jnp.maximum(i - 1, 0)]

    @pl.when(jnp.logical_or(i == 0, e != e_prev))
    def _():
        w = 2 * LANES
        r = lax.broadcasted_iota(I32, (w, w), 0)
        c = lax.broadcasted_iota(I32, (w, w), 1)
        src_col = jnp.where(c < LANES, 2 * c, 2 * (c - LANES) + 1)
        perm = jnp.where(r == src_col, 1.0, 0.0).astype(BF16)
        for cc in range(w1_ref.shape[3] // w):
            chunk = w1_ref[0, 0, :, cc * w:(cc + 1) * w].astype(BF16)
            res = jnp.dot(chunk, perm, preferred_element_type=F32).astype(BF16)
            w1g_ref[:, cc * LANES:(cc + 1) * LANES] = res[:, :LANES]
            w1l_ref[:, cc * LANES:(cc + 1) * LANES] = res[:, LANES:]
        w2c_ref[...] = w2_ref[0, 0].astype(BF16)

    @pl.when(i < nused_ref[0])
    def _():
        x = jnp.concatenate([x_ref[pl.ds(c, rows, stride=nseg), :] for c in range(nseg)], axis=1).astype(BF16)
        hg = jnp.dot(x, w1g_ref[...], preferred_element_type=F32) + b1g_ref[0]
        hl = jnp.dot(x, w1l_ref[...], preferred_element_type=F32) + b1l_ref[0]
        xg = jnp.minimum(hg, SWIGLU_LIMIT)
        xl = jnp.clip(hl, -SWIGLU_LIMIT, SWIGLU_LIMIT)
        act = xg * _sigmoid(SWIGLU_ALPHA * xg) * (xl + 1.0)
        y = jnp.dot(act.astype(BF16), w2c_ref[...], preferred_element_type=F32) + b2_ref[0]
        for c in range(nseg):
            y_ref[pl.ds(c, rows, stride=nseg), :] = y[:, c * LANES:(c + 1) * LANES]

    @pl.when(i >= nused_ref[0])
    def _():
        y_ref[...] = jnp.zeros(y_ref.shape, F32)


def _ffn(x_buf, blk_e, n_used, layer, w1, b1g, b1l, w2, b2):
    _, ne, d, f2 = w1.shape
    f = f2 // 2
    nseg = d // LANES
    n_rows = x_buf.shape[0] // nseg
    n_blk = n_rows // MOE_ROWS
    assert f2 % (2 * LANES) == 0
    wspec = lambda a, b: pl.BlockSpec((1, a, b), lambda i, be, nu: (be[i], 0, 0))
    wfull = lambda a, b: pl.BlockSpec((1, 1, a, b), lambda i, be, nu: (layer, be[i], 0, 0))
    return pl.pallas_call(
        functools.partial(_ffn_kernel, nseg=nseg),
        grid_spec=pltpu.PrefetchScalarGridSpec(
            num_scalar_prefetch=2,
            grid=(n_blk,),
            in_specs=[pl.BlockSpec((MOE_ROWS * nseg, LANES), lambda i, be, nu: (i, 0)),
                      wfull(d, f2), wspec(1, f), wspec(1, f), wfull(f, d), wspec(1, d)],
            out_specs=pl.BlockSpec((MOE_ROWS * nseg, LANES), lambda i, be, nu: (i, 0)),
            scratch_shapes=[pltpu.VMEM((d, f), BF16), pltpu.VMEM((d, f), BF16), pltpu.VMEM((f, d), BF16)],
        ),
        out_shape=jax.ShapeDtypeStruct((n_rows * nseg, LANES), F32),
        compiler_params=_cparams("arbitrary"),
        name="moe_ffn",
    )(blk_e, n_used, x_buf, w1, b1g, b1l, w2, b2)


def _combine_kernel(dest_hbm, ybuf_hbm, gate_ref, h_ref, g_ref, b_ref, o_ref, idx_ref, rows_ref, isem, sem,
                    *, tm, nseg, alpha, chunks_per_row, pad):
    i = pl.program_id(0)
    n = TOP_K * tm

    def row_copy(slot, k, r):
        src = ybuf_hbm.at[pl.ds(pl.multiple_of(idx_ref[slot * n + k * tm + r] * nseg, nseg), nseg), :]
        dst = rows_ref.at[slot * TOP_K + k, pl.ds(pl.multiple_of(r * nseg, nseg), nseg), :]
        return pltpu.make_async_copy(src, dst, sem.at[slot])

    def gather_tile(tile, slot):
        cp = pltpu.make_async_copy(dest_hbm.at[pl.ds(pl.multiple_of(tile * n, n), n)],
                                   idx_ref.at[pl.ds(slot * n, n)], isem)
        cp.start()
        cp.wait()

        def issue(grp, carry):
            for u in range(DMA_UNROLL):
                for k in range(TOP_K):
                    row_copy(slot, k, grp * DMA_UNROLL + u).start(priority=k % DMA_QUEUES)
            return carry

        _for_real_groups(tile, tm, chunks_per_row, pad, issue)

    @pl.when(i == 0)
    def _():
        rows_ref[...] = jnp.zeros(rows_ref.shape, F32)
        gather_tile(0, 0)

    for nxt in range(2):
        @pl.when(jnp.logical_and(i + 1 < pl.num_programs(0), (i + 1) % 2 == nxt))
        def _():
            gather_tile(i + 1, nxt)

    slot = i % 2

    def drain(grp, carry):
        for u in range(DMA_UNROLL):
            for k in range(TOP_K):
                row_copy(slot, k, grp * DMA_UNROLL + u).wait()
        return carry

    _for_real_groups(i, tm, chunks_per_row, pad, drain)
    gates = gate_ref[...]
    y = alpha * h_ref[...]
    for k in range(TOP_K):
        rows = jnp.concatenate([rows_ref[slot * TOP_K + k, pl.ds(c, tm, stride=nseg), :] for c in range(nseg)],
                               axis=1)
        y = y + gates[:, k:k + 1] * rows
    o_ref[...] = _layer_norm(y, g_ref[...], b_ref[...])


def _combine(y_buf, dest_tiles, gates_t, h, g, b, alpha, tm, chunks_per_row, pad):
    nt, d = h.shape
    nseg = d // LANES
    return pl.pallas_call(
        functools.partial(_combine_kernel, tm=tm, nseg=nseg, alpha=alpha, chunks_per_row=chunks_per_row, pad=pad),
        grid=(nt // tm,),
        in_specs=[pl.BlockSpec(memory_space=pl.ANY),
                  pl.BlockSpec(memory_space=pl.ANY),
                  pl.BlockSpec((tm, TOP_K), lambda i: (i, 0)),
                  pl.BlockSpec((tm, d), lambda i: (i, 0)),
                  pl.BlockSpec((1, d), lambda i: (0, 0)),
                  pl.BlockSpec((1, d), lambda i: (0, 0))],
        out_specs=pl.BlockSpec((tm, d), lambda i: (i, 0)),
        out_shape=jax.ShapeDtypeStruct((nt, d), F32),
        scratch_shapes=[pltpu.SMEM((2 * TOP_K * tm,), I32), pltpu.VMEM((2 * TOP_K, tm * nseg, LANES), F32),
                        pltpu.SemaphoreType.DMA(()), pltpu.SemaphoreType.DMA((2,))],
        compiler_params=_cparams("arbitrary"),
        name="moe_combine",
    )(dest_tiles, y_buf, gates_t, h, g, b)


def _moe_layer(t, layer, lp, pad, w_r, b_r, w1, b1, w2, b2, ln_g, ln_b, alpha):
    nt, d = t.shape
    ne = w_r.shape[1]
    tm_r = _pick(nt, (512, 256, 128))
    tm_d = _pick(nt, (512, 256, 128))
    cpr = lp // SSM_CHUNK
    assert lp % SSM_CHUNK == 0 and pad % DMA_UNROLL == 0 and pad < SSM_CHUNK
    top_e, gates, rank, cnt = _router(t, w_r, b_r, tm_r, cpr, pad)

    counts = cnt[:, 0]
    padded = ((counts + MOE_ROWS - 1) // MOE_ROWS) * MOE_ROWS
    pend = jnp.cumsum(padded)
    poff = pend - padded
    expert_ids = jnp.arange(ne, dtype=I32)[:, None, None]
    dest = jnp.sum(jnp.where(top_e[None] == expert_ids, poff[:, None, None], 0), axis=0) + rank
    n_real = (nt - (nt // lp) * pad) * TOP_K
    n_rows = -(-n_real // MOE_ROWS) * MOE_ROWS + ne * MOE_ROWS
    n_blk = n_rows // MOE_ROWS
    blk_start = jnp.arange(n_blk, dtype=I32) * MOE_ROWS
    blk_e = jnp.minimum(jnp.sum(blk_start[:, None] >= pend[None, :], axis=1), ne - 1).astype(I32)
    n_used = (pend[-1:] // MOE_ROWS).astype(I32)
    dest_tiles = dest.astype(I32).reshape(TOP_K, nt // tm_d, tm_d).transpose(1, 0, 2).reshape(-1)

    x_buf = _dispatch(t, dest_tiles, pend.astype(I32), padded.astype(I32), n_used, n_rows, tm_d, cpr, pad)
    y_buf = _ffn(x_buf, blk_e, n_used, layer, w1, b1[:, None, 0::2], b1[:, None, 1::2], w2, b2[:, None, :])
    return _combine(y_buf, dest_tiles, gates.T, t, ln_g[None, :], ln_b[None, :], alpha, tm_d, cpr, pad)


def _conformer_layer(h, w1, b1, dw, dwb, ln_g, ln_b, w2, b2, mix_g, mix_b, alpha, pad):
    bsz, lp, d = h.shape
    c = dw.shape[1]
    tl = _pick(lp, (544, 272, 384, 192, 128))
    u = _glu(h, w1[:, :c].astype(BF16), w1[:, c:].astype(BF16), b1[None, :c], b1[None, c:], pad, tl)
    return _conv_mix(u, h, dw, dwb[None, :], ln_g[None, :], ln_b[None, :], w2.astype(BF16), b2[None, :],
                     mix_g[None, :], mix_b[None, :], alpha, tl)


def _mamba_layer(h, w_in, conv_w, conv_b, dt_bias, a_log, d_skip, norm_g, w_out, mix_g, mix_b, alpha, pad):
    bsz, lp, d = h.shape
    nt = bsz * lp
    d_inner = w_out.shape[0]
    nh = dt_bias.shape[0]
    conv_dim = conv_w.shape[1]
    nproj = d_inner + conv_dim
    hpg = nh // SSM_GROUPS
    tm = _pick(nt, (1024, 512, 256, 128))
    proj = _mm(h.reshape(nt, d), w_in[:, :nproj].astype(BF16), tm, _pick(nproj, (2048, 1024, 512, 256, 128)))
    a_neg = -jnp.exp(a_log.astype(F32))
    dtc, csc, csr = _dt_prep(h, w_in[:, nproj:], dt_bias, a_neg, pad)
    nc = lp // SSM_CHUNK
    csr = csr.reshape(bsz, nc, SSM_GROUPS, hpg, SSM_CHUNK).transpose(0, 2, 1, 3, 4)
    d_exp = jnp.repeat(d_skip.astype(F32), SSM_HEAD_DIM)[None, :]
    y = _ssd(proj.reshape(bsz, lp, nproj), dtc, csc, csr, conv_w, conv_b[None, :], d_exp, norm_g[None, :],
             pad, d_inner)
    out = _mm_resid_ln(y.reshape(nt, d_inner), w_out.astype(BF16), h.reshape(nt, d),
                       mix_g[None, :], mix_b[None, :], alpha, _pick(nt, (512, 256, 128)))
    return out.reshape(bsz, lp, d)


def kernel(x, meta_tokens, conv_w1, conv_b1, conv_dw, conv_dwb, conv_ln_g, conv_ln_b, conv_w2, conv_b2,
           ssm_w_in, ssm_conv_w, ssm_conv_b, ssm_dt_bias, ssm_a_log, ssm_d, ssm_norm_g, ssm_w_out,
           moe_w_router, moe_b_router, moe_w1, moe_b1, moe_w2, moe_b2,
           ln_mix_g, ln_mix_b, ln_ffn_g, ln_ffn_b):
    bsz, seq, d = x.shape
    depth = ln_mix_g.shape[0]
    alpha = (2 * depth) ** 0.25
    pad = SSM_CHUNK - N_META
    lp = pad + N_META + seq
    assert lp % SSM_CHUNK == 0
    meta = jnp.broadcast_to(meta_tokens.astype(x.dtype)[None], (bsz, N_META, d))
    h = jnp.concatenate([jnp.zeros((bsz, pad, d), x.dtype), meta, x], axis=1)
    for i in range(depth):
        j = i // 2
        if i % 2 == 0:
            h = _conformer_layer(h, conv_w1[j], conv_b1[j], conv_dw[j], conv_dwb[j], conv_ln_g[j],
                                 conv_ln_b[j], conv_w2[j], conv_b2[j], ln_mix_g[i], ln_mix_b[i], alpha, pad)
        else:
            h = _mamba_layer(h, ssm_w_in[j], ssm_conv_w[j], ssm_conv_b[j], ssm_dt_bias[j], ssm_a_log[j],
                             ssm_d[j], ssm_norm_g[j], ssm_w_out[j], ln_mix_g[i], ln_mix_b[i], alpha, pad)
        t = _moe_layer(h.reshape(bsz * lp, d), i, lp, pad, moe_w_router[i], moe_b_router[i], moe_w1, moe_b1[i],
                       moe_w2, moe_b2[i], ln_ffn_g[i], ln_ffn_b[i], alpha)
        h = t.reshape(bsz, lp, d)
    return h[:, pad + N_META:]
```

```python
import functools

import jax
import jax.numpy as jnp
from jax import lax
from jax.experimental import pallas as pl
from jax.experimental.pallas import tpu as pltpu

F32 = jnp.float32
BF16 = jnp.bfloat16
I32 = jnp.int32
HIGHEST = lax.Precision.HIGHEST

N_META = 16
SSM_HEAD_DIM = 64
SSM_GROUPS = 8
SSM_STATE = 128
SSM_CHUNK = 128
TOP_K = 4
SWIGLU_ALPHA = 1.702
SWIGLU_LIMIT = 7.0
LN_EPS = 1e-5
RMS_EPS = 1e-5

CONV_HALO = 32
CONV_ROWS = 64
CONV_LANES = 128
MOE_ROWS = 512
DMA_UNROLL = 8
DMA_QUEUES = 2
LANES = 128
VMEM_LIMIT = 56 * 1024 * 1024


def _cparams(*sem):
    return pltpu.CompilerParams(dimension_semantics=tuple(sem), vmem_limit_bytes=VMEM_LIMIT)


def _pick(n, prefs):
    for p in prefs:
        if n % p == 0:
            return p
    raise ValueError(f"no tile for {n} in {prefs}")


def _layer_norm(y, g, b):
    mu = jnp.mean(y, axis=-1, keepdims=True)
    yc = y - mu
    var = jnp.mean(yc * yc, axis=-1, keepdims=True)
    return yc * lax.rsqrt(var + LN_EPS) * g + b


def _dot_3pass(a, b, dims):
    a_hi = a.astype(BF16)
    b_hi = b.astype(BF16)
    a_lo = (a - a_hi.astype(F32)).astype(BF16)
    b_lo = (b - b_hi.astype(F32)).astype(BF16)
    dot = lambda u, v: lax.dot_general(u, v, (dims, ((), ())), preferred_element_type=F32)
    return dot(a_hi, b_hi) + (dot(a_hi, b_lo) + dot(a_lo, b_hi))


def _silu(x):
    h = 0.5 * x
    return h + h * jnp.tanh(h)


def _sigmoid(x):
    return 1.0 / (1.0 + jnp.exp(-x))


def _softplus(x):
    return jnp.maximum(x, 0.0) + jnp.log(1.0 + jnp.exp(-jnp.abs(x)))


def _glu_kernel(x_ref, wa_ref, wg_ref, ba_ref, bg_ref, o_ref, *, pad):
    j = pl.program_id(1)
    x = x_ref[0].astype(BF16)
    a = jnp.dot(x, wa_ref[...], preferred_element_type=F32) + ba_ref[...]
    g = jnp.dot(x, wg_ref[...], preferred_element_type=F32) + bg_ref[...]
    u = a * _sigmoid(g)
    row = lax.broadcasted_iota(I32, u.shape, 0)
    valid = jnp.logical_or(j > 0, row >= pad)
    o_ref[0] = jnp.where(valid, u, 0.0)


def _glu(h, wa, wg, ba, bg, pad, tl):
    bsz, lp, d = h.shape
    c = wa.shape[1]
    return pl.pallas_call(
        functools.partial(_glu_kernel, pad=pad),
        grid=(bsz, lp // tl),
        in_specs=[
            pl.BlockSpec((1, tl, d), lambda b, j: (b, j, 0)),
            pl.BlockSpec((d, c), lambda b, j: (0, 0)),
            pl.BlockSpec((d, c), lambda b, j: (0, 0)),
            pl.BlockSpec((1, c), lambda b, j: (0, 0)),
            pl.BlockSpec((1, c), lambda b, j: (0, 0)),
        ],
        out_specs=pl.BlockSpec((1, tl, c), lambda b, j: (b, j, 0)),
        out_shape=jax.ShapeDtypeStruct((bsz, lp, c), F32),
        compiler_params=_cparams("parallel", "parallel"),
        name="conf_glu",
    )(h, wa, wg, ba, bg)


def _conv_mix_kernel(ucur_ref, uprev_ref, h_ref, dw_ref, dwb_ref, g1_ref, b1_ref, w2_ref, b2_ref,
                     g2_ref, bb2_ref, o_ref, win_ref, act_ref, *, taps, alpha):
    j = pl.program_id(1)
    tl = ucur_ref.shape[1]

    @pl.when(j == 0)
    def _():
        win_ref[0:CONV_HALO, :] = jnp.zeros((CONV_HALO, win_ref.shape[1]), F32)

    @pl.when(j > 0)
    def _():
        win_ref[0:CONV_HALO, :] = uprev_ref[0, tl - CONV_HALO:tl, :]

    win_ref[CONV_HALO:CONV_HALO + tl, :] = ucur_ref[0]

    def chunk(c, carry):
        base = pl.multiple_of(c * CONV_ROWS, CONV_ROWS)
        n = CONV_HALO + CONV_ROWS
        parts = []
        for lc in range(win_ref.shape[1] // CONV_LANES):
            sl = slice(lc * CONV_LANES, (lc + 1) * CONV_LANES)
            win = win_ref[pl.ds(base, n), sl]
            acc = jnp.zeros((CONV_ROWS, CONV_LANES), F32) + dwb_ref[:, sl]
            for s in range(8):
                ws = win if s == 0 else pltpu.roll(win, n - s, 0)
                for k in range(taps):
                    off = CONV_HALO + k - (taps - 1)
                    if off % 8 == s:
                        acc = acc + ws[off - s:off - s + CONV_ROWS, :] * dw_ref[k:k + 1, sl]
            parts.append(acc)
        y = _layer_norm(jnp.concatenate(parts, axis=1), g1_ref[...], b1_ref[...])
        y = _silu(y)
        act_ref[pl.ds(base, CONV_ROWS), :] = y.astype(BF16)
        return carry

    lax.fori_loop(0, tl // CONV_ROWS, chunk, 0, unroll=4)
    mix = jnp.dot(act_ref[...], w2_ref[...], preferred_element_type=F32) + b2_ref[...]
    y = alpha * h_ref[0] + mix
    o_ref[0] = _layer_norm(y, g2_ref[...], bb2_ref[...])


def _conv_mix(u, h, dw, dwb, g1, b1, w2, b2, g2, bb2, alpha, tl):
    bsz, lp, c = u.shape
    d = h.shape[2]
    taps = dw.shape[0]
    assert taps - 1 <= CONV_HALO and tl % CONV_ROWS == 0
    vec = lambda n: pl.BlockSpec((1, n), lambda b, j: (0, 0))
    return pl.pallas_call(
        functools.partial(_conv_mix_kernel, taps=taps, alpha=alpha),
        grid=(bsz, lp // tl),
        in_specs=[
            pl.BlockSpec((1, tl, c), lambda b, j: (b, j, 0)),
            pl.BlockSpec((1, tl, c), lambda b, j: (b, jnp.maximum(j - 1, 0), 0)),
            pl.BlockSpec((1, tl, d), lambda b, j: (b, j, 0)),
            pl.BlockSpec((taps, c), lambda b, j: (0, 0)),
            vec(c), vec(c), vec(c),
            pl.BlockSpec((c, d), lambda b, j: (0, 0)),
            vec(d), vec(d), vec(d),
        ],
        out_specs=pl.BlockSpec((1, tl, d), lambda b, j: (b, j, 0)),
        out_shape=jax.ShapeDtypeStruct((bsz, lp, d), F32),
        scratch_shapes=[pltpu.VMEM((CONV_HALO + tl, c), F32), pltpu.VMEM((tl, c), BF16)],
        compiler_params=_cparams("parallel", "parallel"),
        name="conf_conv_mix",
    )(u, u, h, dw, dwb, g1, b1, w2, b2, g2, bb2)


def _mm_kernel(x_ref, w_ref, o_ref):
    o_ref[...] = jnp.dot(x_ref[...].astype(BF16), w_ref[...], preferred_element_type=F32).astype(o_ref.dtype)


def _mm(x, w, tm, tn, out_dtype=F32):
    m, k = x.shape
    n = w.shape[1]
    return pl.pallas_call(
        _mm_kernel,
        grid=(n // tn, m // tm),
        in_specs=[pl.BlockSpec((tm, k), lambda jn, im: (im, 0)),
                  pl.BlockSpec((k, tn), lambda jn, im: (0, jn))],
        out_specs=pl.BlockSpec((tm, tn), lambda jn, im: (im, jn)),
        out_shape=jax.ShapeDtypeStruct((m, n), out_dtype),
        compiler_params=_cparams("parallel", "parallel"),
        name="mm",
    )(x, w)


def _mm_resid_ln_kernel(x_ref, w_ref, h_ref, g_ref, b_ref, o_ref, *, alpha):
    mix = jnp.dot(x_ref[...].astype(BF16), w_ref[...], preferred_element_type=F32)
    o_ref[...] = _layer_norm(alpha * h_ref[...] + mix, g_ref[...], b_ref[...])


def _mm_resid_ln(x, w, h, g, b, alpha, tm):
    m, k = x.shape
    d = w.shape[1]
    return pl.pallas_call(
        functools.partial(_mm_resid_ln_kernel, alpha=alpha),
        grid=(m // tm,),
        in_specs=[pl.BlockSpec((tm, k), lambda i: (i, 0)),
                  pl.BlockSpec((k, d), lambda i: (0, 0)),
                  pl.BlockSpec((tm, d), lambda i: (i, 0)),
                  pl.BlockSpec((1, d), lambda i: (0, 0)),
                  pl.BlockSpec((1, d), lambda i: (0, 0))],
        out_specs=pl.BlockSpec((tm, d), lambda i: (i, 0)),
        out_shape=jax.ShapeDtypeStruct((m, d), F32),
        compiler_params=_cparams("parallel"),
        name="mm_resid_ln",
    )(x, w, h, g, b)


def _dt_kernel(x_ref, wc_ref, bc_ref, ac_ref, dtc_ref, csc_ref, csr_ref, *, pad):
    c = pl.program_id(1)
    x = x_ref[0]
    n = x.shape[0]
    nh = csr_ref.shape[2]
    raw_c = _dot_3pass(x, wc_ref[...], ((1,), (0,)))
    row_c = lax.broadcasted_iota(I32, raw_c.shape, 0)
    dt_c = jnp.where(jnp.logical_and(c == 0, row_c < pad), 0.0, _softplus(raw_c + bc_ref[...]))
    li = lax.broadcasted_iota(I32, (n, n), 0)
    si = lax.broadcasted_iota(I32, (n, n), 1)
    tri = jnp.where(si <= li, 1.0, 0.0)
    cs_c = jnp.dot(tri, dt_c * ac_ref[...], preferred_element_type=F32, precision=HIGHEST)
    dtc_ref[0] = dt_c
    csc_ref[0] = cs_c
    csr_ref[0, 0] = cs_c.T[:nh, :]


def _dt_prep(h, w_dt, dt_bias, a_neg, pad):
    bsz, lp, d = h.shape
    nh = w_dt.shape[1]
    nc = lp // SSM_CHUNK
    lanes = 128
    wc = jnp.zeros((d, lanes), F32).at[:, :nh].set(w_dt)
    bc = jnp.zeros((1, lanes), F32).at[0, :nh].set(dt_bias)
    ac = jnp.zeros((1, lanes), F32).at[0, :nh].set(a_neg)
    return pl.pallas_call(
        functools.partial(_dt_kernel, pad=pad),
        grid=(bsz, nc),
        in_specs=[
            pl.BlockSpec((1, SSM_CHUNK, d), lambda b, c: (b, c, 0)),
            pl.BlockSpec((d, lanes), lambda b, c: (0, 0)),
            pl.BlockSpec((1, lanes), lambda b, c: (0, 0)),
            pl.BlockSpec((1, lanes), lambda b, c: (0, 0)),
        ],
        out_specs=[
            pl.BlockSpec((1, SSM_CHUNK, lanes), lambda b, c: (b, c, 0)),
            pl.BlockSpec((1, SSM_CHUNK, lanes), lambda b, c: (b, c, 0)),
            pl.BlockSpec((1, 1, nh, SSM_CHUNK), lambda b, c: (b, c, 0, 0)),
        ],
        out_shape=[
            jax.ShapeDtypeStruct((bsz, lp, lanes), F32),
            jax.ShapeDtypeStruct((bsz, lp, lanes), F32),
            jax.ShapeDtypeStruct((bsz, nc, nh, SSM_CHUNK), F32),
        ],
        compiler_params=_cparams("parallel", "parallel"),
        name="ssm_dt",
    )(h, wc, bc, ac)


def _ssd_kernel(xs_ref, bm_ref, cm_ref, z_ref, dtc_ref, csc_ref, csr_ref,
                cwx_ref, cwb_ref, cwc_ref, cbx_ref, cbb_ref, cbc_ref, dsk_ref, ng_ref,
                o_ref, state_ref, carx_ref, carb_ref, carc_ref, *, pad, hpg, taps):
    g = pl.program_id(1)
    q = SSM_CHUNK
    hd = SSM_HEAD_DIM
    gw = hpg * hd
    n_chunks = xs_ref.shape[1] // q

    state_ref[...] = jnp.zeros(state_ref.shape, F32)
    carx_ref[...] = jnp.zeros(carx_ref.shape, F32)
    carb_ref[...] = jnp.zeros(carb_ref.shape, F32)
    carc_ref[...] = jnp.zeros(carc_ref.shape, F32)

    hrow = lax.broadcasted_iota(I32, (128, gw), 0)
    hcol = lax.broadcasted_iota(I32, (128, gw), 1) // hd
    sel = jnp.where(hrow == g * hpg + hcol, 1.0, 0.0).astype(BF16)

    def select_heads(v):
        hi = v.astype(BF16)
        r1 = v - hi.astype(F32)
        mid = r1.astype(BF16)
        lo = (r1 - mid.astype(F32)).astype(BF16)
        out = jnp.dot(hi, sel, preferred_element_type=F32)
        out = out + jnp.dot(mid, sel, preferred_element_type=F32)
        return out + jnp.dot(lo, sel, preferred_element_type=F32)

    lane_head = lax.broadcasted_iota(I32, (q, gw), 1) // hd
    head_mask = [jnp.where(lane_head == r, 1.0, 0.0).astype(BF16) for r in range(hpg)]
    li = lax.broadcasted_iota(I32, (q, q), 0)
    si = lax.broadcasted_iota(I32, (q, q), 1)
    causal = si <= li
    row1 = lax.broadcasted_iota(I32, (q, 1), 0)

    def conv_act(raw, car_ref, w_ref, b_ref, valid):
        if valid is not None:
            raw = jnp.where(valid, raw, 0.0)
        win = jnp.concatenate([car_ref[...], raw], axis=0)
        acc = win * w_ref[0:1, :]
        for k in range(1, taps):
            acc = pltpu.roll(acc, 1, 0) + win * w_ref[k:k + 1, :]
        acc = acc[8:8 + q, :] + b_ref[...]
        car_ref[...] = raw[q - 8:q, :]
        act = _silu(acc)
        return act if valid is None else jnp.where(valid, act, 0.0)

    def chunk(c, first):
        r0 = 0 if first else pl.multiple_of(c * q, q)
        valid = (row1 >= pad) if first else None
        xs = conv_act(xs_ref[0, pl.ds(r0, q), :], carx_ref, cwx_ref, cbx_ref, valid)
        bm = conv_act(bm_ref[0, pl.ds(r0, q), :], carb_ref, cwb_ref, cbb_ref, valid)
        cm = conv_act(cm_ref[0, pl.ds(r0, q), :], carc_ref, cwc_ref, cbc_ref, valid)

        dt_e = select_heads(dtc_ref[0, pl.ds(r0, q), :])
        cs_e = select_heads(csc_ref[0, pl.ds(r0, q), :])
        cs_r = csr_ref[0, 0, c]
        a_last = cs_e[q - 1:q, :]

        x_dt = xs * dt_e
        xw = (x_dt * jnp.exp(a_last - cs_e)).astype(BF16)
        bm16 = bm.astype(BF16)
        cm16 = cm.astype(BF16)
        cb = lax.dot_general(cm16, bm16, (((1,), (1,)), ((), ())), preferred_element_type=F32)

        m_parts = []
        x_parts = []
        x_dt16 = x_dt.astype(BF16)
        for r in range(hpg):
            col = cs_e[:, r * hd:r * hd + 1]
            seg = col - cs_r[r:r + 1, :]
            decay = jnp.exp(jnp.where(causal, seg, -jnp.inf))
            m_parts.append((cb * decay).astype(BF16))
            x_parts.append(x_dt16 * head_mask[r])
        m_cat = jnp.concatenate(m_parts, axis=1)
        x_bd = jnp.concatenate(x_parts, axis=0)
        y = jnp.dot(m_cat, x_bd, preferred_element_type=F32)

        st = state_ref[...]
        y = y + jnp.dot(cm16, st.astype(BF16), preferred_element_type=F32) * jnp.exp(cs_e)
        new = jnp.dot(bm.T.astype(BF16), xw, preferred_element_type=F32)
        state_ref[...] = st * jnp.exp(a_last) + new
        y = y + dsk_ref[...] * xs

        z = z_ref[0, pl.ds(r0, q), :]
        y = y * _silu(z)
        y = y * lax.rsqrt(jnp.mean(y * y, axis=-1, keepdims=True) + RMS_EPS) * ng_ref[...]
        o_ref[0, pl.ds(r0, q), :] = y.astype(o_ref.dtype)

    def later_chunk(c, carry):
        chunk(c, False)
        return carry

    chunk(0, True)
    lax.fori_loop(1, n_chunks, later_chunk, 0, unroll=4)


def _ssd(proj, dtc, csc, csr, conv_w, conv_b, d_exp, norm_g, pad, d_inner):
    bsz, lp, _ = proj.shape
    g = SSM_GROUPS
    n = SSM_STATE
    gw = d_inner // g
    hpg = gw // SSM_HEAD_DIM
    assert gw % 128 == 0 and n == 128 and dtc.shape[2] == 128
    taps = conv_w.shape[0]
    nc = lp // SSM_CHUNK
    zb = 0
    xb = d_inner // gw
    bb = 2 * d_inner // n
    cb_ = (2 * d_inner + g * n) // n
    cxb = 0
    cbb = d_inner // n
    ccb = (d_inner + g * n) // n
    seq = lambda w, off: pl.BlockSpec((1, lp, w), lambda b, j, off=off: (b, 0, off + j))
    cw = lambda w, off: pl.BlockSpec((taps, w), lambda b, j, off=off: (0, off + j))
    cbv = lambda w, off: pl.BlockSpec((1, w), lambda b, j, off=off: (0, off + j))
    return pl.pallas_call(
        functools.partial(_ssd_kernel, pad=pad, hpg=hpg, taps=taps),
        grid=(bsz, g),
        in_specs=[
            seq(gw, xb), seq(n, bb), seq(n, cb_), seq(gw, zb),
            pl.BlockSpec((1, lp, 128), lambda b, j: (b, 0, 0)),
            pl.BlockSpec((1, lp, 128), lambda b, j: (b, 0, 0)),
            pl.BlockSpec((1, 1, nc, hpg, SSM_CHUNK), lambda b, j: (b, j, 0, 0, 0)),
            cw(gw, cxb), cw(n, cbb), cw(n, ccb),
            cbv(gw, cxb), cbv(n, cbb), cbv(n, ccb),
            pl.BlockSpec((1, gw), lambda b, j: (0, j)),
            pl.BlockSpec((1, gw), lambda b, j: (0, j)),
        ],
        out_specs=pl.BlockSpec((1, lp, gw), lambda b, j: (b, 0, j)),
        out_shape=jax.ShapeDtypeStruct((bsz, lp, d_inner), BF16),
        scratch_shapes=[pltpu.VMEM((n, gw), F32), pltpu.VMEM((8, gw), F32),
                        pltpu.VMEM((8, n), F32), pltpu.VMEM((8, n), F32)],
        compiler_params=_cparams("parallel", "parallel"),
        name="ssm_ssd",
    )(proj, proj, proj, proj, dtc, csc, csr, conv_w, conv_w, conv_w, conv_b, conv_b, conv_b, d_exp, norm_g)


def _pad_chunk(tile, q, tm, chunks_per_row):
    return lax.rem(tile * (tm // SSM_CHUNK) + q, chunks_per_row) == 0


def _for_real_groups(tile, tm, chunks_per_row, pad, body):
    per_chunk = SSM_CHUNK // DMA_UNROLL
    for q in range(tm // SSM_CHUNK):
        skip = jnp.where(_pad_chunk(tile, q, tm, chunks_per_row), pad // DMA_UNROLL, 0)
        lax.fori_loop(q * per_chunk + skip, (q + 1) * per_chunk, body, 0)


def _router_kernel(x_ref, wr_ref, br_ref, e_ref, gate_ref, rank_ref, cnt_ref, carry_ref, *, chunks_per_row, pad):
    i = pl.program_id(0)

    @pl.when(i == 0)
    def _():
        carry_ref[...] = jnp.zeros(carry_ref.shape, F32)

    x = x_ref[...]
    tm = x.shape[0]
    ne = wr_ref.shape[0]
    logits = _dot_3pass(wr_ref[...], x, ((1,), (1,))) + br_ref[...]
    ids = lax.broadcasted_iota(I32, (ne, tm), 0)
    vals = logits
    top_v, top_e, sels = [], [], []
    for _ in range(TOP_K):
        m = jnp.max(vals, axis=0, keepdims=True)
        idx = jnp.min(jnp.where(vals == m, ids, ne), axis=0, keepdims=True)
        sel = ids == idx
        top_v.append(m)
        top_e.append(idx)
        sels.append(sel)
        vals = jnp.where(sel, -jnp.inf, vals)
    ex = [jnp.exp(v - top_v[0]) for v in top_v]
    den = ex[0]
    for t in ex[1:]:
        den = den + t
    e_ref[...] = jnp.concatenate(top_e, axis=0)
    gate_ref[...] = jnp.concatenate([t / den for t in ex], axis=0)

    onehot = sels[0].astype(F32)
    for s in sels[1:]:
        onehot = onehot + s.astype(F32)
    lane = lax.broadcasted_iota(I32, (1, tm), 1)
    real = jnp.ones((1, tm), F32)
    for q in range(tm // SSM_CHUNK):
        inert = jnp.logical_and(_pad_chunk(i, q, tm, chunks_per_row),
                                jnp.logical_and(lane >= q * SSM_CHUNK, lane < q * SSM_CHUNK + pad))
        real = jnp.where(inert, 0.0, real)
    onehot = onehot * real
    ji = lax.broadcasted_iota(I32, (tm, tm), 0)
    ti = lax.broadcasted_iota(I32, (tm, tm), 1)
    before = (ji < ti).astype(BF16)
    prefix = jnp.dot(onehot.astype(BF16), before, preferred_element_type=F32)
    prefix = prefix + carry_ref[:, 0:1]
    ranks = [jnp.sum(jnp.where(s, prefix, 0.0), axis=0, keepdims=True) for s in sels]
    rank_ref[...] = jnp.concatenate(ranks, axis=0).astype(I32)
    carry_ref[...] = carry_ref[...] + jnp.sum(onehot, axis=1, keepdims=True)
    cnt_ref[...] = carry_ref[...].astype(I32)


def _router(t, w_r, b_r, tm, chunks_per_row, pad):
    nt, d = t.shape
    ne = w_r.shape[1]
    assert tm % SSM_CHUNK == 0
    return pl.pallas_call(
        functools.partial(_router_kernel, chunks_per_row=chunks_per_row, pad=pad),
        grid=(nt // tm,),
        in_specs=[pl.BlockSpec((tm, d), lambda i: (i, 0)),
                  pl.BlockSpec((ne, d), lambda i: (0, 0)),
                  pl.BlockSpec((ne, 1), lambda i: (0, 0))],
        out_specs=[pl.BlockSpec((TOP_K, tm), lambda i: (0, i)),
                   pl.BlockSpec((TOP_K, tm), lambda i: (0, i)),
                   pl.BlockSpec((TOP_K, tm), lambda i: (0, i)),
                   pl.BlockSpec((ne, 128), lambda i: (0, 0))],
        out_shape=[jax.ShapeDtypeStruct((TOP_K, nt), I32),
                   jax.ShapeDtypeStruct((TOP_K, nt), F32),
                   jax.ShapeDtypeStruct((TOP_K, nt), I32),
                   jax.ShapeDtypeStruct((ne, 128), I32)],
        scratch_shapes=[pltpu.VMEM((ne, 128), F32)],
        compiler_params=_cparams("arbitrary"),
        name="moe_router",
    )(t, w_r.T, b_r[:, None])


def _dispatch_kernel(pend_ref, padded_ref, nused_ref, dest_hbm, t_ref, xbuf_hbm,
                     idx_ref, rows_ref, zero_ref, isem, sem, zsem, *, tm, nseg, n_blk, chunks_per_row, pad):
    i = pl.program_id(0)
    n = TOP_K * tm
    blk = MOE_ROWS * nseg

    @pl.when(i == 0)
    def _():
        zero_ref[...] = jnp.zeros(zero_ref.shape, zero_ref.dtype)

        def group_fill(e):
            return pltpu.make_async_copy(
                zero_ref, xbuf_hbm.at[pl.ds(pl.multiple_of((pend_ref[e] - MOE_ROWS) * nseg, blk), blk), :], zsem)

        def tail_fill(j):
            return pltpu.make_async_copy(zero_ref, xbuf_hbm.at[pl.ds(pl.multiple_of(j * blk, blk), blk), :], zsem)

        def start_group(e, carry):
            @pl.when(padded_ref[e] > 0)
            def _():
                group_fill(e).start()
            return carry

        def wait_group(e, carry):
            @pl.when(padded_ref[e] > 0)
            def _():
                group_fill(e).wait()
            return carry

        def start_tail(j, carry):
            tail_fill(j).start()
            return carry

        def wait_tail(j, carry):
            tail_fill(j).wait()
            return carry

        n_exp = pend_ref.shape[0]
        lax.fori_loop(0, n_exp, start_group, 0)
        lax.fori_loop(nused_ref[0], n_blk, start_tail, 0)
        lax.fori_loop(0, n_exp, wait_group, 0)
        lax.fori_loop(nused_ref[0], n_blk, wait_tail, 0)

    last = pl.num_programs(0) - 1

    def row_copy(slot, k, r):
        src = rows_ref.at[slot, pl.ds(pl.multiple_of(r * nseg, nseg), nseg), :]
        dst = xbuf_hbm.at[pl.ds(pl.multiple_of(idx_ref[slot * n + k * tm + r] * nseg, nseg), nseg), :]
        return pltpu.make_async_copy(src, dst, sem.at[slot])

    def drain_slot(slot, tile):
        def drain(grp, carry):
            for u in range(DMA_UNROLL):
                for k in range(TOP_K):
                    row_copy(slot, k, grp * DMA_UNROLL + u).wait()
            return carry

        _for_real_groups(tile, tm, chunks_per_row, pad, drain)

    for slot in range(2):
        @pl.when(i % 2 == slot)
        def _():
            @pl.when(i >= 2)
            def _():
                drain_slot(slot, i - 2)

            cp = pltpu.make_async_copy(dest_hbm.at[pl.ds(pl.multiple_of(i * n, n), n)],
                                       idx_ref.at[pl.ds(slot * n, n)], isem)
            cp.start()
            x = t_ref[...]
            for c in range(nseg):
                rows_ref[slot, pl.ds(c, tm, stride=nseg), :] = x[:, c * LANES:(c + 1) * LANES]
            cp.wait()

            def issue(grp, carry):
                for u in range(DMA_UNROLL):
                    for k in range(TOP_K):
                        row_copy(slot, k, grp * DMA_UNROLL + u).start(priority=k % DMA_QUEUES)
                return carry

            _for_real_groups(i, tm, chunks_per_row, pad, issue)

            @pl.when(i == last)
            def _():
                drain_slot(slot, i)

                @pl.when(i >= 1)
                def _():
                    drain_slot(1 - slot, i - 1)


def _dispatch(t, dest_tiles, pend, padded, n_used, n_rows, tm, chunks_per_row, pad):
    nt, d = t.shape
    nseg = d // LANES
    return pl.pallas_call(
        functools.partial(_dispatch_kernel, tm=tm, nseg=nseg, n_blk=n_rows // MOE_ROWS,
                          chunks_per_row=chunks_per_row, pad=pad),
        grid_spec=pltpu.PrefetchScalarGridSpec(
            num_scalar_prefetch=3,
            grid=(nt // tm,),
            in_specs=[pl.BlockSpec(memory_space=pl.ANY),
                      pl.BlockSpec((tm, d), lambda i, *_: (i, 0))],
            out_specs=pl.BlockSpec(memory_space=pl.ANY),
            scratch_shapes=[pltpu.SMEM((2 * TOP_K * tm,), I32), pltpu.VMEM((2, tm * nseg, LANES), t.dtype),
                            pltpu.VMEM((MOE_ROWS * nseg, LANES), t.dtype),
                            pltpu.SemaphoreType.DMA(()), pltpu.SemaphoreType.DMA((2,)),
                            pltpu.SemaphoreType.DMA(())],
        ),
        out_shape=jax.ShapeDtypeStruct((n_rows * nseg, LANES), t.dtype),
        compiler_params=pltpu.CompilerParams(dimension_semantics=("arbitrary",), has_side_effects=True,
                                             vmem_limit_bytes=VMEM_LIMIT),
        name="moe_dispatch",
    )(pend, padded, n_used, dest_tiles, t)


def _ffn_kernel(blk_e_ref, nused_ref, x_ref, w1_ref, b1g_ref, b1l_ref, w2_ref, b2_ref, y_ref,
                w1g_ref, w1l_ref, w2c_ref, *, nseg):
    i = pl.program_id(0)
    rows = x_ref.shape[0] // nseg
    e = blk_e_ref[i]
    e_prev = blk_e_ref[jnp.maximum(i - 1, 0)]

    @pl.when(jnp.logical_or(i == 0, e != e_prev))
    def _():
        w = 2 * LANES
        r = lax.broadcasted_iota(I32, (w, w), 0)
        c = lax.broadcasted_iota(I32, (w, w), 1)
        src_col = jnp.where(c < LANES, 2 * c, 2 * (c - LANES) + 1)
        perm = jnp.where(r == src_col, 1.0, 0.0).astype(BF16)
        for cc in range(w1_ref.shape[3] // w):
            chunk = w1_ref[0, 0, :, cc * w:(cc + 1) * w].astype(BF16)
            res = jnp.dot(chunk, perm, preferred_element_type=F32).astype(BF16)
            w1g_ref[:, cc * LANES:(cc + 1) * LANES] = res[:, :LANES]
            w1l_ref[:, cc * LANES:(cc + 1) * LANES] = res[:, LANES:]
        w2c_ref[...] = w2_ref[0, 0].astype(BF16)

    @pl.when(i < nused_ref[0])
    def _():
        x = jnp.concatenate([x_ref[pl.ds(c, rows, stride=nseg), :] for c in range(nseg)], axis=1).astype(BF16)
        hg = jnp.dot(x, w1g_ref[...], preferred_element_type=F32) + b1g_ref[0]
        hl = jnp.dot(x, w1l_ref[...], preferred_element_type=F32) + b1l_ref[0]
        xg = jnp.minimum(hg, SWIGLU_LIMIT)
        xl = jnp.clip(hl, -SWIGLU_LIMIT, SWIGLU_LIMIT)
        act = xg * _sigmoid(SWIGLU_ALPHA * xg) * (xl + 1.0)
        y = jnp.dot(act.astype(BF16), w2c_ref[...], preferred_element_type=F32) + b2_ref[0]
        for c in range(nseg):
            y_ref[pl.ds(c, rows, stride=nseg), :] = y[:, c * LANES:(c + 1) * LANES]

    @pl.when(i >= nused_ref[0])
    def _():
        y_ref[...] = jnp.zeros(y_ref.shape, F32)


def _ffn(x_buf, blk_e, n_used, layer, w1, b1g, b1l, w2, b2):
    _, ne, d, f2 = w1.shape
    f = f2 // 2
    nseg = d // LANES
    n_rows = x_buf.shape[0] // nseg
    n_blk = n_rows // MOE_ROWS
    assert f2 % (2 * LANES) == 0
    wspec = lambda a, b: pl.BlockSpec((1, a, b), lambda i, be, nu: (be[i], 0, 0))
    wfull = lambda a, b: pl.BlockSpec((1, 1, a, b), lambda i, be, nu: (layer, be[i], 0, 0))
    return pl.pallas_call(
        functools.partial(_ffn_kernel, nseg=nseg),
        grid_spec=pltpu.PrefetchScalarGridSpec(
            num_scalar_prefetch=2,
            grid=(n_blk,),
            in_specs=[pl.BlockSpec((MOE_ROWS * nseg, LANES), lambda i, be, nu: (i, 0)),
                      wfull(d, f2), wspec(1, f), wspec(1, f), wfull(f, d), wspec(1, d)],
            out_specs=pl.BlockSpec((MOE_ROWS * nseg, LANES), lambda i, be, nu: (i, 0)),
            scratch_shapes=[pltpu.VMEM((d, f), BF16), pltpu.VMEM((d, f), BF16), pltpu.VMEM((f, d), BF16)],
        ),
        out_shape=jax.ShapeDtypeStruct((n_rows * nseg, LANES), F32),
        compiler_params=_cparams("arbitrary"),
        name="moe_ffn",
    )(blk_e, n_used, x_buf, w1, b1g, b1l, w2, b2)


def _combine_kernel(dest_hbm, ybuf_hbm, gate_ref, h_ref, g_ref, b_ref, o_ref, idx_ref, rows_ref, isem, sem,
                    *, tm, nseg, alpha, chunks_per_row, pad):
    i = pl.program_id(0)
    n = TOP_K * tm

    def row_copy(slot, k, r):
        src = ybuf_hbm.at[pl.ds(pl.multiple_of(idx_ref[slot * n + k * tm + r] * nseg, nseg), nseg), :]
        dst = rows_ref.at[slot * TOP_K + k, pl.ds(pl.multiple_of(r * nseg, nseg), nseg), :]
        return pltpu.make_async_copy(src, dst, sem.at[slot])

    def gather_tile(tile, slot):
        cp = pltpu.make_async_copy(dest_hbm.at[pl.ds(pl.multiple_of(tile * n, n), n)],
                                   idx_ref.at[pl.ds(slot * n, n)], isem)
        cp.start()
        cp.wait()

        def issue(grp, carry):
            for u in range(DMA_UNROLL):
                for k in range(TOP_K):
                    row_copy(slot, k, grp * DMA_UNROLL + u).start(priority=k % DMA_QUEUES)
            return carry

        _for_real_groups(tile, tm, chunks_per_row, pad, issue)

    @pl.when(i == 0)
    def _():
        rows_ref[...] = jnp.zeros(rows_ref.shape, F32)
        gather_tile(0, 0)

    for nxt in range(2):
        @pl.when(jnp.logical_and(i + 1 < pl.num_programs(0), (i + 1) % 2 == nxt))
        def _():
            gather_tile(i + 1, nxt)

    slot = i % 2

    def drain(grp, carry):
        for u in range(DMA_UNROLL):
            for k in range(TOP_K):
                row_copy(slot, k, grp * DMA_UNROLL + u).wait()
        return carry

    _for_real_groups(i, tm, chunks_per_row, pad, drain)
    gates = gate_ref[...]
    y = alpha * h_ref[...]
    for k in range(TOP_K):
        rows = jnp.concatenate([rows_ref[slot * TOP_K + k, pl.ds(c, tm, stride=nseg), :] for c in range(nseg)],
                               axis=1)
        y = y + gates[:, k:k + 1] * rows
    o_ref[...] = _layer_norm(y, g_ref[...], b_ref[...])


def _combine(y_buf, dest_tiles, gates_t, h, g, b, alpha, tm, chunks_per_row, pad):
    nt, d = h.shape
    nseg = d // LANES
    return pl.pallas_call(
        functools.partial(_combine_kernel, tm=tm, nseg=nseg, alpha=alpha, chunks_per_row=chunks_per_row, pad=pad),
        grid=(nt // tm,),
        in_specs=[pl.BlockSpec(memory_space=pl.ANY),
                  pl.BlockSpec(memory_space=pl.ANY),
                  pl.BlockSpec((tm, TOP_K), lambda i: (i, 0)),
                  pl.BlockSpec((tm, d), lambda i: (i, 0)),
                  pl.BlockSpec((1, d), lambda i: (0, 0)),
                  pl.BlockSpec((1, d), lambda i: (0, 0))],
        out_specs=pl.BlockSpec((tm, d), lambda i: (i, 0)),
        out_shape=jax.ShapeDtypeStruct((nt, d), F32),
        scratch_shapes=[pltpu.SMEM((2 * TOP_K * tm,), I32), pltpu.VMEM((2 * TOP_K, tm * nseg, LANES), F32),
                        pltpu.SemaphoreType.DMA(()), pltpu.SemaphoreType.DMA((2,))],
        compiler_params=_cparams("arbitrary"),
        name="moe_combine",
    )(dest_tiles, y_buf, gates_t, h, g, b)


def _moe_layer(t, layer, lp, pad, w_r, b_r, w1, b1, w2, b2, ln_g, ln_b, alpha):
    nt, d = t.shape
    ne = w_r.shape[1]
    tm_r = _pick(nt, (512, 256, 128))
    tm_d = _pick(nt, (512, 256, 128))
    cpr = lp // SSM_CHUNK
    assert lp % SSM_CHUNK == 0 and pad % DMA_UNROLL == 0 and pad < SSM_CHUNK
    top_e, gates, rank, cnt = _router(t, w_r, b_r, tm_r, cpr, pad)

    counts = cnt[:, 0]
    padded = ((counts + MOE_ROWS - 1) // MOE_ROWS) * MOE_ROWS
    pend = jnp.cumsum(padded)
    poff = pend - padded
    expert_ids = jnp.arange(ne, dtype=I32)[:, None, None]
    dest = jnp.sum(jnp.where(top_e[None] == expert_ids, poff[:, None, None], 0), axis=0) + rank
    n_real = (nt - (nt // lp) * pad) * TOP_K
    n_rows = -(-n_real // MOE_ROWS) * MOE_ROWS + ne * MOE_ROWS
    n_blk = n_rows // MOE_ROWS
    blk_start = jnp.arange(n_blk, dtype=I32) * MOE_ROWS
    blk_e = jnp.minimum(jnp.sum(blk_start[:, None] >= pend[None, :], axis=1), ne - 1).astype(I32)
    n_used = (pend[-1:] // MOE_ROWS).astype(I32)
    dest_tiles = dest.astype(I32).reshape(TOP_K, nt // tm_d, tm_d).transpose(1, 0, 2).reshape(-1)

    x_buf = _dispatch(t, dest_tiles, pend.astype(I32), padded.astype(I32), n_used, n_rows, tm_d, cpr, pad)
    y_buf = _ffn(x_buf, blk_e, n_used, layer, w1, b1[:, None, 0::2], b1[:, None, 1::2], w2, b2[:, None, :])
    return _combine(y_buf, dest_tiles, gates.T, t, ln_g[None, :], ln_b[None, :], alpha, tm_d, cpr, pad)


def _conformer_layer(h, w1, b1, dw, dwb, ln_g, ln_b, w2, b2, mix_g, mix_b, alpha, pad):
    bsz, lp, d = h.shape
    c = dw.shape[1]
    tl = _pick(lp, (544, 272, 384, 192, 128))
    u = _glu(h, w1[:, :c].astype(BF16), w1[:, c:].astype(BF16), b1[None, :c], b1[None, c:], pad, tl)
    tl_conv = _pick(lp, (1088, 576, 384, 192, 128))
    return _conv_mix(u, h, dw, dwb[None, :], ln_g[None, :], ln_b[None, :], w2.astype(BF16), b2[None, :],
                     mix_g[None, :], mix_b[None, :], alpha, tl_conv)


def _mamba_layer(h, w_in, conv_w, conv_b, dt_bias, a_log, d_skip, norm_g, w_out, mix_g, mix_b, alpha, pad):
    bsz, lp, d = h.shape
    nt = bsz * lp
    d_inner = w_out.shape[0]
    nh = dt_bias.shape[0]
    conv_dim = conv_w.shape[1]
    nproj = d_inner + conv_dim
    hpg = nh // SSM_GROUPS
    tm = _pick(nt, (1024, 512, 256, 128))
    proj = _mm(h.reshape(nt, d), w_in[:, :nproj].astype(BF16), tm, _pick(nproj, (2048, 1024, 512, 256, 128)))
    a_neg = -jnp.exp(a_log.astype(F32))
    dtc, csc, csr = _dt_prep(h, w_in[:, nproj:], dt_bias, a_neg, pad)
    nc = lp // SSM_CHUNK
    csr = csr.reshape(bsz, nc, SSM_GROUPS, hpg, SSM_CHUNK).transpose(0, 2, 1, 3, 4)
    d_exp = jnp.repeat(d_skip.astype(F32), SSM_HEAD_DIM)[None, :]
    y = _ssd(proj.reshape(bsz, lp, nproj), dtc, csc, csr, conv_w, conv_b[None, :], d_exp, norm_g[None, :],
             pad, d_inner)
    out = _mm_resid_ln(y.reshape(nt, d_inner), w_out.astype(BF16), h.reshape(nt, d),
                       mix_g[None, :], mix_b[None, :], alpha, _pick(nt, (512, 256, 128)))
    return out.reshape(bsz, lp, d)


def kernel(x, meta_tokens, conv_w1, conv_b1, conv_dw, conv_dwb, conv_ln_g, conv_ln_b, conv_w2, conv_b2,
           ssm_w_in, ssm_conv_w, ssm_conv_b, ssm_dt_bias, ssm_a_log, ssm_d, ssm_norm_g, ssm_w_out,
           moe_w_router, moe_b_router, moe_w1, moe_b1, moe_w2, moe_b2,
           ln_mix_g, ln_mix_b, ln_ffn_g, ln_ffn_b):
    bsz, seq, d = x.shape
    depth = ln_mix_g.shape[0]
    alpha = (2 * depth) ** 0.25
    pad = SSM_CHUNK - N_META
    lp = pad + N_META + seq
    assert lp % SSM_CHUNK == 0
    meta = jnp.broadcast_to(meta_tokens.astype(x.dtype)[None], (bsz, N_META, d))
    h = jnp.concatenate([jnp.zeros((bsz, pad, d), x.dtype), meta, x], axis=1)
    for i in range(depth):
        j = i // 2
        if i % 2 == 0:
            h = _conformer_layer(h, conv_w1[j], conv_b1[j], conv_dw[j], conv_dwb[j], conv_ln_g[j],
                                 conv_ln_b[j], conv_w2[j], conv_b2[j], ln_mix_g[i], ln_mix_b[i], alpha, pad)
        else:
            h = _mamba_layer(h, ssm_w_in[j], ssm_conv_w[j], ssm_conv_b[j], ssm_dt_bias[j], ssm_a_log[j],
                             ssm_d[j], ssm_norm_g[j], ssm_w_out[j], ln_mix_g[i], ln_mix_b[i], alpha, pad)
        t = _moe_layer(h.reshape(bsz * lp, d), i, lp, pad, moe_w_router[i], moe_b_router[i], moe_w1, moe_b1[i],
                       moe_w2, moe_b2[i], ln_ffn_g[i], ln_ffn_b[i], alpha)
        h = t.reshape(bsz, lp, d)
    return h[:, pad + N_META:]
```

```python
import functools

import jax
import jax.numpy as jnp
from jax import lax
from jax.experimental import pallas as pl
from jax.experimental.pallas import tpu as pltpu

F32 = jnp.float32
BF16 = jnp.bfloat16
I32 = jnp.int32
HIGHEST = lax.Precision.HIGHEST

N_META = 16
SSM_HEAD_DIM = 64
SSM_GROUPS = 8
SSM_STATE = 128
SSM_CHUNK = 128
TOP_K = 4
SWIGLU_ALPHA = 1.702
SWIGLU_LIMIT = 7.0
LN_EPS = 1e-5
RMS_EPS = 1e-5

CONV_HALO = 32
CONV_ROWS = 64
CONV_LANES = 128
MOE_ROWS = 512
DMA_UNROLL = 8
SEG_ROWS = 16
LANES = 128
VMEM_LIMIT = 56 * 1024 * 1024


def _cparams(*sem):
    return pltpu.CompilerParams(dimension_semantics=tuple(sem), vmem_limit_bytes=VMEM_LIMIT)


def _pick(n, prefs):
    for p in prefs:
        if n % p == 0:
            return p
    raise ValueError(f"no tile for {n} in {prefs}")


def _layer_norm(y, g, b):
    mu = jnp.mean(y, axis=-1, keepdims=True)
    yc = y - mu
    var = jnp.mean(yc * yc, axis=-1, keepdims=True)
    return yc * lax.rsqrt(var + LN_EPS) * g + b


def _dot_3pass(a, b, dims):
    a_hi = a.astype(BF16)
    b_hi = b.astype(BF16)
    a_lo = (a - a_hi.astype(F32)).astype(BF16)
    b_lo = (b - b_hi.astype(F32)).astype(BF16)
    dot = lambda u, v: lax.dot_general(u, v, (dims, ((), ())), preferred_element_type=F32)
    return dot(a_hi, b_hi) + (dot(a_hi, b_lo) + dot(a_lo, b_hi))


def _silu(x):
    h = 0.5 * x
    return h + h * jnp.tanh(h)


def _sigmoid(x):
    return 1.0 / (1.0 + jnp.exp(-x))


def _softplus(x):
    return jnp.maximum(x, 0.0) + jnp.log(1.0 + jnp.exp(-jnp.abs(x)))


def _glu_kernel(x_ref, wa_ref, wg_ref, ba_ref, bg_ref, o_ref, *, pad):
    j = pl.program_id(1)
    x = x_ref[0].astype(BF16)
    a = jnp.dot(x, wa_ref[...], preferred_element_type=F32) + ba_ref[...]
    g = jnp.dot(x, wg_ref[...], preferred_element_type=F32) + bg_ref[...]
    u = a * _sigmoid(g)
    row = lax.broadcasted_iota(I32, u.shape, 0)
    valid = jnp.logical_or(j > 0, row >= pad)
    o_ref[0] = jnp.where(valid, u, 0.0)


def _glu(h, wa, wg, ba, bg, pad, tl):
    bsz, lp, d = h.shape
    c = wa.shape[1]
    return pl.pallas_call(
        functools.partial(_glu_kernel, pad=pad),
        grid=(bsz, lp // tl),
        in_specs=[
            pl.BlockSpec((1, tl, d), lambda b, j: (b, j, 0)),
            pl.BlockSpec((d, c), lambda b, j: (0, 0)),
            pl.BlockSpec((d, c), lambda b, j: (0, 0)),
            pl.BlockSpec((1, c), lambda b, j: (0, 0)),
            pl.BlockSpec((1, c), lambda b, j: (0, 0)),
        ],
        out_specs=pl.BlockSpec((1, tl, c), lambda b, j: (b, j, 0)),
        out_shape=jax.ShapeDtypeStruct((bsz, lp, c), F32),
        compiler_params=_cparams("parallel", "parallel"),
        name="conf_glu",
    )(h, wa, wg, ba, bg)


def _conv_mix_kernel(ucur_ref, uprev_ref, h_ref, dw_ref, dwb_ref, g1_ref, b1_ref, w2_ref, b2_ref,
                     g2_ref, bb2_ref, o_ref, win_ref, act_ref, *, taps, alpha):
    j = pl.program_id(1)
    tl = ucur_ref.shape[1]

    @pl.when(j == 0)
    def _():
        win_ref[0:CONV_HALO, :] = jnp.zeros((CONV_HALO, win_ref.shape[1]), F32)

    @pl.when(j > 0)
    def _():
        win_ref[0:CONV_HALO, :] = uprev_ref[0, tl - CONV_HALO:tl, :]

    win_ref[CONV_HALO:CONV_HALO + tl, :] = ucur_ref[0]

    def chunk(c, carry):
        base = pl.multiple_of(c * CONV_ROWS, CONV_ROWS)
        n = CONV_HALO + CONV_ROWS
        parts = []
        for lc in range(win_ref.shape[1] // CONV_LANES):
            sl = slice(lc * CONV_LANES, (lc + 1) * CONV_LANES)
            win = win_ref[pl.ds(base, n), sl]
            acc = jnp.zeros((CONV_ROWS, CONV_LANES), F32) + dwb_ref[:, sl]
            for s in range(8):
                ws = win if s == 0 else pltpu.roll(win, n - s, 0)
                for k in range(taps):
                    off = CONV_HALO + k - (taps - 1)
                    if off % 8 == s:
                        acc = acc + ws[off - s:off - s + CONV_ROWS, :] * dw_ref[k:k + 1, sl]
            parts.append(acc)
        y = _layer_norm(jnp.concatenate(parts, axis=1), g1_ref[...], b1_ref[...])
        y = _silu(y)
        act_ref[pl.ds(base, CONV_ROWS), :] = y.astype(BF16)
        return carry

    lax.fori_loop(0, tl // CONV_ROWS, chunk, 0, unroll=4)
    mix = jnp.dot(act_ref[...], w2_ref[...], preferred_element_type=F32) + b2_ref[...]
    y = alpha * h_ref[0] + mix
    o_ref[0] = _layer_norm(y, g2_ref[...], bb2_ref[...])


def _conv_mix(u, h, dw, dwb, g1, b1, w2, b2, g2, bb2, alpha, tl):
    bsz, lp, c = u.shape
    d = h.shape[2]
    taps = dw.shape[0]
    assert taps - 1 <= CONV_HALO and tl % CONV_ROWS == 0
    vec = lambda n: pl.BlockSpec((1, n), lambda b, j: (0, 0))
    return pl.pallas_call(
        functools.partial(_conv_mix_kernel, taps=taps, alpha=alpha),
        grid=(bsz, lp // tl),
        in_specs=[
            pl.BlockSpec((1, tl, c), lambda b, j: (b, j, 0)),
            pl.BlockSpec((1, tl, c), lambda b, j: (b, jnp.maximum(j - 1, 0), 0)),
            pl.BlockSpec((1, tl, d), lambda b, j: (b, j, 0)),
            pl.BlockSpec((taps, c), lambda b, j: (0, 0)),
            vec(c), vec(c), vec(c),
            pl.BlockSpec((c, d), lambda b, j: (0, 0)),
            vec(d), vec(d), vec(d),
        ],
        out_specs=pl.BlockSpec((1, tl, d), lambda b, j: (b, j, 0)),
        out_shape=jax.ShapeDtypeStruct((bsz, lp, d), F32),
        scratch_shapes=[pltpu.VMEM((CONV_HALO + tl, c), F32), pltpu.VMEM((tl, c), BF16)],
        compiler_params=_cparams("parallel", "parallel"),
        name="conf_conv_mix",
    )(u, u, h, dw, dwb, g1, b1, w2, b2, g2, bb2)


def _mm_kernel(x_ref, w_ref, o_ref):
    o_ref[...] = jnp.dot(x_ref[...].astype(BF16), w_ref[...], preferred_element_type=F32).astype(o_ref.dtype)


def _mm(x, w, tm, tn, out_dtype=F32):
    m, k = x.shape
    n = w.shape[1]
    return pl.pallas_call(
        _mm_kernel,
        grid=(n // tn, m // tm),
        in_specs=[pl.BlockSpec((tm, k), lambda jn, im: (im, 0)),
                  pl.BlockSpec((k, tn), lambda jn, im: (0, jn))],
        out_specs=pl.BlockSpec((tm, tn), lambda jn, im: (im, jn)),
        out_shape=jax.ShapeDtypeStruct((m, n), out_dtype),
        compiler_params=_cparams("parallel", "parallel"),
        name="mm",
    )(x, w)


def _mm_resid_ln_kernel(x_ref, w_ref, h_ref, g_ref, b_ref, o_ref, *, alpha):
    mix = jnp.dot(x_ref[...].astype(BF16), w_ref[...], preferred_element_type=F32)
    o_ref[...] = _layer_norm(alpha * h_ref[...] + mix, g_ref[...], b_ref[...])


def _mm_resid_ln(x, w, h, g, b, alpha, tm):
    m, k = x.shape
    d = w.shape[1]
    return pl.pallas_call(
        functools.partial(_mm_resid_ln_kernel, alpha=alpha),
        grid=(m // tm,),
        in_specs=[pl.BlockSpec((tm, k), lambda i: (i, 0)),
                  pl.BlockSpec((k, d), lambda i: (0, 0)),
                  pl.BlockSpec((tm, d), lambda i: (i, 0)),
                  pl.BlockSpec((1, d), lambda i: (0, 0)),
                  pl.BlockSpec((1, d), lambda i: (0, 0))],
        out_specs=pl.BlockSpec((tm, d), lambda i: (i, 0)),
        out_shape=jax.ShapeDtypeStruct((m, d), F32),
        compiler_params=_cparams("parallel"),
        name="mm_resid_ln",
    )(x, w, h, g, b)


def _dt_kernel(x_ref, wc_ref, bc_ref, ac_ref, dtc_ref, csc_ref, csr_ref, *, pad):
    c = pl.program_id(1)
    x = x_ref[0]
    n = x.shape[0]
    nh = csr_ref.shape[2]
    raw_c = _dot_3pass(x, wc_ref[...], ((1,), (0,)))
    row_c = lax.broadcasted_iota(I32, raw_c.shape, 0)
    dt_c = jnp.where(jnp.logical_and(c == 0, row_c < pad), 0.0, _softplus(raw_c + bc_ref[...]))
    li = lax.broadcasted_iota(I32, (n, n), 0)
    si = lax.broadcasted_iota(I32, (n, n), 1)
    tri = jnp.where(si <= li, 1.0, 0.0)
    cs_c = jnp.dot(tri, dt_c * ac_ref[...], preferred_element_type=F32, precision=HIGHEST)
    dtc_ref[0] = dt_c
    csc_ref[0] = cs_c
    csr_ref[0, 0] = cs_c.T[:nh, :]


def _dt_prep(h, w_dt, dt_bias, a_neg, pad):
    bsz, lp, d = h.shape
    nh = w_dt.shape[1]
    nc = lp // SSM_CHUNK
    lanes = 128
    wc = jnp.zeros((d, lanes), F32).at[:, :nh].set(w_dt)
    bc = jnp.zeros((1, lanes), F32).at[0, :nh].set(dt_bias)
    ac = jnp.zeros((1, lanes), F32).at[0, :nh].set(a_neg)
    return pl.pallas_call(
        functools.partial(_dt_kernel, pad=pad),
        grid=(bsz, nc),
        in_specs=[
            pl.BlockSpec((1, SSM_CHUNK, d), lambda b, c: (b, c, 0)),
            pl.BlockSpec((d, lanes), lambda b, c: (0, 0)),
            pl.BlockSpec((1, lanes), lambda b, c: (0, 0)),
            pl.BlockSpec((1, lanes), lambda b, c: (0, 0)),
        ],
        out_specs=[
            pl.BlockSpec((1, SSM_CHUNK, lanes), lambda b, c: (b, c, 0)),
            pl.BlockSpec((1, SSM_CHUNK, lanes), lambda b, c: (b, c, 0)),
            pl.BlockSpec((1, 1, nh, SSM_CHUNK), lambda b, c: (b, c, 0, 0)),
        ],
        out_shape=[
            jax.ShapeDtypeStruct((bsz, lp, lanes), F32),
            jax.ShapeDtypeStruct((bsz, lp, lanes), F32),
            jax.ShapeDtypeStruct((bsz, nc, nh, SSM_CHUNK), F32),
        ],
        compiler_params=_cparams("parallel", "parallel"),
        name="ssm_dt",
    )(h, wc, bc, ac)


def _ssd_kernel(xs_ref, bm_ref, cm_ref, z_ref, dtc_ref, csc_ref, csr_ref,
                cwx_ref, cwb_ref, cwc_ref, cbx_ref, cbb_ref, cbc_ref, dsk_ref, ng_ref,
                o_ref, state_ref, carx_ref, carb_ref, carc_ref, *, pad, hpg, taps):
    g = pl.program_id(1)
    q = SSM_CHUNK
    hd = SSM_HEAD_DIM
    gw = hpg * hd
    n_chunks = xs_ref.shape[1] // q

    state_ref[...] = jnp.zeros(state_ref.shape, F32)
    carx_ref[...] = jnp.zeros(carx_ref.shape, F32)
    carb_ref[...] = jnp.zeros(carb_ref.shape, F32)
    carc_ref[...] = jnp.zeros(carc_ref.shape, F32)

    hrow = lax.broadcasted_iota(I32, (128, gw), 0)
    hcol = lax.broadcasted_iota(I32, (128, gw), 1) // hd
    sel = jnp.where(hrow == g * hpg + hcol, 1.0, 0.0).astype(BF16)

    def select_heads(v):
        hi = v.astype(BF16)
        r1 = v - hi.astype(F32)
        mid = r1.astype(BF16)
        lo = (r1 - mid.astype(F32)).astype(BF16)
        out = jnp.dot(hi, sel, preferred_element_type=F32)
        out = out + jnp.dot(mid, sel, preferred_element_type=F32)
        return out + jnp.dot(lo, sel, preferred_element_type=F32)

    lane_head = lax.broadcasted_iota(I32, (q, gw), 1) // hd
    head_mask = [jnp.where(lane_head == r, 1.0, 0.0).astype(BF16) for r in range(hpg)]
    li = lax.broadcasted_iota(I32, (q, q), 0)
    si = lax.broadcasted_iota(I32, (q, q), 1)
    causal = si <= li
    row1 = lax.broadcasted_iota(I32, (q, 1), 0)

    def conv_act(raw, car_ref, w_ref, b_ref, valid):
        if valid is not None:
            raw = jnp.where(valid, raw, 0.0)
        win = jnp.concatenate([car_ref[...], raw], axis=0)
        acc = win * w_ref[0:1, :]
        for k in range(1, taps):
            acc = pltpu.roll(acc, 1, 0) + win * w_ref[k:k + 1, :]
        acc = acc[8:8 + q, :] + b_ref[...]
        car_ref[...] = raw[q - 8:q, :]
        act = _silu(acc)
        return act if valid is None else jnp.where(valid, act, 0.0)

    def chunk(c, first):
        r0 = 0 if first else pl.multiple_of(c * q, q)
        valid = (row1 >= pad) if first else None
        xs = conv_act(xs_ref[0, pl.ds(r0, q), :], carx_ref, cwx_ref, cbx_ref, valid)
        bm = conv_act(bm_ref[0, pl.ds(r0, q), :], carb_ref, cwb_ref, cbb_ref, valid)
        cm = conv_act(cm_ref[0, pl.ds(r0, q), :], carc_ref, cwc_ref, cbc_ref, valid)

        dt_e = select_heads(dtc_ref[0, pl.ds(r0, q), :])
        cs_e = select_heads(csc_ref[0, pl.ds(r0, q), :])
        cs_r = csr_ref[0, 0, c]
        a_last = cs_e[q - 1:q, :]

        x_dt = xs * dt_e
        xw = (x_dt * jnp.exp(a_last - cs_e)).astype(BF16)
        bm16 = bm.astype(BF16)
        cm16 = cm.astype(BF16)
        cb = lax.dot_general(cm16, bm16, (((1,), (1,)), ((), ())), preferred_element_type=F32)

        m_parts = []
        x_parts = []
        x_dt16 = x_dt.astype(BF16)
        for r in range(hpg):
            col = cs_e[:, r * hd:r * hd + 1]
            seg = col - cs_r[r:r + 1, :]
            decay = jnp.exp(jnp.where(causal, seg, -jnp.inf))
            m_parts.append((cb * decay).astype(BF16))
            x_parts.append(x_dt16 * head_mask[r])
        m_cat = jnp.concatenate(m_parts, axis=1)
        x_bd = jnp.concatenate(x_parts, axis=0)
        y = jnp.dot(m_cat, x_bd, preferred_element_type=F32)

        st = state_ref[...]
        y = y + jnp.dot(cm16, st.astype(BF16), preferred_element_type=F32) * jnp.exp(cs_e)
        new = jnp.dot(bm.T.astype(BF16), xw, preferred_element_type=F32)
        state_ref[...] = st * jnp.exp(a_last) + new
        y = y + dsk_ref[...] * xs

        z = z_ref[0, pl.ds(r0, q), :]
        y = y * _silu(z)
        y = y * lax.rsqrt(jnp.mean(y * y, axis=-1, keepdims=True) + RMS_EPS) * ng_ref[...]
        o_ref[0, pl.ds(r0, q), :] = y.astype(o_ref.dtype)

    def later_chunk(c, carry):
        chunk(c, False)
        return carry

    chunk(0, True)
    lax.fori_loop(1, n_chunks, later_chunk, 0, unroll=4)


def _ssd(proj, dtc, csc, csr, conv_w, conv_b, d_exp, norm_g, pad, d_inner):
    bsz, lp, _ = proj.shape
    g = SSM_GROUPS
    n = SSM_STATE
    gw = d_inner // g
    hpg = gw // SSM_HEAD_DIM
    assert gw % 128 == 0 and n == 128 and dtc.shape[2] == 128
    taps = conv_w.shape[0]
    nc = lp // SSM_CHUNK
    zb = 0
    xb = d_inner // gw
    bb = 2 * d_inner // n
    cb_ = (2 * d_inner + g * n) // n
    cxb = 0
    cbb = d_inner // n
    ccb = (d_inner + g * n) // n
    seq = lambda w, off: pl.BlockSpec((1, lp, w), lambda b, j, off=off: (b, 0, off + j))
    cw = lambda w, off: pl.BlockSpec((taps, w), lambda b, j, off=off: (0, off + j))
    cbv = lambda w, off: pl.BlockSpec((1, w), lambda b, j, off=off: (0, off + j))
    return pl.pallas_call(
        functools.partial(_ssd_kernel, pad=pad, hpg=hpg, taps=taps),
        grid=(bsz, g),
        in_specs=[
            seq(gw, xb), seq(n, bb), seq(n, cb_), seq(gw, zb),
            pl.BlockSpec((1, lp, 128), lambda b, j: (b, 0, 0)),
            pl.BlockSpec((1, lp, 128), lambda b, j: (b, 0, 0)),
            pl.BlockSpec((1, 1, nc, hpg, SSM_CHUNK), lambda b, j: (b, j, 0, 0, 0)),
            cw(gw, cxb), cw(n, cbb), cw(n, ccb),
            cbv(gw, cxb), cbv(n, cbb), cbv(n, ccb),
            pl.BlockSpec((1, gw), lambda b, j: (0, j)),
            pl.BlockSpec((1, gw), lambda b, j: (0, j)),
        ],
        out_specs=pl.BlockSpec((1, lp, gw), lambda b, j: (b, 0, j)),
        out_shape=jax.ShapeDtypeStruct((bsz, lp, d_inner), BF16),
        scratch_shapes=[pltpu.VMEM((n, gw), F32), pltpu.VMEM((8, gw), F32),
                        pltpu.VMEM((8, n), F32), pltpu.VMEM((8, n), F32)],
        compiler_params=_cparams("parallel", "parallel"),
        name="ssm_ssd",
    )(proj, proj, proj, proj, dtc, csc, csr, conv_w, conv_w, conv_w, conv_b, conv_b, conv_b, d_exp, norm_g)


def _pad_chunk(tile, q, tm, chunks_per_row):
    return lax.rem(tile * (tm // SSM_CHUNK) + q, chunks_per_row) == 0


def _router_kernel(x_ref, wr_ref, br_ref, lpos_ref, gate_ref, tab_ref, cnt_ref, carry_ref, *, chunks_per_row, pad):
    i = pl.program_id(0)

    @pl.when(i == 0)
    def _():
        carry_ref[...] = jnp.zeros(carry_ref.shape, F32)

    x = x_ref[...]
    tm = x.shape[0]
    ne = wr_ref.shape[0]
    logits = _dot_3pass(wr_ref[...], x, ((1,), (1,))) + br_ref[...]
    ids = lax.broadcasted_iota(I32, (ne, tm), 0)
    vals = logits
    top_v, sels = [], []
    for _ in range(TOP_K):
        m = jnp.max(vals, axis=0, keepdims=True)
        idx = jnp.min(jnp.where(vals == m, ids, ne), axis=0, keepdims=True)
        sel = ids == idx
        top_v.append(m)
        sels.append(sel)
        vals = jnp.where(sel, -jnp.inf, vals)
    ex = [jnp.exp(v - top_v[0]) for v in top_v]
    den = ex[0]
    for t in ex[1:]:
        den = den + t
    gate_ref[...] = jnp.concatenate([t / den for t in ex], axis=0)

    onehot = sels[0].astype(F32)
    for s in sels[1:]:
        onehot = onehot + s.astype(F32)
    lane = lax.broadcasted_iota(I32, (1, tm), 1)
    real = jnp.ones((1, tm), F32)
    for q in range(tm // SSM_CHUNK):
        inert = jnp.logical_and(_pad_chunk(i, q, tm, chunks_per_row),
                                jnp.logical_and(lane >= q * SSM_CHUNK, lane < q * SSM_CHUNK + pad))
        real = jnp.where(inert, 0.0, real)
    onehot = onehot * real
    ji = lax.broadcasted_iota(I32, (tm, tm), 0)
    ti = lax.broadcasted_iota(I32, (tm, tm), 1)
    before = (ji < ti).astype(BF16)
    prefix = jnp.dot(onehot.astype(BF16), before, preferred_element_type=F32)
    tile_cnt = jnp.sum(onehot, axis=1, keepdims=True) + jnp.zeros((ne, LANES), F32)
    pieces = jnp.floor((tile_cnt + (SEG_ROWS - 1.0)) * (1.0 / SEG_ROWS))
    ei = lax.broadcasted_iota(I32, (ne, ne), 0)
    ej = lax.broadcasted_iota(I32, (ne, ne), 1)
    below = jnp.where(ej < ei, 1.0, 0.0).astype(BF16)
    lstart = SEG_ROWS * jnp.dot(below, pieces.astype(BF16), preferred_element_type=F32)
    local = prefix + lstart[:, 0:1]
    lpos = [jnp.sum(jnp.where(s, local, 0.0), axis=0, keepdims=True) for s in sels]
    spare = float(TOP_K * tm + ne * SEG_ROWS - 1)
    lpos_ref[...] = jnp.where(real > 0.0, jnp.concatenate(lpos, axis=0), spare).astype(I32)
    tab_ref[0, 0] = tile_cnt.astype(I32)
    tab_ref[0, 1] = lstart.astype(I32)
    tab_ref[0, 2] = carry_ref[...].astype(I32)
    carry_ref[...] = carry_ref[...] + tile_cnt
    cnt_ref[...] = carry_ref[...].astype(I32)


def _router(t, w_r, b_r, tm, chunks_per_row, pad):
    nt, d = t.shape
    ne = w_r.shape[1]
    assert tm % SSM_CHUNK == 0
    return pl.pallas_call(
        functools.partial(_router_kernel, chunks_per_row=chunks_per_row, pad=pad),
        grid=(nt // tm,),
        in_specs=[pl.BlockSpec((tm, d), lambda i: (i, 0)),
                  pl.BlockSpec((ne, d), lambda i: (0, 0)),
                  pl.BlockSpec((ne, 1), lambda i: (0, 0))],
        out_specs=[pl.BlockSpec((TOP_K, tm), lambda i: (0, i)),
                   pl.BlockSpec((TOP_K, tm), lambda i: (0, i)),
                   pl.BlockSpec((1, 3, ne, LANES), lambda i: (i, 0, 0, 0)),
                   pl.BlockSpec((ne, LANES), lambda i: (0, 0))],
        out_shape=[jax.ShapeDtypeStruct((TOP_K, nt), I32),
                   jax.ShapeDtypeStruct((TOP_K, nt), F32),
                   jax.ShapeDtypeStruct((nt // tm, 3, ne, LANES), I32),
                   jax.ShapeDtypeStruct((ne, LANES), I32)],
        scratch_shapes=[pltpu.VMEM((ne, 128), F32)],
        compiler_params=_cparams("arbitrary"),
        name="moe_router",
    )(t, w_r.T, b_r[:, None])


def _dispatch_kernel(pend_ref, padded_ref, nused_ref, cnt_ref, lst_ref, dst_ref, tot_ref, lpos_hbm, t_ref, xbuf_hbm,
                     idx_ref, rows_ref, sort_ref, zero_ref, isem, sem, zsem, *, tm, nseg, n_blk):
    i = pl.program_id(0)
    n = TOP_K * tm
    blk = MOE_ROWS * nseg

    @pl.when(i == 0)
    def _():
        zero_ref[...] = jnp.zeros(zero_ref.shape, zero_ref.dtype)

        def group_fill(e):
            return pltpu.make_async_copy(
                zero_ref, xbuf_hbm.at[pl.ds(pl.multiple_of((pend_ref[e] - MOE_ROWS) * nseg, blk), blk), :], zsem)

        def tail_fill(j):
            return pltpu.make_async_copy(zero_ref, xbuf_hbm.at[pl.ds(pl.multiple_of(j * blk, blk), blk), :], zsem)

        def start_group(e, carry):
            @pl.when(padded_ref[e] > 0)
            def _():
                group_fill(e).start()
            return carry

        def wait_group(e, carry):
            @pl.when(padded_ref[e] > 0)
            def _():
                group_fill(e).wait()
            return carry

        def start_tail(j, carry):
            tail_fill(j).start()
            return carry

        def wait_tail(j, carry):
            tail_fill(j).wait()
            return carry

        n_exp = pend_ref.shape[0]
        lax.fori_loop(0, n_exp, start_group, 0)
        lax.fori_loop(nused_ref[0], n_blk, start_tail, 0)
        lax.fori_loop(0, n_exp, wait_group, 0)
        lax.fori_loop(nused_ref[0], n_blk, wait_tail, 0)

    last = pl.num_programs(0) - 1

    n_exp = pend_ref.shape[0]
    piece = SEG_ROWS * nseg

    def run_copies(slot, tile, start):
        def per_expert(e, carry):
            cnt = cnt_ref[tile * n_exp + e]
            src0 = lst_ref[tile * n_exp + e]
            dst0 = dst_ref[tile * n_exp + e]
            full = cnt // SEG_ROWS

            def piece_copy(j, c2):
                cp = pltpu.make_async_copy(
                    sort_ref.at[slot, pl.ds(pl.multiple_of((src0 + j * SEG_ROWS) * nseg, nseg), piece), :],
                    xbuf_hbm.at[pl.ds(pl.multiple_of((dst0 + j * SEG_ROWS) * nseg, nseg), piece), :],
                    sem.at[slot])
                if start:
                    cp.start()
                else:
                    cp.wait()
                return c2

            def row_copy(r, c2):
                cp = pltpu.make_async_copy(
                    sort_ref.at[slot, pl.ds(pl.multiple_of((src0 + r) * nseg, nseg), nseg), :],
                    xbuf_hbm.at[pl.ds(pl.multiple_of((dst0 + r) * nseg, nseg), nseg), :],
                    sem.at[slot])
                if start:
                    cp.start()
                else:
                    cp.wait()
                return c2

            lax.fori_loop(0, full, piece_copy, 0)
            lax.fori_loop(full * SEG_ROWS, cnt, row_copy, 0)
            return carry

        lax.fori_loop(0, n_exp, per_expert, 0)

    def wait_tile(slot, tile):
        rows_moved = tot_ref[tile]
        piece_wait = pltpu.make_async_copy(sort_ref.at[slot, pl.ds(0, piece), :], xbuf_hbm.at[pl.ds(0, piece), :],
                                           sem.at[slot])
        row_wait = pltpu.make_async_copy(sort_ref.at[slot, pl.ds(0, nseg), :], xbuf_hbm.at[pl.ds(0, nseg), :],
                                         sem.at[slot])

        def wait_piece(j, carry):
            piece_wait.wait()
            return carry

        def wait_row(j, carry):
            row_wait.wait()
            return carry

        lax.fori_loop(0, rows_moved // SEG_ROWS, wait_piece, 0)
        lax.fori_loop(0, rows_moved % SEG_ROWS, wait_row, 0)

    for slot in range(2):
        @pl.when(i % 2 == slot)
        def _():
            @pl.when(i >= 2)
            def _():
                wait_tile(slot, i - 2)

            cp = pltpu.make_async_copy(lpos_hbm.at[pl.ds(pl.multiple_of(i * n, n), n)], idx_ref, isem)
            cp.start()
            x = t_ref[...]
            for c in range(nseg):
                rows_ref[pl.ds(c, tm, stride=nseg), :] = x[:, c * LANES:(c + 1) * LANES]
            cp.wait()

            def place(grp, carry):
                for u in range(DMA_UNROLL):
                    r = grp * DMA_UNROLL + u
                    slab = rows_ref[pl.ds(pl.multiple_of(r * nseg, nseg), nseg), :]
                    for k in range(TOP_K):
                        sort_ref[slot, pl.ds(pl.multiple_of(idx_ref[k * tm + r] * nseg, nseg), nseg), :] = slab
                return carry

            lax.fori_loop(0, tm // DMA_UNROLL, place, 0)
            run_copies(slot, i, True)

            @pl.when(i == last)
            def _():
                wait_tile(slot, i)

                @pl.when(i >= 1)
                def _():
                    wait_tile(1 - slot, i - 1)


def _dispatch(t, lpos_tiles, pend, padded, n_used, seg_cnt, seg_lst, seg_dst, tile_rows, n_rows, tm):
    nt, d = t.shape
    nseg = d // LANES
    sort_rows = TOP_K * tm + pend.shape[0] * SEG_ROWS
    return pl.pallas_call(
        functools.partial(_dispatch_kernel, tm=tm, nseg=nseg, n_blk=n_rows // MOE_ROWS),
        grid_spec=pltpu.PrefetchScalarGridSpec(
            num_scalar_prefetch=7,
            grid=(nt // tm,),
            in_specs=[pl.BlockSpec(memory_space=pl.ANY),
                      pl.BlockSpec((tm, d), lambda i, *_: (i, 0))],
            out_specs=pl.BlockSpec(memory_space=pl.ANY),
            scratch_shapes=[pltpu.SMEM((TOP_K * tm,), I32), pltpu.VMEM((tm * nseg, LANES), t.dtype),
                            pltpu.VMEM((2, sort_rows * nseg, LANES), t.dtype),
                            pltpu.VMEM((MOE_ROWS * nseg, LANES), t.dtype),
                            pltpu.SemaphoreType.DMA(()), pltpu.SemaphoreType.DMA((2,)),
                            pltpu.SemaphoreType.DMA(())],
        ),
        out_shape=jax.ShapeDtypeStruct((n_rows * nseg, LANES), t.dtype),
        compiler_params=pltpu.CompilerParams(dimension_semantics=("arbitrary",), has_side_effects=True,
                                             vmem_limit_bytes=VMEM_LIMIT),
        name="moe_dispatch",
    )(pend, padded, n_used, seg_cnt, seg_lst, seg_dst, tile_rows, lpos_tiles, t)


def _ffn_kernel(blk_e_ref, nused_ref, x_ref, w1_ref, b1g_ref, b1l_ref, w2_ref, b2_ref, y_ref,
                w1g_ref, w1l_ref, w2c_ref, *, nseg):
    i = pl.program_id(0)
    rows = x_ref.shape[0] // nseg
    e = blk_e_ref[i]
    e_prev = blk_e_ref[jnp.maximum(i - 1, 0)]

    @pl.when(jnp.logical_or(i == 0, e != e_prev))
    def _():
        w = 2 * LANES
        r = lax.broadcasted_iota(I32, (w, w), 0)
        c = lax.broadcasted_iota(I32, (w, w), 1)
        src_col = jnp.where(c < LANES, 2 * c, 2 * (c - LANES) + 1)
        perm = jnp.where(r == src_col, 1.0, 0.0).astype(BF16)
        for cc in range(w1_ref.shape[3] // w):
            chunk = w1_ref[0, 0, :, cc * w:(cc + 1) * w].astype(BF16)
            res = jnp.dot(chunk, perm, preferred_element_type=F32).astype(BF16)
            w1g_ref[:, cc * LANES:(cc + 1) * LANES] = res[:, :LANES]
            w1l_ref[:, cc * LANES:(cc + 1) * LANES] = res[:, LANES:]
        w2c_ref[...] = w2_ref[0, 0].astype(BF16)

    @pl.when(i < nused_ref[0])
    def _():
        x = jnp.concatenate([x_ref[pl.ds(c, rows, stride=nseg), :] for c in range(nseg)], axis=1).astype(BF16)
        hg = jnp.dot(x, w1g_ref[...], preferred_element_type=F32) + b1g_ref[0]
        hl = jnp.dot(x, w1l_ref[...], preferred_element_type=F32) + b1l_ref[0]
        xg = jnp.minimum(hg, SWIGLU_LIMIT)
        xl = jnp.clip(hl, -SWIGLU_LIMIT, SWIGLU_LIMIT)
        act = xg * _sigmoid(SWIGLU_ALPHA * xg) * (xl + 1.0)
        y = jnp.dot(act.astype(BF16), w2c_ref[...], preferred_element_type=F32) + b2_ref[0]
        for c in range(nseg):
            y_ref[pl.ds(c, rows, stride=nseg), :] = y[:, c * LANES:(c + 1) * LANES]

    @pl.when(i >= nused_ref[0])
    def _():
        y_ref[...] = jnp.zeros(y_ref.shape, F32)


def _ffn(x_buf, blk_e, n_used, layer, w1, b1g, b1l, w2, b2):
    _, ne, d, f2 = w1.shape
    f = f2 // 2
    nseg = d // LANES
    n_rows = x_buf.shape[0] // nseg
    n_blk = n_rows // MOE_ROWS
    assert f2 % (2 * LANES) == 0
    wspec = lambda a, b: pl.BlockSpec((1, a, b), lambda i, be, nu: (be[i], 0, 0))
    wfull = lambda a, b: pl.BlockSpec((1, 1, a, b), lambda i, be, nu: (layer, be[i], 0, 0))
    return pl.pallas_call(
        functools.partial(_ffn_kernel, nseg=nseg),
        grid_spec=pltpu.PrefetchScalarGridSpec(
            num_scalar_prefetch=2,
            grid=(n_blk,),
            in_specs=[pl.BlockSpec((MOE_ROWS * nseg, LANES), lambda i, be, nu: (i, 0)),
                      wfull(d, f2), wspec(1, f), wspec(1, f), wfull(f, d), wspec(1, d)],
            out_specs=pl.BlockSpec((MOE_ROWS * nseg, LANES), lambda i, be, nu: (i, 0)),
            scratch_shapes=[pltpu.VMEM((d, f), BF16), pltpu.VMEM((d, f), BF16), pltpu.VMEM((f, d), BF16)],
        ),
        out_shape=jax.ShapeDtypeStruct((n_rows * nseg, LANES), F32),
        compiler_params=_cparams("arbitrary"),
        name="moe_ffn",
    )(blk_e, n_used, x_buf, w1, b1g, b1l, w2, b2)


def _combine_kernel(cnt_ref, lst_ref, dst_ref, pcs_ref, lpos_hbm, gates_hbm, ybuf_hbm, h_ref, g_ref, b_ref, o_ref,
                    idx_ref, gsm_ref, sort_ref, sum_ref, isem, sem, *, tm, nseg, alpha, n_exp):
    i = pl.program_id(0)
    n = TOP_K * tm
    piece = SEG_ROWS * nseg

    def run_copies(slot, tile, start):
        def per_expert(e, carry):
            cnt = cnt_ref[tile * n_exp + e]
            dst0 = lst_ref[tile * n_exp + e]
            src0 = dst_ref[tile * n_exp + e]

            def piece_copy(j, c2):
                cp = pltpu.make_async_copy(
                    ybuf_hbm.at[pl.ds(pl.multiple_of((src0 + j * SEG_ROWS) * nseg, nseg), piece), :],
                    sort_ref.at[slot, pl.ds(pl.multiple_of((dst0 + j * SEG_ROWS) * nseg, nseg), piece), :],
                    sem.at[slot])
                if start:
                    cp.start()
                else:
                    cp.wait()
                return c2

            lax.fori_loop(0, (cnt + SEG_ROWS - 1) // SEG_ROWS, piece_copy, 0)
            return carry

        lax.fori_loop(0, n_exp, per_expert, 0)

    def wait_tile(slot, tile):
        piece_wait = pltpu.make_async_copy(ybuf_hbm.at[pl.ds(0, piece), :], sort_ref.at[slot, pl.ds(0, piece), :],
                                           sem.at[slot])

        def wait_piece(j, carry):
            piece_wait.wait()
            return carry

        lax.fori_loop(0, pcs_ref[tile], wait_piece, 0)

    @pl.when(i == 0)
    def _():
        sort_ref[...] = jnp.zeros(sort_ref.shape, F32)
        run_copies(0, 0, True)

    for cur in range(2):
        @pl.when(i % 2 == cur)
        def _():
            @pl.when(i + 1 < pl.num_programs(0))
            def _():
                run_copies(1 - cur, i + 1, True)

            c1 = pltpu.make_async_copy(lpos_hbm.at[pl.ds(pl.multiple_of(i * n, n), n)], idx_ref, isem.at[0])
            c2 = pltpu.make_async_copy(gates_hbm.at[pl.ds(pl.multiple_of(i * n, n), n)], gsm_ref, isem.at[1])
            c1.start()
            c2.start()
            wait_tile(cur, i)
            c1.wait()
            c2.wait()

            def reduce(grp, carry):
                for u in range(DMA_UNROLL):
                    r = grp * DMA_UNROLL + u
                    acc = None
                    for k in range(TOP_K):
                        slab = sort_ref[cur, pl.ds(pl.multiple_of(idx_ref[k * tm + r] * nseg, nseg), nseg), :]
                        term = gsm_ref[k * tm + r] * slab
                        acc = term if acc is None else acc + term
                    sum_ref[pl.ds(pl.multiple_of(r * nseg, nseg), nseg), :] = acc
                return carry

            lax.fori_loop(0, tm // DMA_UNROLL, reduce, 0)

    ffn = jnp.concatenate([sum_ref[pl.ds(c, tm, stride=nseg), :] for c in range(nseg)], axis=1)
    o_ref[...] = _layer_norm(alpha * h_ref[...] + ffn, g_ref[...], b_ref[...])


def _combine(y_buf, lpos_tiles, gate_tiles, seg_cnt, seg_lst, seg_dst, tile_pieces, h, g, b, alpha, tm, n_exp):
    nt, d = h.shape
    nseg = d // LANES
    sort_rows = TOP_K * tm + n_exp * SEG_ROWS
    return pl.pallas_call(
        functools.partial(_combine_kernel, tm=tm, nseg=nseg, alpha=alpha, n_exp=n_exp),
        grid_spec=pltpu.PrefetchScalarGridSpec(
            num_scalar_prefetch=4,
            grid=(nt // tm,),
            in_specs=[pl.BlockSpec(memory_space=pl.ANY),
                      pl.BlockSpec(memory_space=pl.ANY),
                      pl.BlockSpec(memory_space=pl.ANY),
                      pl.BlockSpec((tm, d), lambda i, *_: (i, 0)),
                      pl.BlockSpec((1, d), lambda i, *_: (0, 0)),
                      pl.BlockSpec((1, d), lambda i, *_: (0, 0))],
            out_specs=pl.BlockSpec((tm, d), lambda i, *_: (i, 0)),
            scratch_shapes=[pltpu.SMEM((TOP_K * tm,), I32), pltpu.SMEM((TOP_K * tm,), F32),
                            pltpu.VMEM((2, sort_rows * nseg, LANES), F32), pltpu.VMEM((tm * nseg, LANES), F32),
                            pltpu.SemaphoreType.DMA((2,)), pltpu.SemaphoreType.DMA((2,))],
        ),
        out_shape=jax.ShapeDtypeStruct((nt, d), F32),
        compiler_params=_cparams("arbitrary"),
        name="moe_combine",
    )(seg_cnt, seg_lst, seg_dst, tile_pieces, lpos_tiles, gate_tiles, y_buf, h, g, b)


def _moe_layer(t, layer, lp, pad, w_r, b_r, w1, b1, w2, b2, ln_g, ln_b, alpha):
    nt, d = t.shape
    ne = w_r.shape[1]
    tm = _pick(nt, (512, 256, 128))
    cpr = lp // SSM_CHUNK
    assert lp % SSM_CHUNK == 0 and pad < SSM_CHUNK
    lpos, gates, tabs, cnt = _router(t, w_r, b_r, tm, cpr, pad)

    counts = cnt[:, 0]
    padded = ((counts + MOE_ROWS - 1) // MOE_ROWS) * MOE_ROWS
    pend = jnp.cumsum(padded)
    poff = pend - padded
    seg_cnt = tabs[:, 0, :, 0].reshape(-1)
    seg_lst = tabs[:, 1, :, 0].reshape(-1)
    seg_dst = (tabs[:, 2, :, 0] + poff[None, :].astype(I32)).reshape(-1)
    tile_rows = jnp.sum(tabs[:, 0, :, 0], axis=1)
    tile_pieces = jnp.sum((tabs[:, 0, :, 0] + SEG_ROWS - 1) // SEG_ROWS, axis=1)
    n_real = (nt - (nt // lp) * pad) * TOP_K
    n_rows = -(-n_real // MOE_ROWS) * MOE_ROWS + (ne + 1) * MOE_ROWS
    n_blk = n_rows // MOE_ROWS
    blk_start = jnp.arange(n_blk, dtype=I32) * MOE_ROWS
    blk_e = jnp.minimum(jnp.sum(blk_start[:, None] >= pend[None, :], axis=1), ne - 1).astype(I32)
    n_used = (pend[-1:] // MOE_ROWS).astype(I32)
    tiles = lambda a: a.reshape(TOP_K, nt // tm, tm).transpose(1, 0, 2).reshape(-1)

    x_buf = _dispatch(t, tiles(lpos), pend.astype(I32), padded.astype(I32), n_used, seg_cnt, seg_lst, seg_dst,
                      tile_rows, n_rows, tm)
    y_buf = _ffn(x_buf, blk_e, n_used, layer, w1, b1[:, None, 0::2], b1[:, None, 1::2], w2, b2[:, None, :])
    return _combine(y_buf, tiles(lpos), tiles(gates), seg_cnt, seg_lst, seg_dst, tile_pieces, t, ln_g[None, :],
                    ln_b[None, :], alpha, tm, ne)


def _conformer_layer(h, w1, b1, dw, dwb, ln_g, ln_b, w2, b2, mix_g, mix_b, alpha, pad):
    bsz, lp, d = h.shape
    c = dw.shape[1]
    tl = _pick(lp, (544, 272, 384, 192, 128))
    u = _glu(h, w1[:, :c].astype(BF16), w1[:, c:].astype(BF16), b1[None, :c], b1[None, c:], pad, tl)
    tl_conv = _pick(lp, (1088, 576, 384, 192, 128))
    return _conv_mix(u, h, dw, dwb[None, :], ln_g[None, :], ln_b[None, :], w2.astype(BF16), b2[None, :],
                     mix_g[None, :], mix_b[None, :], alpha, tl_conv)


def _mamba_layer(h, w_in, conv_w, conv_b, dt_bias, a_log, d_skip, norm_g, w_out, mix_g, mix_b, alpha, pad):
    bsz, lp, d = h.shape
    nt = bsz * lp
    d_inner = w_out.shape[0]
    nh = dt_bias.shape[0]
    conv_dim = conv_w.shape[1]
    nproj = d_inner + conv_dim
    hpg = nh // SSM_GROUPS
    tm = _pick(nt, (1024, 512, 256, 128))
    proj = _mm(h.reshape(nt, d), w_in[:, :nproj].astype(BF16), tm, _pick(nproj, (2048, 1024, 512, 256, 128)))
    a_neg = -jnp.exp(a_log.astype(F32))
    dtc, csc, csr = _dt_prep(h, w_in[:, nproj:], dt_bias, a_neg, pad)
    nc = lp // SSM_CHUNK
    csr = csr.reshape(bsz, nc, SSM_GROUPS, hpg, SSM_CHUNK).transpose(0, 2, 1, 3, 4)
    d_exp = jnp.repeat(d_skip.astype(F32), SSM_HEAD_DIM)[None, :]
    y = _ssd(proj.reshape(bsz, lp, nproj), dtc, csc, csr, conv_w, conv_b[None, :], d_exp, norm_g[None, :],
             pad, d_inner)
    out = _mm_resid_ln(y.reshape(nt, d_inner), w_out.astype(BF16), h.reshape(nt, d),
                       mix_g[None, :], mix_b[None, :], alpha, _pick(nt, (512, 256, 128)))
    return out.reshape(bsz, lp, d)


def kernel(x, meta_tokens, conv_w1, conv_b1, conv_dw, conv_dwb, conv_ln_g, conv_ln_b, conv_w2, conv_b2,
           ssm_w_in, ssm_conv_w, ssm_conv_b, ssm_dt_bias, ssm_a_log, ssm_d, ssm_norm_g, ssm_w_out,
           moe_w_router, moe_b_router, moe_w1, moe_b1, moe_w2, moe_b2,
           ln_mix_g, ln_mix_b, ln_ffn_g, ln_ffn_b):
    bsz, seq, d = x.shape
    depth = ln_mix_g.shape[0]
    alpha = (2 * depth) ** 0.25
    pad = SSM_CHUNK - N_META
    lp = pad + N_META + seq
    assert lp % SSM_CHUNK == 0
    meta = jnp.broadcast_to(meta_tokens.astype(x.dtype)[None], (bsz, N_META, d))
    h = jnp.concatenate([jnp.zeros((bsz, pad, d), x.dtype), meta, x], axis=1)
    for i in range(depth):
        j = i // 2
        if i % 2 == 0:
            h = _conformer_layer(h, conv_w1[j], conv_b1[j], conv_dw[j], conv_dwb[j], conv_ln_g[j],
                                 conv_ln_b[j], conv_w2[j], conv_b2[j], ln_mix_g[i], ln_mix_b[i], alpha, pad)
        else:
            h = _mamba_layer(h, ssm_w_in[j], ssm_conv_w[j], ssm_conv_b[j], ssm_dt_bias[j], ssm_a_log[j],
                             ssm_d[j], ssm_norm_g[j], ssm_w_out[j], ln_mix_g[i], ln_mix_b[i], alpha, pad)
        t = _moe_layer(h.reshape(bsz * lp, d), i, lp, pad, moe_w_router[i], moe_b_router[i], moe_w1, moe_b1[i],
                       moe_w2, moe_b2[i], ln_ffn_g[i], ln_ffn_b[i], alpha)
        h = t.reshape(bsz, lp, d)
    return h[:, pad + N_META:]
```

```python
import functools

import jax
import jax.numpy as jnp
from jax import lax
from jax.experimental import pallas as pl
from jax.experimental.pallas import tpu as pltpu

F32 = jnp.float32
BF16 = jnp.bfloat16
I32 = jnp.int32
HIGHEST = lax.Precision.HIGHEST

N_META = 16
SSM_HEAD_DIM = 64
SSM_GROUPS = 8
SSM_STATE = 128
SSM_CHUNK = 128
TOP_K = 4
SWIGLU_ALPHA = 1.702
SWIGLU_LIMIT = 7.0
LN_EPS = 1e-5
RMS_EPS = 1e-5

CONV_HALO = 32
CONV_ROWS = 64
CONV_LANES = 128
MOE_ROWS = 512
DMA_UNROLL = 8
DMA_QUEUES = 2
SEG_ROWS = 16
LANES = 128
VMEM_LIMIT = 56 * 1024 * 1024


def _cparams(*sem):
    return pltpu.CompilerParams(dimension_semantics=tuple(sem), vmem_limit_bytes=VMEM_LIMIT)


def _pick(n, prefs):
    for p in prefs:
        if n % p == 0:
            return p
    raise ValueError(f"no tile for {n} in {prefs}")


def _layer_norm(y, g, b):
    mu = jnp.mean(y, axis=-1, keepdims=True)
    yc = y - mu
    var = jnp.mean(yc * yc, axis=-1, keepdims=True)
    return yc * lax.rsqrt(var + LN_EPS) * g + b


def _dot_3pass(a, b, dims):
    a_hi = a.astype(BF16)
    b_hi = b.astype(BF16)
    a_lo = (a - a_hi.astype(F32)).astype(BF16)
    b_lo = (b - b_hi.astype(F32)).astype(BF16)
    dot = lambda u, v: lax.dot_general(u, v, (dims, ((), ())), preferred_element_type=F32)
    return dot(a_hi, b_hi) + (dot(a_hi, b_lo) + dot(a_lo, b_hi))


def _silu(x):
    h = 0.5 * x
    return h + h * jnp.tanh(h)


def _sigmoid(x):
    return 1.0 / (1.0 + jnp.exp(-x))


def _softplus(x):
    return jnp.maximum(x, 0.0) + jnp.log(1.0 + jnp.exp(-jnp.abs(x)))


def _glu_kernel(x_ref, wa_ref, wg_ref, ba_ref, bg_ref, o_ref, *, pad):
    j = pl.program_id(1)
    x = x_ref[0].astype(BF16)
    a = jnp.dot(x, wa_ref[...], preferred_element_type=F32) + ba_ref[...]
    g = jnp.dot(x, wg_ref[...], preferred_element_type=F32) + bg_ref[...]
    u = a * _sigmoid(g)
    row = lax.broadcasted_iota(I32, u.shape, 0)
    valid = jnp.logical_or(j > 0, row >= pad)
    o_ref[0] = jnp.where(valid, u, 0.0)


def _glu(h, wa, wg, ba, bg, pad, tl):
    bsz, lp, d = h.shape
    c = wa.shape[1]
    return pl.pallas_call(
        functools.partial(_glu_kernel, pad=pad),
        grid=(bsz, lp // tl),
        in_specs=[
            pl.BlockSpec((1, tl, d), lambda b, j: (b, j, 0)),
            pl.BlockSpec((d, c), lambda b, j: (0, 0)),
            pl.BlockSpec((d, c), lambda b, j: (0, 0)),
            pl.BlockSpec((1, c), lambda b, j: (0, 0)),
            pl.BlockSpec((1, c), lambda b, j: (0, 0)),
        ],
        out_specs=pl.BlockSpec((1, tl, c), lambda b, j: (b, j, 0)),
        out_shape=jax.ShapeDtypeStruct((bsz, lp, c), F32),
        compiler_params=_cparams("parallel", "parallel"),
        name="conf_glu",
    )(h, wa, wg, ba, bg)


def _conv_mix_kernel(ucur_ref, uprev_ref, h_ref, dw_ref, dwb_ref, g1_ref, b1_ref, w2_ref, b2_ref,
                     g2_ref, bb2_ref, o_ref, win_ref, act_ref, *, taps, alpha):
    j = pl.program_id(1)
    tl = ucur_ref.shape[1]

    @pl.when(j == 0)
    def _():
        win_ref[0:CONV_HALO, :] = jnp.zeros((CONV_HALO, win_ref.shape[1]), F32)

    @pl.when(j > 0)
    def _():
        win_ref[0:CONV_HALO, :] = uprev_ref[0, tl - CONV_HALO:tl, :]

    win_ref[CONV_HALO:CONV_HALO + tl, :] = ucur_ref[0]

    def chunk(c, carry):
        base = pl.multiple_of(c * CONV_ROWS, CONV_ROWS)
        n = CONV_HALO + CONV_ROWS
        parts = []
        for lc in range(win_ref.shape[1] // CONV_LANES):
            sl = slice(lc * CONV_LANES, (lc + 1) * CONV_LANES)
            win = win_ref[pl.ds(base, n), sl]
            acc = jnp.zeros((CONV_ROWS, CONV_LANES), F32) + dwb_ref[:, sl]
            for s in range(8):
                ws = win if s == 0 else pltpu.roll(win, n - s, 0)
                for k in range(taps):
                    off = CONV_HALO + k - (taps - 1)
                    if off % 8 == s:
                        acc = acc + ws[off - s:off - s + CONV_ROWS, :] * dw_ref[k:k + 1, sl]
            parts.append(acc)
        y = _layer_norm(jnp.concatenate(parts, axis=1), g1_ref[...], b1_ref[...])
        y = _silu(y)
        act_ref[pl.ds(base, CONV_ROWS), :] = y.astype(BF16)
        return carry

    lax.fori_loop(0, tl // CONV_ROWS, chunk, 0, unroll=4)
    mix = jnp.dot(act_ref[...], w2_ref[...], preferred_element_type=F32) + b2_ref[...]
    y = alpha * h_ref[0] + mix
    o_ref[0] = _layer_norm(y, g2_ref[...], bb2_ref[...])


def _conv_mix(u, h, dw, dwb, g1, b1, w2, b2, g2, bb2, alpha, tl):
    bsz, lp, c = u.shape
    d = h.shape[2]
    taps = dw.shape[0]
    assert taps - 1 <= CONV_HALO and tl % CONV_ROWS == 0
    vec = lambda n: pl.BlockSpec((1, n), lambda b, j: (0, 0))
    return pl.pallas_call(
        functools.partial(_conv_mix_kernel, taps=taps, alpha=alpha),
        grid=(bsz, lp // tl),
        in_specs=[
            pl.BlockSpec((1, tl, c), lambda b, j: (b, j, 0)),
            pl.BlockSpec((1, tl, c), lambda b, j: (b, jnp.maximum(j - 1, 0), 0)),
            pl.BlockSpec((1, tl, d), lambda b, j: (b, j, 0)),
            pl.BlockSpec((taps, c), lambda b, j: (0, 0)),
            vec(c), vec(c), vec(c),
            pl.BlockSpec((c, d), lambda b, j: (0, 0)),
            vec(d), vec(d), vec(d),
        ],
        out_specs=pl.BlockSpec((1, tl, d), lambda b, j: (b, j, 0)),
        out_shape=jax.ShapeDtypeStruct((bsz, lp, d), F32),
        scratch_shapes=[pltpu.VMEM((CONV_HALO + tl, c), F32), pltpu.VMEM((tl, c), BF16)],
        compiler_params=_cparams("parallel", "parallel"),
        name="conf_conv_mix",
    )(u, u, h, dw, dwb, g1, b1, w2, b2, g2, bb2)


def _mm_kernel(x_ref, w_ref, o_ref):
    o_ref[...] = jnp.dot(x_ref[...].astype(BF16), w_ref[...], preferred_element_type=F32).astype(o_ref.dtype)


def _mm(x, w, tm, tn, out_dtype=F32):
    m, k = x.shape
    n = w.shape[1]
    return pl.pallas_call(
        _mm_kernel,
        grid=(n // tn, m // tm),
        in_specs=[pl.BlockSpec((tm, k), lambda jn, im: (im, 0)),
                  pl.BlockSpec((k, tn), lambda jn, im: (0, jn))],
        out_specs=pl.BlockSpec((tm, tn), lambda jn, im: (im, jn)),
        out_shape=jax.ShapeDtypeStruct((m, n), out_dtype),
        compiler_params=_cparams("parallel", "parallel"),
        name="mm",
    )(x, w)


def _mm_resid_ln_kernel(x_ref, w_ref, h_ref, g_ref, b_ref, o_ref, *, alpha):
    mix = jnp.dot(x_ref[...].astype(BF16), w_ref[...], preferred_element_type=F32)
    o_ref[...] = _layer_norm(alpha * h_ref[...] + mix, g_ref[...], b_ref[...])


def _mm_resid_ln(x, w, h, g, b, alpha, tm):
    m, k = x.shape
    d = w.shape[1]
    return pl.pallas_call(
        functools.partial(_mm_resid_ln_kernel, alpha=alpha),
        grid=(m // tm,),
        in_specs=[pl.BlockSpec((tm, k), lambda i: (i, 0)),
                  pl.BlockSpec((k, d), lambda i: (0, 0)),
                  pl.BlockSpec((tm, d), lambda i: (i, 0)),
                  pl.BlockSpec((1, d), lambda i: (0, 0)),
                  pl.BlockSpec((1, d), lambda i: (0, 0))],
        out_specs=pl.BlockSpec((tm, d), lambda i: (i, 0)),
        out_shape=jax.ShapeDtypeStruct((m, d), F32),
        compiler_params=_cparams("parallel"),
        name="mm_resid_ln",
    )(x, w, h, g, b)


def _dt_kernel(x_ref, wc_ref, bc_ref, ac_ref, dtc_ref, csc_ref, csr_ref, *, pad):
    c = pl.program_id(1)
    x = x_ref[0]
    n = x.shape[0]
    nh = csr_ref.shape[2]
    raw_c = _dot_3pass(x, wc_ref[...], ((1,), (0,)))
    row_c = lax.broadcasted_iota(I32, raw_c.shape, 0)
    dt_c = jnp.where(jnp.logical_and(c == 0, row_c < pad), 0.0, _softplus(raw_c + bc_ref[...]))
    li = lax.broadcasted_iota(I32, (n, n), 0)
    si = lax.broadcasted_iota(I32, (n, n), 1)
    tri = jnp.where(si <= li, 1.0, 0.0)
    cs_c = jnp.dot(tri, dt_c * ac_ref[...], preferred_element_type=F32, precision=HIGHEST)
    dtc_ref[0] = dt_c
    csc_ref[0] = cs_c
    csr_ref[0, 0] = cs_c.T[:nh, :]


def _dt_prep(h, w_dt, dt_bias, a_neg, pad):
    bsz, lp, d = h.shape
    nh = w_dt.shape[1]
    nc = lp // SSM_CHUNK
    lanes = 128
    wc = jnp.zeros((d, lanes), F32).at[:, :nh].set(w_dt)
    bc = jnp.zeros((1, lanes), F32).at[0, :nh].set(dt_bias)
    ac = jnp.zeros((1, lanes), F32).at[0, :nh].set(a_neg)
    return pl.pallas_call(
        functools.partial(_dt_kernel, pad=pad),
        grid=(bsz, nc),
        in_specs=[
            pl.BlockSpec((1, SSM_CHUNK, d), lambda b, c: (b, c, 0)),
            pl.BlockSpec((d, lanes), lambda b, c: (0, 0)),
            pl.BlockSpec((1, lanes), lambda b, c: (0, 0)),
            pl.BlockSpec((1, lanes), lambda b, c: (0, 0)),
        ],
        out_specs=[
            pl.BlockSpec((1, SSM_CHUNK, lanes), lambda b, c: (b, c, 0)),
            pl.BlockSpec((1, SSM_CHUNK, lanes), lambda b, c: (b, c, 0)),
            pl.BlockSpec((1, 1, nh, SSM_CHUNK), lambda b, c: (b, c, 0, 0)),
        ],
        out_shape=[
            jax.ShapeDtypeStruct((bsz, lp, lanes), F32),
            jax.ShapeDtypeStruct((bsz, lp, lanes), F32),
            jax.ShapeDtypeStruct((bsz, nc, nh, SSM_CHUNK), F32),
        ],
        compiler_params=_cparams("parallel", "parallel"),
        name="ssm_dt",
    )(h, wc, bc, ac)


def _ssd_kernel(xs_ref, bm_ref, cm_ref, z_ref, dtc_ref, csc_ref, csr_ref,
                cwx_ref, cwb_ref, cwc_ref, cbx_ref, cbb_ref, cbc_ref, dsk_ref, ng_ref,
                o_ref, state_ref, carx_ref, carb_ref, carc_ref, *, pad, hpg, taps):
    g = pl.program_id(1)
    q = SSM_CHUNK
    hd = SSM_HEAD_DIM
    gw = hpg * hd
    n_chunks = xs_ref.shape[1] // q

    state_ref[...] = jnp.zeros(state_ref.shape, F32)
    carx_ref[...] = jnp.zeros(carx_ref.shape, F32)
    carb_ref[...] = jnp.zeros(carb_ref.shape, F32)
    carc_ref[...] = jnp.zeros(carc_ref.shape, F32)

    hrow = lax.broadcasted_iota(I32, (128, gw), 0)
    hcol = lax.broadcasted_iota(I32, (128, gw), 1) // hd
    sel = jnp.where(hrow == g * hpg + hcol, 1.0, 0.0).astype(BF16)

    def select_heads(v):
        hi = v.astype(BF16)
        r1 = v - hi.astype(F32)
        mid = r1.astype(BF16)
        lo = (r1 - mid.astype(F32)).astype(BF16)
        out = jnp.dot(hi, sel, preferred_element_type=F32)
        out = out + jnp.dot(mid, sel, preferred_element_type=F32)
        return out + jnp.dot(lo, sel, preferred_element_type=F32)

    lane_head = lax.broadcasted_iota(I32, (q, gw), 1) // hd
    head_mask = [jnp.where(lane_head == r, 1.0, 0.0).astype(BF16) for r in range(hpg)]
    li = lax.broadcasted_iota(I32, (q, q), 0)
    si = lax.broadcasted_iota(I32, (q, q), 1)
    causal = si <= li
    row1 = lax.broadcasted_iota(I32, (q, 1), 0)

    def conv_act(raw, car_ref, w_ref, b_ref, valid):
        if valid is not None:
            raw = jnp.where(valid, raw, 0.0)
        win = jnp.concatenate([car_ref[...], raw], axis=0)
        acc = win * w_ref[0:1, :]
        for k in range(1, taps):
            acc = pltpu.roll(acc, 1, 0) + win * w_ref[k:k + 1, :]
        acc = acc[8:8 + q, :] + b_ref[...]
        car_ref[...] = raw[q - 8:q, :]
        act = _silu(acc)
        return act if valid is None else jnp.where(valid, act, 0.0)

    def chunk(c, first):
        r0 = 0 if first else pl.multiple_of(c * q, q)
        valid = (row1 >= pad) if first else None
        xs = conv_act(xs_ref[0, pl.ds(r0, q), :], carx_ref, cwx_ref, cbx_ref, valid)
        bm = conv_act(bm_ref[0, pl.ds(r0, q), :], carb_ref, cwb_ref, cbb_ref, valid)
        cm = conv_act(cm_ref[0, pl.ds(r0, q), :], carc_ref, cwc_ref, cbc_ref, valid)

        dt_e = select_heads(dtc_ref[0, pl.ds(r0, q), :])
        cs_e = select_heads(csc_ref[0, pl.ds(r0, q), :])
        cs_r = csr_ref[0, 0, c]
        a_last = cs_e[q - 1:q, :]

        x_dt = xs * dt_e
        xw = (x_dt * jnp.exp(a_last - cs_e)).astype(BF16)
        bm16 = bm.astype(BF16)
        cm16 = cm.astype(BF16)
        cb = lax.dot_general(cm16, bm16, (((1,), (1,)), ((), ())), preferred_element_type=F32)

        m_parts = []
        x_parts = []
        x_dt16 = x_dt.astype(BF16)
        for r in range(hpg):
            col = cs_e[:, r * hd:r * hd + 1]
            seg = col - cs_r[r:r + 1, :]
            decay = jnp.exp(jnp.where(causal, seg, -jnp.inf))
            m_parts.append((cb * decay).astype(BF16))
            x_parts.append(x_dt16 * head_mask[r])
        m_cat = jnp.concatenate(m_parts, axis=1)
        x_bd = jnp.concatenate(x_parts, axis=0)
        y = jnp.dot(m_cat, x_bd, preferred_element_type=F32)

        st = state_ref[...]
        y = y + jnp.dot(cm16, st.astype(BF16), preferred_element_type=F32) * jnp.exp(cs_e)
        new = jnp.dot(bm.T.astype(BF16), xw, preferred_element_type=F32)
        state_ref[...] = st * jnp.exp(a_last) + new
        y = y + dsk_ref[...] * xs

        z = z_ref[0, pl.ds(r0, q), :]
        y = y * _silu(z)
        y = y * lax.rsqrt(jnp.mean(y * y, axis=-1, keepdims=True) + RMS_EPS) * ng_ref[...]
        o_ref[0, pl.ds(r0, q), :] = y.astype(o_ref.dtype)

    def later_chunk(c, carry):
        chunk(c, False)
        return carry

    chunk(0, True)
    lax.fori_loop(1, n_chunks, later_chunk, 0, unroll=4)


def _ssd(proj, dtc, csc, csr, conv_w, conv_b, d_exp, norm_g, pad, d_inner):
    bsz, lp, _ = proj.shape
    g = SSM_GROUPS
    n = SSM_STATE
    gw = d_inner // g
    hpg = gw // SSM_HEAD_DIM
    assert gw % 128 == 0 and n == 128 and dtc.shape[2] == 128
    taps = conv_w.shape[0]
    nc = lp // SSM_CHUNK
    zb = 0
    xb = d_inner // gw
    bb = 2 * d_inner // n
    cb_ = (2 * d_inner + g * n) // n
    cxb = 0
    cbb = d_inner // n
    ccb = (d_inner + g * n) // n
    seq = lambda w, off: pl.BlockSpec((1, lp, w), lambda b, j, off=off: (b, 0, off + j))
    cw = lambda w, off: pl.BlockSpec((taps, w), lambda b, j, off=off: (0, off + j))
    cbv = lambda w, off: pl.BlockSpec((1, w), lambda b, j, off=off: (0, off + j))
    return pl.pallas_call(
        functools.partial(_ssd_kernel, pad=pad, hpg=hpg, taps=taps),
        grid=(bsz, g),
        in_specs=[
            seq(gw, xb), seq(n, bb), seq(n, cb_), seq(gw, zb),
            pl.BlockSpec((1, lp, 128), lambda b, j: (b, 0, 0)),
            pl.BlockSpec((1, lp, 128), lambda b, j: (b, 0, 0)),
            pl.BlockSpec((1, 1, nc, hpg, SSM_CHUNK), lambda b, j: (b, j, 0, 0, 0)),
            cw(gw, cxb), cw(n, cbb), cw(n, ccb),
            cbv(gw, cxb), cbv(n, cbb), cbv(n, ccb),
            pl.BlockSpec((1, gw), lambda b, j: (0, j)),
            pl.BlockSpec((1, gw), lambda b, j: (0, j)),
        ],
        out_specs=pl.BlockSpec((1, lp, gw), lambda b, j: (b, 0, j)),
        out_shape=jax.ShapeDtypeStruct((bsz, lp, d_inner), BF16),
        scratch_shapes=[pltpu.VMEM((n, gw), F32), pltpu.VMEM((8, gw), F32),
                        pltpu.VMEM((8, n), F32), pltpu.VMEM((8, n), F32)],
        compiler_params=_cparams("parallel", "parallel"),
        name="ssm_ssd",
    )(proj, proj, proj, proj, dtc, csc, csr, conv_w, conv_w, conv_w, conv_b, conv_b, conv_b, d_exp, norm_g)


def _pad_chunk(tile, q, tm, chunks_per_row):
    return lax.rem(tile * (tm // SSM_CHUNK) + q, chunks_per_row) == 0


def _for_real_groups(tile, tm, chunks_per_row, pad, body):
    per_chunk = SSM_CHUNK // DMA_UNROLL
    for q in range(tm // SSM_CHUNK):
        skip = jnp.where(_pad_chunk(tile, q, tm, chunks_per_row), pad // DMA_UNROLL, 0)
        lax.fori_loop(q * per_chunk + skip, (q + 1) * per_chunk, body, 0)


def _router_kernel(x_ref, wr_ref, br_ref, e_ref, gate_ref, rank_ref, lpos_ref, tab_ref, cnt_ref, carry_ref,
                   *, chunks_per_row, pad):
    i = pl.program_id(0)

    @pl.when(i == 0)
    def _():
        carry_ref[...] = jnp.zeros(carry_ref.shape, F32)

    x = x_ref[...]
    tm = x.shape[0]
    ne = wr_ref.shape[0]
    logits = _dot_3pass(wr_ref[...], x, ((1,), (1,))) + br_ref[...]
    ids = lax.broadcasted_iota(I32, (ne, tm), 0)
    vals = logits
    top_v, top_e, sels = [], [], []
    for _ in range(TOP_K):
        m = jnp.max(vals, axis=0, keepdims=True)
        idx = jnp.min(jnp.where(vals == m, ids, ne), axis=0, keepdims=True)
        sel = ids == idx
        top_v.append(m)
        top_e.append(idx)
        sels.append(sel)
        vals = jnp.where(sel, -jnp.inf, vals)
    ex = [jnp.exp(v - top_v[0]) for v in top_v]
    den = ex[0]
    for t in ex[1:]:
        den = den + t
    e_ref[...] = jnp.concatenate(top_e, axis=0)
    gate_ref[...] = jnp.concatenate([t / den for t in ex], axis=0)

    onehot = sels[0].astype(F32)
    for s in sels[1:]:
        onehot = onehot + s.astype(F32)
    lane = lax.broadcasted_iota(I32, (1, tm), 1)
    real = jnp.ones((1, tm), F32)
    for q in range(tm // SSM_CHUNK):
        inert = jnp.logical_and(_pad_chunk(i, q, tm, chunks_per_row),
                                jnp.logical_and(lane >= q * SSM_CHUNK, lane < q * SSM_CHUNK + pad))
        real = jnp.where(inert, 0.0, real)
    onehot = onehot * real
    ji = lax.broadcasted_iota(I32, (tm, tm), 0)
    ti = lax.broadcasted_iota(I32, (tm, tm), 1)
    before = (ji < ti).astype(BF16)
    prefix = jnp.dot(onehot.astype(BF16), before, preferred_element_type=F32)
    tile_cnt = jnp.sum(onehot, axis=1, keepdims=True) + jnp.zeros((ne, LANES), F32)
    pieces = jnp.floor((tile_cnt + (SEG_ROWS - 1.0)) * (1.0 / SEG_ROWS))
    ei = lax.broadcasted_iota(I32, (ne, ne), 0)
    ej = lax.broadcasted_iota(I32, (ne, ne), 1)
    below = jnp.where(ej < ei, 1.0, 0.0).astype(BF16)
    lstart = SEG_ROWS * jnp.dot(below, pieces.astype(BF16), preferred_element_type=F32)
    local = prefix + lstart[:, 0:1]
    lpos = [jnp.sum(jnp.where(s, local, 0.0), axis=0, keepdims=True) for s in sels]
    spare = float(TOP_K * tm + ne * SEG_ROWS - 1)
    lpos_ref[...] = jnp.where(real > 0.0, jnp.concatenate(lpos, axis=0), spare).astype(I32)
    tab_ref[0, 0] = tile_cnt.astype(I32)
    tab_ref[0, 1] = lstart.astype(I32)
    tab_ref[0, 2] = carry_ref[...].astype(I32)

    prefix = prefix + carry_ref[:, 0:1]
    ranks = [jnp.sum(jnp.where(s, prefix, 0.0), axis=0, keepdims=True) for s in sels]
    rank_ref[...] = jnp.concatenate(ranks, axis=0).astype(I32)
    carry_ref[...] = carry_ref[...] + tile_cnt
    cnt_ref[...] = carry_ref[...].astype(I32)


def _router(t, w_r, b_r, tm, chunks_per_row, pad):
    nt, d = t.shape
    ne = w_r.shape[1]
    assert tm % SSM_CHUNK == 0
    return pl.pallas_call(
        functools.partial(_router_kernel, chunks_per_row=chunks_per_row, pad=pad),
        grid=(nt // tm,),
        in_specs=[pl.BlockSpec((tm, d), lambda i: (i, 0)),
                  pl.BlockSpec((ne, d), lambda i: (0, 0)),
                  pl.BlockSpec((ne, 1), lambda i: (0, 0))],
        out_specs=[pl.BlockSpec((TOP_K, tm), lambda i: (0, i)),
                   pl.BlockSpec((TOP_K, tm), lambda i: (0, i)),
                   pl.BlockSpec((TOP_K, tm), lambda i: (0, i)),
                   pl.BlockSpec((TOP_K, tm), lambda i: (0, i)),
                   pl.BlockSpec((1, 3, ne, LANES), lambda i: (i, 0, 0, 0)),
                   pl.BlockSpec((ne, 128), lambda i: (0, 0))],
        out_shape=[jax.ShapeDtypeStruct((TOP_K, nt), I32),
                   jax.ShapeDtypeStruct((TOP_K, nt), F32),
                   jax.ShapeDtypeStruct((TOP_K, nt), I32),
                   jax.ShapeDtypeStruct((TOP_K, nt), I32),
                   jax.ShapeDtypeStruct((nt // tm, 3, ne, LANES), I32),
                   jax.ShapeDtypeStruct((ne, 128), I32)],
        scratch_shapes=[pltpu.VMEM((ne, 128), F32)],
        compiler_params=_cparams("arbitrary"),
        name="moe_router",
    )(t, w_r.T, b_r[:, None])


def _dispatch_kernel(pend_ref, padded_ref, nused_ref, dest_hbm, t_ref, xbuf_hbm,
                     idx_ref, rows_ref, zero_ref, isem, sem, zsem, *, tm, nseg, n_blk, chunks_per_row, pad):
    i = pl.program_id(0)
    n = TOP_K * tm
    blk = MOE_ROWS * nseg

    @pl.when(i == 0)
    def _():
        zero_ref[...] = jnp.zeros(zero_ref.shape, zero_ref.dtype)

        def group_fill(e):
            return pltpu.make_async_copy(
                zero_ref, xbuf_hbm.at[pl.ds(pl.multiple_of((pend_ref[e] - MOE_ROWS) * nseg, blk), blk), :], zsem)

        def tail_fill(j):
            return pltpu.make_async_copy(zero_ref, xbuf_hbm.at[pl.ds(pl.multiple_of(j * blk, blk), blk), :], zsem)

        def start_group(e, carry):
            @pl.when(padded_ref[e] > 0)
            def _():
                group_fill(e).start()
            return carry

        def wait_group(e, carry):
            @pl.when(padded_ref[e] > 0)
            def _():
                group_fill(e).wait()
            return carry

        def start_tail(j, carry):
            tail_fill(j).start()
            return carry

        def wait_tail(j, carry):
            tail_fill(j).wait()
            return carry

        n_exp = pend_ref.shape[0]
        lax.fori_loop(0, n_exp, start_group, 0)
        lax.fori_loop(nused_ref[0], n_blk, start_tail, 0)
        lax.fori_loop(0, n_exp, wait_group, 0)
        lax.fori_loop(nused_ref[0], n_blk, wait_tail, 0)

    last = pl.num_programs(0) - 1

    def row_copy(slot, k, r):
        src = rows_ref.at[slot, pl.ds(pl.multiple_of(r * nseg, nseg), nseg), :]
        dst = xbuf_hbm.at[pl.ds(pl.multiple_of(idx_ref[slot * n + k * tm + r] * nseg, nseg), nseg), :]
        return pltpu.make_async_copy(src, dst, sem.at[slot])

    def drain_slot(slot, tile):
        def drain(grp, carry):
            for u in range(DMA_UNROLL):
                for k in range(TOP_K):
                    row_copy(slot, k, grp * DMA_UNROLL + u).wait()
            return carry

        _for_real_groups(tile, tm, chunks_per_row, pad, drain)

    for slot in range(2):
        @pl.when(i % 2 == slot)
        def _():
            @pl.when(i >= 2)
            def _():
                drain_slot(slot, i - 2)

            cp = pltpu.make_async_copy(dest_hbm.at[pl.ds(pl.multiple_of(i * n, n), n)],
                                       idx_ref.at[pl.ds(slot * n, n)], isem)
            cp.start()
            x = t_ref[...]
            for c in range(nseg):
                rows_ref[slot, pl.ds(c, tm, stride=nseg), :] = x[:, c * LANES:(c + 1) * LANES]
            cp.wait()

            def issue(grp, carry):
                for u in range(DMA_UNROLL):
                    for k in range(TOP_K):
                        row_copy(slot, k, grp * DMA_UNROLL + u).start(priority=k % DMA_QUEUES)
                return carry

            _for_real_groups(i, tm, chunks_per_row, pad, issue)

            @pl.when(i == last)
            def _():
                drain_slot(slot, i)

                @pl.when(i >= 1)
                def _():
                    drain_slot(1 - slot, i - 1)


def _dispatch(t, dest_tiles, pend, padded, n_used, n_rows, tm, chunks_per_row, pad):
    nt, d = t.shape
    nseg = d // LANES
    return pl.pallas_call(
        functools.partial(_dispatch_kernel, tm=tm, nseg=nseg, n_blk=n_rows // MOE_ROWS,
                          chunks_per_row=chunks_per_row, pad=pad),
        grid_spec=pltpu.PrefetchScalarGridSpec(
            num_scalar_prefetch=3,
            grid=(nt // tm,),
            in_specs=[pl.BlockSpec(memory_space=pl.ANY),
                      pl.BlockSpec((tm, d), lambda i, *_: (i, 0))],
            out_specs=pl.BlockSpec(memory_space=pl.ANY),
            scratch_shapes=[pltpu.SMEM((2 * TOP_K * tm,), I32), pltpu.VMEM((2, tm * nseg, LANES), t.dtype),
                            pltpu.VMEM((MOE_ROWS * nseg, LANES), t.dtype),
                            pltpu.SemaphoreType.DMA(()), pltpu.SemaphoreType.DMA((2,)),
                            pltpu.SemaphoreType.DMA(())],
        ),
        out_shape=jax.ShapeDtypeStruct((n_rows * nseg, LANES), t.dtype),
        compiler_params=pltpu.CompilerParams(dimension_semantics=("arbitrary",), has_side_effects=True,
                                             vmem_limit_bytes=VMEM_LIMIT),
        name="moe_dispatch",
    )(pend, padded, n_used, dest_tiles, t)


def _ffn_kernel(blk_e_ref, nused_ref, x_ref, w1_ref, b1g_ref, b1l_ref, w2_ref, b2_ref, y_ref,
                w1g_ref, w1l_ref, w2c_ref, *, nseg):
    i = pl.program_id(0)
    rows = x_ref.shape[0] // nseg
    e = blk_e_ref[i]
    e_prev = blk_e_ref[jnp.maximum(i - 1, 0)]

    @pl.when(jnp.logical_or(i == 0, e != e_prev))
    def _():
        w = 2 * LANES
        r = lax.broadcasted_iota(I32, (w, w), 0)
        c = lax.broadcasted_iota(I32, (w, w), 1)
        src_col = jnp.where(c < LANES, 2 * c, 2 * (c - LANES) + 1)
        perm = jnp.where(r == src_col, 1.0, 0.0).astype(BF16)
        for cc in range(w1_ref.shape[3] // w):
            chunk = w1_ref[0, 0, :, cc * w:(cc + 1) * w].astype(BF16)
            res = jnp.dot(chunk, perm, preferred_element_type=F32).astype(BF16)
            w1g_ref[:, cc * LANES:(cc + 1) * LANES] = res[:, :LANES]
            w1l_ref[:, cc * LANES:(cc + 1) * LANES] = res[:, LANES:]
        w2c_ref[...] = w2_ref[0, 0].astype(BF16)

    @pl.when(i < nused_ref[0])
    def _():
        x = jnp.concatenate([x_ref[pl.ds(c, rows, stride=nseg), :] for c in range(nseg)], axis=1).astype(BF16)
        hg = jnp.dot(x, w1g_ref[...], preferred_element_type=F32) + b1g_ref[0]
        hl = jnp.dot(x, w1l_ref[...], preferred_element_type=F32) + b1l_ref[0]
        xg = jnp.minimum(hg, SWIGLU_LIMIT)
        xl = jnp.clip(hl, -SWIGLU_LIMIT, SWIGLU_LIMIT)
        act = xg * _sigmoid(SWIGLU_ALPHA * xg) * (xl + 1.0)
        y = jnp.dot(act.astype(BF16), w2c_ref[...], preferred_element_type=F32) + b2_ref[0]
        for c in range(nseg):
            y_ref[pl.ds(c, rows, stride=nseg), :] = y[:, c * LANES:(c + 1) * LANES]

    @pl.when(i >= nused_ref[0])
    def _():
        y_ref[...] = jnp.zeros(y_ref.shape, F32)


def _ffn(x_buf, blk_e, n_used, layer, w1, b1g, b1l, w2, b2):
    _, ne, d, f2 = w1.shape
    f = f2 // 2
    nseg = d // LANES
    n_rows = x_buf.shape[0] // nseg
    n_blk = n_rows // MOE_ROWS
    assert f2 % (2 * LANES) == 0
    wspec = lambda a, b: pl.BlockSpec((1, a, b), lambda i, be, nu: (be[i], 0, 0))
    wfull = lambda a, b: pl.BlockSpec((1, 1, a, b), lambda i, be, nu: (layer, be[i], 0, 0))
    return pl.pallas_call(
        functools.partial(_ffn_kernel, nseg=nseg),
        grid_spec=pltpu.PrefetchScalarGridSpec(
            num_scalar_prefetch=2,
            grid=(n_blk,),
            in_specs=[pl.BlockSpec((MOE_ROWS * nseg, LANES), lambda i, be, nu: (i, 0)),
                      wfull(d, f2), wspec(1, f), wspec(1, f), wfull(f, d), wspec(1, d)],
            out_specs=pl.BlockSpec((MOE_ROWS * nseg, LANES), lambda i, be, nu: (i, 0)),
            scratch_shapes=[pltpu.VMEM((d, f), BF16), pltpu.VMEM((d, f), BF16), pltpu.VMEM((f, d), BF16)],
        ),
        out_shape=jax.ShapeDtypeStruct((n_rows * nseg, LANES), F32),
        compiler_params=_cparams("arbitrary"),
        name="moe_ffn",
    )(blk_e, n_used, x_buf, w1, b1g, b1l, w2, b2)


def _combine_kernel(cnt_ref, lst_ref, dst_ref, pcs_ref, lpos_hbm, gates_hbm, ybuf_hbm, h_ref, g_ref, b_ref, o_ref,
                    idx_ref, gsm_ref, sort_ref, sum_ref, isem, sem, *, tm, nseg, alpha, n_exp):
    i = pl.program_id(0)
    n = TOP_K * tm
    piece = SEG_ROWS * nseg

    def start_runs(slot, tile):
        def per_expert(e, carry):
            cnt = cnt_ref[tile * n_exp + e]
            dst0 = lst_ref[tile * n_exp + e]
            src0 = dst_ref[tile * n_exp + e]

            def piece_copy(j, c2):
                pltpu.make_async_copy(
                    ybuf_hbm.at[pl.ds(pl.multiple_of((src0 + j * SEG_ROWS) * nseg, nseg), piece), :],
                    sort_ref.at[slot, pl.ds(pl.multiple_of((dst0 + j * SEG_ROWS) * nseg, nseg), piece), :],
                    sem.at[slot]).start()
                return c2

            lax.fori_loop(0, (cnt + SEG_ROWS - 1) // SEG_ROWS, piece_copy, 0)
            return carry

        lax.fori_loop(0, n_exp, per_expert, 0)

    def wait_tile(slot, tile):
        piece_wait = pltpu.make_async_copy(ybuf_hbm.at[pl.ds(0, piece), :], sort_ref.at[slot, pl.ds(0, piece), :],
                                           sem.at[slot])

        def wait_piece(j, carry):
            piece_wait.wait()
            return carry

        lax.fori_loop(0, pcs_ref[tile], wait_piece, 0)

    @pl.when(i == 0)
    def _():
        sort_ref[...] = jnp.zeros(sort_ref.shape, F32)
        start_runs(0, 0)

    for cur in range(2):
        @pl.when(i % 2 == cur)
        def _():
            @pl.when(i + 1 < pl.num_programs(0))
            def _():
                start_runs(1 - cur, i + 1)

            c1 = pltpu.make_async_copy(lpos_hbm.at[pl.ds(pl.multiple_of(i * n, n), n)], idx_ref, isem.at[0])
            c2 = pltpu.make_async_copy(gates_hbm.at[pl.ds(pl.multiple_of(i * n, n), n)], gsm_ref, isem.at[1])
            c1.start()
            c2.start()
            wait_tile(cur, i)
            c1.wait()
            c2.wait()

            def reduce(grp, carry):
                for u in range(DMA_UNROLL):
                    r = grp * DMA_UNROLL + u
                    acc = None
                    for k in range(TOP_K):
                        slab = sort_ref[cur, pl.ds(pl.multiple_of(idx_ref[k * tm + r] * nseg, nseg), nseg), :]
                        term = gsm_ref[k * tm + r] * slab
                        acc = term if acc is None else acc + term
                    sum_ref[pl.ds(pl.multiple_of(r * nseg, nseg), nseg), :] = acc
                return carry

            lax.fori_loop(0, tm // DMA_UNROLL, reduce, 0)

    ffn = jnp.concatenate([sum_ref[pl.ds(c, tm, stride=nseg), :] for c in range(nseg)], axis=1)
    o_ref[...] = _layer_norm(alpha * h_ref[...] + ffn, g_ref[...], b_ref[...])


def _combine(y_buf, lpos_tiles, gate_tiles, seg_cnt, seg_lst, seg_dst, tile_pieces, h, g, b, alpha, tm, n_exp):
    nt, d = h.shape
    nseg = d // LANES
    sort_rows = TOP_K * tm + n_exp * SEG_ROWS
    return pl.pallas_call(
        functools.partial(_combine_kernel, tm=tm, nseg=nseg, alpha=alpha, n_exp=n_exp),
        grid_spec=pltpu.PrefetchScalarGridSpec(
            num_scalar_prefetch=4,
            grid=(nt // tm,),
            in_specs=[pl.BlockSpec(memory_space=pl.ANY),
                      pl.BlockSpec(memory_space=pl.ANY),
                      pl.BlockSpec(memory_space=pl.ANY),
                      pl.BlockSpec((tm, d), lambda i, *_: (i, 0)),
                      pl.BlockSpec((1, d), lambda i, *_: (0, 0)),
                      pl.BlockSpec((1, d), lambda i, *_: (0, 0))],
            out_specs=pl.BlockSpec((tm, d), lambda i, *_: (i, 0)),
            scratch_shapes=[pltpu.SMEM((TOP_K * tm,), I32), pltpu.SMEM((TOP_K * tm,), F32),
                            pltpu.VMEM((2, sort_rows * nseg, LANES), F32), pltpu.VMEM((tm * nseg, LANES), F32),
                            pltpu.SemaphoreType.DMA((2,)), pltpu.SemaphoreType.DMA((2,))],
        ),
        out_shape=jax.ShapeDtypeStruct((nt, d), F32),
        compiler_params=_cparams("arbitrary"),
        name="moe_combine",
    )(seg_cnt, seg_lst, seg_dst, tile_pieces, lpos_tiles, gate_tiles, y_buf, h, g, b)


def _moe_layer(t, layer, lp, pad, w_r, b_r, w1, b1, w2, b2, ln_g, ln_b, alpha):
    nt, d = t.shape
    ne = w_r.shape[1]
    tm_r = _pick(nt, (512, 256, 128))
    tm_d = _pick(nt, (512, 256, 128))
    cpr = lp // SSM_CHUNK
    assert lp % SSM_CHUNK == 0 and pad % DMA_UNROLL == 0 and pad < SSM_CHUNK
    assert tm_r == tm_d
    top_e, gates, rank, lpos, tabs, cnt = _router(t, w_r, b_r, tm_r, cpr, pad)

    counts = cnt[:, 0]
    padded = ((counts + MOE_ROWS - 1) // MOE_ROWS) * MOE_ROWS
    pend = jnp.cumsum(padded)
    poff = pend - padded
    expert_ids = jnp.arange(ne, dtype=I32)[:, None, None]
    dest = jnp.sum(jnp.where(top_e[None] == expert_ids, poff[:, None, None], 0), axis=0) + rank
    seg_cnt = tabs[:, 0, :, 0].reshape(-1)
    seg_lst = tabs[:, 1, :, 0].reshape(-1)
    seg_dst = (tabs[:, 2, :, 0] + poff[None, :].astype(I32)).reshape(-1)
    tile_pieces = jnp.sum((tabs[:, 0, :, 0] + SEG_ROWS - 1) // SEG_ROWS, axis=1)
    n_real = (nt - (nt // lp) * pad) * TOP_K
    n_rows = -(-n_real // MOE_ROWS) * MOE_ROWS + (ne + 1) * MOE_ROWS
    n_blk = n_rows // MOE_ROWS
    blk_start = jnp.arange(n_blk, dtype=I32) * MOE_ROWS
    blk_e = jnp.minimum(jnp.sum(blk_start[:, None] >= pend[None, :], axis=1), ne - 1).astype(I32)
    n_used = (pend[-1:] // MOE_ROWS).astype(I32)
    tiles = lambda a: a.reshape(TOP_K, nt // tm_d, tm_d).transpose(1, 0, 2).reshape(-1)

    x_buf = _dispatch(t, tiles(dest.astype(I32)), pend.astype(I32), padded.astype(I32), n_used, n_rows, tm_d,
                      cpr, pad)
    y_buf = _ffn(x_buf, blk_e, n_used, layer, w1, b1[:, None, 0::2], b1[:, None, 1::2], w2, b2[:, None, :])
    return _combine(y_buf, tiles(lpos), tiles(gates), seg_cnt, seg_lst, seg_dst, tile_pieces, t, ln_g[None, :],
                    ln_b[None, :], alpha, tm_d, ne)


def _conformer_layer(h, w1, b1, dw, dwb, ln_g, ln_b, w2, b2, mix_g, mix_b, alpha, pad):
    bsz, lp, d = h.shape
    c = dw.shape[1]
    tl = _pick(lp, (544, 272, 384, 192, 128))
    u = _glu(h, w1[:, :c].astype(BF16), w1[:, c:].astype(BF16), b1[None, :c], b1[None, c:], pad, tl)
    tl_conv = _pick(lp, (1088, 576, 384, 192, 128))
    return _conv_mix(u, h, dw, dwb[None, :], ln_g[None, :], ln_b[None, :], w2.astype(BF16), b2[None, :],
                     mix_g[None, :], mix_b[None, :], alpha, tl_conv)


def _mamba_layer(h, w_in, conv_w, conv_b, dt_bias, a_log, d_skip, norm_g, w_out, mix_g, mix_b, alpha, pad):
    bsz, lp, d = h.shape
    nt = bsz * lp
    d_inner = w_out.shape[0]
    nh = dt_bias.shape[0]
    conv_dim = conv_w.shape[1]
    nproj = d_inner + conv_dim
    hpg = nh // SSM_GROUPS
    tm = _pick(nt, (1024, 512, 256, 128))
    proj = _mm(h.reshape(nt, d), w_in[:, :nproj].astype(BF16), tm, _pick(nproj, (2048, 1024, 512, 256, 128)))
    a_neg = -jnp.exp(a_log.astype(F32))
    dtc, csc, csr = _dt_prep(h, w_in[:, nproj:], dt_bias, a_neg, pad)
    nc = lp // SSM_CHUNK
    csr = csr.reshape(bsz, nc, SSM_GROUPS, hpg, SSM_CHUNK).transpose(0, 2, 1, 3, 4)
    d_exp = jnp.repeat(d_skip.astype(F32), SSM_HEAD_DIM)[None, :]
    y = _ssd(proj.reshape(bsz, lp, nproj), dtc, csc, csr, conv_w, conv_b[None, :], d_exp, norm_g[None, :],
             pad, d_inner)
    out = _mm_resid_ln(y.reshape(nt, d_inner), w_out.astype(BF16), h.reshape(nt, d),
                       mix_g[None, :], mix_b[None, :], alpha, _pick(nt, (512, 256, 128)))
    return out.reshape(bsz, lp, d)


def kernel(x, meta_tokens, conv_w1, conv_b1, conv_dw, conv_dwb, conv_ln_g, conv_ln_b, conv_w2, conv_b2,
           ssm_w_in, ssm_conv_w, ssm_conv_b, ssm_dt_bias, ssm_a_log, ssm_d, ssm_norm_g, ssm_w_out,
           moe_w_router, moe_b_router, moe_w1, moe_b1, moe_w2, moe_b2,
           ln_mix_g, ln_mix_b, ln_ffn_g, ln_ffn_b):
    bsz, seq, d = x.shape
    depth = ln_mix_g.shape[0]
    alpha = (2 * depth) ** 0.25
    pad = SSM_CHUNK - N_META
    lp = pad + N_META + seq
    assert lp % SSM_CHUNK == 0
    meta = jnp.broadcast_to(meta_tokens.astype(x.dtype)[None], (bsz, N_META, d))
    h = jnp.concatenate([jnp.zeros((bsz, pad, d), x.dtype), meta, x], axis=1)
    for i in range(depth):
        j = i // 2
        if i % 2 == 0:
            h = _conformer_layer(h, conv_w1[j], conv_b1[j], conv_dw[j], conv_dwb[j], conv_ln_g[j],
                                 conv_ln_b[j], conv_w2[j], conv_b2[j], ln_mix_g[i], ln_mix_b[i], alpha, pad)
        else:
            h = _mamba_layer(h, ssm_w_in[j], ssm_conv_w[j], ssm_conv_b[j], ssm_dt_bias[j], ssm_a_log[j],
                             ssm_d[j], ssm_norm_g[j], ssm_w_out[j], ln_mix_g[i], ln_mix_b[i], alpha, pad)
        t = _moe_layer(h.reshape(bsz * lp, d), i, lp, pad, moe_w_router[i], moe_b_router[i], moe_w1, moe_b1[i],
                       moe_w2, moe_b2[i], ln_ffn_g[i], ln_ffn_b[i], alpha)
        h = t.reshape(bsz, lp, d)
    return h[:, pad + N_META:]
```

```python
import functools

import jax
import jax.numpy as jnp
from jax import lax
from jax.experimental import pallas as pl
from jax.experimental.pallas import tpu as pltpu

F32 = jnp.float32
BF16 = jnp.bfloat16
I32 = jnp.int32
HIGHEST = lax.Precision.HIGHEST

N_META = 16
SSM_HEAD_DIM = 64
SSM_GROUPS = 8
SSM_STATE = 128
SSM_CHUNK = 128
TOP_K = 4
SWIGLU_ALPHA = 1.702
SWIGLU_LIMIT = 7.0
LN_EPS = 1e-5
RMS_EPS = 1e-5

CONV_HALO = 32
CONV_ROWS = 64
CONV_LANES = 128
MOE_ROWS = 512
DMA_UNROLL = 8
DMA_QUEUES = 2
SEG_ROWS = 16
LANES = 128
VMEM_LIMIT = 56 * 1024 * 1024


def _cparams(*sem):
    return pltpu.CompilerParams(dimension_semantics=tuple(sem), vmem_limit_bytes=VMEM_LIMIT)


def _pick(n, prefs):
    for p in prefs:
        if n % p == 0:
            return p
    raise ValueError(f"no tile for {n} in {prefs}")


def _layer_norm(y, g, b):
    mu = jnp.mean(y, axis=-1, keepdims=True)
    yc = y - mu
    var = jnp.mean(yc * yc, axis=-1, keepdims=True)
    return yc * lax.rsqrt(var + LN_EPS) * g + b


def _dot_3pass(a, b, dims):
    a_hi = a.astype(BF16)
    b_hi = b.astype(BF16)
    a_lo = (a - a_hi.astype(F32)).astype(BF16)
    b_lo = (b - b_hi.astype(F32)).astype(BF16)
    dot = lambda u, v: lax.dot_general(u, v, (dims, ((), ())), preferred_element_type=F32)
    return dot(a_hi, b_hi) + (dot(a_hi, b_lo) + dot(a_lo, b_hi))


def _silu(x):
    h = 0.5 * x
    return h + h * jnp.tanh(h)


def _sigmoid(x):
    return 1.0 / (1.0 + jnp.exp(-x))


def _softplus(x):
    return jnp.maximum(x, 0.0) + jnp.log(1.0 + jnp.exp(-jnp.abs(x)))


def _glu_kernel(x_ref, wa_ref, wg_ref, ba_ref, bg_ref, o_ref, *, pad):
    j = pl.program_id(1)
    x = x_ref[0].astype(BF16)
    a = jnp.dot(x, wa_ref[...], preferred_element_type=F32) + ba_ref[...]
    g = jnp.dot(x, wg_ref[...], preferred_element_type=F32) + bg_ref[...]
    u = a * _sigmoid(g)
    row = lax.broadcasted_iota(I32, u.shape, 0)
    valid = jnp.logical_or(j > 0, row >= pad)
    o_ref[0] = jnp.where(valid, u, 0.0)


def _glu(h, wa, wg, ba, bg, pad, tl):
    bsz, lp, d = h.shape
    c = wa.shape[1]
    return pl.pallas_call(
        functools.partial(_glu_kernel, pad=pad),
        grid=(bsz, lp // tl),
        in_specs=[
            pl.BlockSpec((1, tl, d), lambda b, j: (b, j, 0)),
            pl.BlockSpec((d, c), lambda b, j: (0, 0)),
            pl.BlockSpec((d, c), lambda b, j: (0, 0)),
            pl.BlockSpec((1, c), lambda b, j: (0, 0)),
            pl.BlockSpec((1, c), lambda b, j: (0, 0)),
        ],
        out_specs=pl.BlockSpec((1, tl, c), lambda b, j: (b, j, 0)),
        out_shape=jax.ShapeDtypeStruct((bsz, lp, c), F32),
        compiler_params=_cparams("parallel", "parallel"),
        name="conf_glu",
    )(h, wa, wg, ba, bg)


def _conv_mix_kernel(ucur_ref, uprev_ref, h_ref, dw_ref, dwb_ref, g1_ref, b1_ref, w2_ref, b2_ref,
                     g2_ref, bb2_ref, o_ref, win_ref, act_ref, *, taps, alpha):
    j = pl.program_id(1)
    tl = ucur_ref.shape[1]

    @pl.when(j == 0)
    def _():
        win_ref[0:CONV_HALO, :] = jnp.zeros((CONV_HALO, win_ref.shape[1]), F32)

    @pl.when(j > 0)
    def _():
        win_ref[0:CONV_HALO, :] = uprev_ref[0, tl - CONV_HALO:tl, :]

    win_ref[CONV_HALO:CONV_HALO + tl, :] = ucur_ref[0]

    def chunk(c, carry):
        base = pl.multiple_of(c * CONV_ROWS, CONV_ROWS)
        n = CONV_HALO + CONV_ROWS
        parts = []
        for lc in range(win_ref.shape[1] // CONV_LANES):
            sl = slice(lc * CONV_LANES, (lc + 1) * CONV_LANES)
            win = win_ref[pl.ds(base, n), sl]
            acc = jnp.zeros((CONV_ROWS, CONV_LANES), F32) + dwb_ref[:, sl]
            for s in range(8):
                ws = win if s == 0 else pltpu.roll(win, n - s, 0)
                for k in range(taps):
                    off = CONV_HALO + k - (taps - 1)
                    if off % 8 == s:
                        acc = acc + ws[off - s:off - s + CONV_ROWS, :] * dw_ref[k:k + 1, sl]
            parts.append(acc)
        y = _layer_norm(jnp.concatenate(parts, axis=1), g1_ref[...], b1_ref[...])
        y = _silu(y)
        act_ref[pl.ds(base, CONV_ROWS), :] = y.astype(BF16)
        return carry

    lax.fori_loop(0, tl // CONV_ROWS, chunk, 0, unroll=4)
    mix = jnp.dot(act_ref[...], w2_ref[...], preferred_element_type=F32) + b2_ref[...]
    y = alpha * h_ref[0] + mix
    o_ref[0] = _layer_norm(y, g2_ref[...], bb2_ref[...])


def _conv_mix(u, h, dw, dwb, g1, b1, w2, b2, g2, bb2, alpha, tl):
    bsz, lp, c = u.shape
    d = h.shape[2]
    taps = dw.shape[0]
    assert taps - 1 <= CONV_HALO and tl % CONV_ROWS == 0
    vec = lambda n: pl.BlockSpec((1, n), lambda b, j: (0, 0))
    return pl.pallas_call(
        functools.partial(_conv_mix_kernel, taps=taps, alpha=alpha),
        grid=(bsz, lp // tl),
        in_specs=[
            pl.BlockSpec((1, tl, c), lambda b, j: (b, j, 0)),
            pl.BlockSpec((1, tl, c), lambda b, j: (b, jnp.maximum(j - 1, 0), 0)),
            pl.BlockSpec((1, tl, d), lambda b, j: (b, j, 0)),
            pl.BlockSpec((taps, c), lambda b, j: (0, 0)),
            vec(c), vec(c), vec(c),
            pl.BlockSpec((c, d), lambda b, j: (0, 0)),
            vec(d), vec(d), vec(d),
        ],
        out_specs=pl.BlockSpec((1, tl, d), lambda b, j: (b, j, 0)),
        out_shape=jax.ShapeDtypeStruct((bsz, lp, d), F32),
        scratch_shapes=[pltpu.VMEM((CONV_HALO + tl, c), F32), pltpu.VMEM((tl, c), BF16)],
        compiler_params=_cparams("parallel", "parallel"),
        name="conf_conv_mix",
    )(u, u, h, dw, dwb, g1, b1, w2, b2, g2, bb2)


def _mm_kernel(x_ref, w_ref, o_ref):
    o_ref[...] = jnp.dot(x_ref[...].astype(BF16), w_ref[...], preferred_element_type=F32).astype(o_ref.dtype)


def _mm(x, w, tm, tn, out_dtype=F32):
    m, k = x.shape
    n = w.shape[1]
    return pl.pallas_call(
        _mm_kernel,
        grid=(n // tn, m // tm),
        in_specs=[pl.BlockSpec((tm, k), lambda jn, im: (im, 0)),
                  pl.BlockSpec((k, tn), lambda jn, im: (0, jn))],
        out_specs=pl.BlockSpec((tm, tn), lambda jn, im: (im, jn)),
        out_shape=jax.ShapeDtypeStruct((m, n), out_dtype),
        compiler_params=_cparams("parallel", "parallel"),
        name="mm",
    )(x, w)


def _mm_resid_ln_kernel(x_ref, w_ref, h_ref, g_ref, b_ref, o_ref, *, alpha):
    mix = jnp.dot(x_ref[...].astype(BF16), w_ref[...], preferred_element_type=F32)
    o_ref[...] = _layer_norm(alpha * h_ref[...] + mix, g_ref[...], b_ref[...])


def _mm_resid_ln(x, w, h, g, b, alpha, tm):
    m, k = x.shape
    d = w.shape[1]
    return pl.pallas_call(
        functools.partial(_mm_resid_ln_kernel, alpha=alpha),
        grid=(m // tm,),
        in_specs=[pl.BlockSpec((tm, k), lambda i: (i, 0)),
                  pl.BlockSpec((k, d), lambda i: (0, 0)),
                  pl.BlockSpec((tm, d), lambda i: (i, 0)),
                  pl.BlockSpec((1, d), lambda i: (0, 0)),
                  pl.BlockSpec((1, d), lambda i: (0, 0))],
        out_specs=pl.BlockSpec((tm, d), lambda i: (i, 0)),
        out_shape=jax.ShapeDtypeStruct((m, d), F32),
        compiler_params=_cparams("parallel"),
        name="mm_resid_ln",
    )(x, w, h, g, b)


def _dt_kernel(x_ref, wc_ref, bc_ref, ac_ref, dtc_ref, csc_ref, csr_ref, *, pad):
    n = SSM_CHUNK
    nh = csr_ref.shape[2]
    raw_c = _dot_3pass(x_ref[0], wc_ref[...], ((1,), (0,)))
    row_c = lax.broadcasted_iota(I32, raw_c.shape, 0)
    dt_c = jnp.where(row_c < pad, 0.0, _softplus(raw_c + bc_ref[...]))
    dtc_ref[0] = dt_c
    li = lax.broadcasted_iota(I32, (n, n), 0)
    si = lax.broadcasted_iota(I32, (n, n), 1)
    tri = jnp.where(si <= li, 1.0, 0.0)
    for c in range(x_ref.shape[1] // n):
        a_c = dt_c[c * n:(c + 1) * n, :] * ac_ref[...]
        cs_c = jnp.dot(tri, a_c, preferred_element_type=F32, precision=HIGHEST)
        csc_ref[0, c * n:(c + 1) * n, :] = cs_c
        csr_ref[0, c] = cs_c.T[:nh, :]


def _dt_prep(h, w_dt, dt_bias, a_neg, pad):
    bsz, lp, d = h.shape
    nh = w_dt.shape[1]
    nc = lp // SSM_CHUNK
    lanes = 128
    wc = jnp.zeros((d, lanes), F32).at[:, :nh].set(w_dt)
    bc = jnp.zeros((1, lanes), F32).at[0, :nh].set(dt_bias)
    ac = jnp.zeros((1, lanes), F32).at[0, :nh].set(a_neg)
    return pl.pallas_call(
        functools.partial(_dt_kernel, pad=pad),
        grid=(bsz,),
        in_specs=[
            pl.BlockSpec((1, lp, d), lambda b: (b, 0, 0)),
            pl.BlockSpec((d, lanes), lambda b: (0, 0)),
            pl.BlockSpec((1, lanes), lambda b: (0, 0)),
            pl.BlockSpec((1, lanes), lambda b: (0, 0)),
        ],
        out_specs=[
            pl.BlockSpec((1, lp, lanes), lambda b: (b, 0, 0)),
            pl.BlockSpec((1, lp, lanes), lambda b: (b, 0, 0)),
            pl.BlockSpec((1, nc, nh, SSM_CHUNK), lambda b: (b, 0, 0, 0)),
        ],
        out_shape=[
            jax.ShapeDtypeStruct((bsz, lp, lanes), F32),
            jax.ShapeDtypeStruct((bsz, lp, lanes), F32),
            jax.ShapeDtypeStruct((bsz, nc, nh, SSM_CHUNK), F32),
        ],
        compiler_params=_cparams("parallel"),
        name="ssm_dt",
    )(h, wc, bc, ac)


def _ssd_kernel(xs_ref, bm_ref, cm_ref, z_ref, dtc_ref, csc_ref, csr_ref,
                cwx_ref, cwb_ref, cwc_ref, cbx_ref, cbb_ref, cbc_ref, dsk_ref, ng_ref,
                o_ref, state_ref, carx_ref, carb_ref, carc_ref, *, pad, hpg, taps):
    g = pl.program_id(1)
    q = SSM_CHUNK
    hd = SSM_HEAD_DIM
    gw = hpg * hd
    n_chunks = xs_ref.shape[1] // q

    state_ref[...] = jnp.zeros(state_ref.shape, F32)
    carx_ref[...] = jnp.zeros(carx_ref.shape, F32)
    carb_ref[...] = jnp.zeros(carb_ref.shape, F32)
    carc_ref[...] = jnp.zeros(carc_ref.shape, F32)

    hrow = lax.broadcasted_iota(I32, (128, gw), 0)
    hcol = lax.broadcasted_iota(I32, (128, gw), 1) // hd
    sel = jnp.where(hrow == g * hpg + hcol, 1.0, 0.0).astype(BF16)

    def select_heads(v):
        hi = v.astype(BF16)
        r1 = v - hi.astype(F32)
        mid = r1.astype(BF16)
        lo = (r1 - mid.astype(F32)).astype(BF16)
        out = jnp.dot(hi, sel, preferred_element_type=F32)
        out = out + jnp.dot(mid, sel, preferred_element_type=F32)
        return out + jnp.dot(lo, sel, preferred_element_type=F32)

    lane_head = lax.broadcasted_iota(I32, (q, gw), 1) // hd
    head_mask = [jnp.where(lane_head == r, 1.0, 0.0).astype(BF16) for r in range(hpg)]
    li = lax.broadcasted_iota(I32, (q, q), 0)
    si = lax.broadcasted_iota(I32, (q, q), 1)
    causal = si <= li
    row1 = lax.broadcasted_iota(I32, (q, 1), 0)

    def conv_act(raw, car_ref, w_ref, b_ref, valid):
        if valid is not None:
            raw = jnp.where(valid, raw, 0.0)
        win = jnp.concatenate([car_ref[...], raw], axis=0)
        acc = win * w_ref[0:1, :]
        for k in range(1, taps):
            acc = pltpu.roll(acc, 1, 0) + win * w_ref[k:k + 1, :]
        acc = acc[8:8 + q, :] + b_ref[...]
        car_ref[...] = raw[q - 8:q, :]
        act = _silu(acc)
        return act if valid is None else jnp.where(valid, act, 0.0)

    def chunk(c, first):
        r0 = 0 if first else pl.multiple_of(c * q, q)
        valid = (row1 >= pad) if first else None
        xs = conv_act(xs_ref[0, pl.ds(r0, q), :], carx_ref, cwx_ref, cbx_ref, valid)
        bm = conv_act(bm_ref[0, pl.ds(r0, q), :], carb_ref, cwb_ref, cbb_ref, valid)
        cm = conv_act(cm_ref[0, pl.ds(r0, q), :], carc_ref, cwc_ref, cbc_ref, valid)

        dt_e = select_heads(dtc_ref[0, pl.ds(r0, q), :])
        cs_e = select_heads(csc_ref[0, pl.ds(r0, q), :])
        cs_r = csr_ref[0, 0, c]
        a_last = cs_e[q - 1:q, :]

        x_dt = xs * dt_e
        xw = (x_dt * jnp.exp(a_last - cs_e)).astype(BF16)
        bm16 = bm.astype(BF16)
        cm16 = cm.astype(BF16)
        cb = lax.dot_general(cm16, bm16, (((1,), (1,)), ((), ())), preferred_element_type=F32)

        m_parts = []
        x_parts = []
        x_dt16 = x_dt.astype(BF16)
        for r in range(hpg):
            col = cs_e[:, r * hd:r * hd + 1]
            seg = col - cs_r[r:r + 1, :]
            decay = jnp.exp(jnp.where(causal, seg, -jnp.inf))
            m_parts.append((cb * decay).astype(BF16))
            x_parts.append(x_dt16 * head_mask[r])
        m_cat = jnp.concatenate(m_parts, axis=1)
        x_bd = jnp.concatenate(x_parts, axis=0)
        y = jnp.dot(m_cat, x_bd, preferred_element_type=F32)

        st = state_ref[...]
        y = y + jnp.dot(cm16, st.astype(BF16), preferred_element_type=F32) * jnp.exp(cs_e)
        new = jnp.dot(bm.T.astype(BF16), xw, preferred_element_type=F32)
        state_ref[...] = st * jnp.exp(a_last) + new
        y = y + dsk_ref[...] * xs

        z = z_ref[0, pl.ds(r0, q), :]
        y = y * _silu(z)
        y = y * lax.rsqrt(jnp.mean(y * y, axis=-1, keepdims=True) + RMS_EPS) * ng_ref[...]
        o_ref[0, pl.ds(r0, q), :] = y.astype(o_ref.dtype)

    def later_chunk(c, carry):
        chunk(c, False)
        return carry

    chunk(0, True)
    lax.fori_loop(1, n_chunks, later_chunk, 0, unroll=4)


def _ssd(proj, dtc, csc, csr, conv_w, conv_b, d_exp, norm_g, pad, d_inner):
    bsz, lp, _ = proj.shape
    g = SSM_GROUPS
    n = SSM_STATE
    gw = d_inner // g
    hpg = gw // SSM_HEAD_DIM
    assert gw % 128 == 0 and n == 128 and dtc.shape[2] == 128
    taps = conv_w.shape[0]
    nc = lp // SSM_CHUNK
    zb = 0
    xb = d_inner // gw
    bb = 2 * d_inner // n
    cb_ = (2 * d_inner + g * n) // n
    cxb = 0
    cbb = d_inner // n
    ccb = (d_inner + g * n) // n
    seq = lambda w, off: pl.BlockSpec((1, lp, w), lambda b, j, off=off: (b, 0, off + j))
    cw = lambda w, off: pl.BlockSpec((taps, w), lambda b, j, off=off: (0, off + j))
    cbv = lambda w, off: pl.BlockSpec((1, w), lambda b, j, off=off: (0, off + j))
    return pl.pallas_call(
        functools.partial(_ssd_kernel, pad=pad, hpg=hpg, taps=taps),
        grid=(bsz, g),
        in_specs=[
            seq(gw, xb), seq(n, bb), seq(n, cb_), seq(gw, zb),
            pl.BlockSpec((1, lp, 128), lambda b, j: (b, 0, 0)),
            pl.BlockSpec((1, lp, 128), lambda b, j: (b, 0, 0)),
            pl.BlockSpec((1, 1, nc, hpg, SSM_CHUNK), lambda b, j: (b, j, 0, 0, 0)),
            cw(gw, cxb), cw(n, cbb), cw(n, ccb),
            cbv(gw, cxb), cbv(n, cbb), cbv(n, ccb),
            pl.BlockSpec((1, gw), lambda b, j: (0, j)),
            pl.BlockSpec((1, gw), lambda b, j: (0, j)),
        ],
        out_specs=pl.BlockSpec((1, lp, gw), lambda b, j: (b, 0, j)),
        out_shape=jax.ShapeDtypeStruct((bsz, lp, d_inner), BF16),
        scratch_shapes=[pltpu.VMEM((n, gw), F32), pltpu.VMEM((8, gw), F32),
                        pltpu.VMEM((8, n), F32), pltpu.VMEM((8, n), F32)],
        compiler_params=_cparams("parallel", "parallel"),
        name="ssm_ssd",
    )(proj, proj, proj, proj, dtc, csc, csr, conv_w, conv_w, conv_w, conv_b, conv_b, conv_b, d_exp, norm_g)


def _pad_chunk(tile, q, tm, chunks_per_row):
    return lax.rem(tile * (tm // SSM_CHUNK) + q, chunks_per_row) == 0


def _for_real_groups(tile, tm, chunks_per_row, pad, body):
    per_chunk = SSM_CHUNK // DMA_UNROLL
    for q in range(tm // SSM_CHUNK):
        skip = jnp.where(_pad_chunk(tile, q, tm, chunks_per_row), pad // DMA_UNROLL, 0)
        lax.fori_loop(q * per_chunk + skip, (q + 1) * per_chunk, body, 0)


def _router_kernel(x_ref, wr_ref, br_ref, e_ref, gate_ref, rank_ref, lpos_ref, tab_ref, cnt_ref, carry_ref,
                   *, chunks_per_row, pad):
    i = pl.program_id(0)

    @pl.when(i == 0)
    def _():
        carry_ref[...] = jnp.zeros(carry_ref.shape, F32)

    x = x_ref[...]
    tm = x.shape[0]
    ne = wr_ref.shape[0]
    logits = _dot_3pass(wr_ref[...], x, ((1,), (1,))) + br_ref[...]
    ids = lax.broadcasted_iota(I32, (ne, tm), 0)
    vals = logits
    top_v, top_e, sels = [], [], []
    for _ in range(TOP_K):
        m = jnp.max(vals, axis=0, keepdims=True)
        idx = jnp.min(jnp.where(vals == m, ids, ne), axis=0, keepdims=True)
        sel = ids == idx
        top_v.append(m)
        top_e.append(idx)
        sels.append(sel)
        vals = jnp.where(sel, -jnp.inf, vals)
    ex = [jnp.exp(v - top_v[0]) for v in top_v]
    den = ex[0]
    for t in ex[1:]:
        den = den + t
    e_ref[...] = jnp.concatenate(top_e, axis=0)
    gate_ref[...] = jnp.concatenate([t / den for t in ex], axis=0)

    onehot = sels[0].astype(F32)
    for s in sels[1:]:
        onehot = onehot + s.astype(F32)
    lane = lax.broadcasted_iota(I32, (1, tm), 1)
    real = jnp.ones((1, tm), F32)
    for q in range(tm // SSM_CHUNK):
        inert = jnp.logical_and(_pad_chunk(i, q, tm, chunks_per_row),
                                jnp.logical_and(lane >= q * SSM_CHUNK, lane < q * SSM_CHUNK + pad))
        real = jnp.where(inert, 0.0, real)
    onehot = onehot * real
    ji = lax.broadcasted_iota(I32, (tm, tm), 0)
    ti = lax.broadcasted_iota(I32, (tm, tm), 1)
    before = (ji < ti).astype(BF16)
    prefix = jnp.dot(onehot.astype(BF16), before, preferred_element_type=F32)
    tile_cnt = jnp.sum(onehot, axis=1, keepdims=True) + jnp.zeros((ne, LANES), F32)
    pieces = jnp.floor((tile_cnt + (SEG_ROWS - 1.0)) * (1.0 / SEG_ROWS))
    ei = lax.broadcasted_iota(I32, (ne, ne), 0)
    ej = lax.broadcasted_iota(I32, (ne, ne), 1)
    below = jnp.where(ej < ei, 1.0, 0.0).astype(BF16)
    lstart = SEG_ROWS * jnp.dot(below, pieces.astype(BF16), preferred_element_type=F32)
    local = prefix + lstart[:, 0:1]
    lpos = [jnp.sum(jnp.where(s, local, 0.0), axis=0, keepdims=True) for s in sels]
    spare = float(TOP_K * tm + ne * SEG_ROWS - 1)
    lpos_ref[...] = jnp.where(real > 0.0, jnp.concatenate(lpos, axis=0), spare).astype(I32)
    tab_ref[0, 0] = tile_cnt.astype(I32)
    tab_ref[0, 1] = lstart.astype(I32)
    tab_ref[0, 2] = carry_ref[...].astype(I32)

    prefix = prefix + carry_ref[:, 0:1]
    ranks = [jnp.sum(jnp.where(s, prefix, 0.0), axis=0, keepdims=True) for s in sels]
    rank_ref[...] = jnp.concatenate(ranks, axis=0).astype(I32)
    carry_ref[...] = carry_ref[...] + tile_cnt
    cnt_ref[...] = carry_ref[...].astype(I32)


def _router(t, w_r, b_r, tm, chunks_per_row, pad):
    nt, d = t.shape
    ne = w_r.shape[1]
    assert tm % SSM_CHUNK == 0
    return pl.pallas_call(
        functools.partial(_router_kernel, chunks_per_row=chunks_per_row, pad=pad),
        grid=(nt // tm,),
        in_specs=[pl.BlockSpec((tm, d), lambda i: (i, 0)),
                  pl.BlockSpec((ne, d), lambda i: (0, 0)),
                  pl.BlockSpec((ne, 1), lambda i: (0, 0))],
        out_specs=[pl.BlockSpec((TOP_K, tm), lambda i: (0, i)),
                   pl.BlockSpec((TOP_K, tm), lambda i: (0, i)),
                   pl.BlockSpec((TOP_K, tm), lambda i: (0, i)),
                   pl.BlockSpec((TOP_K, tm), lambda i: (0, i)),
                   pl.BlockSpec((1, 3, ne, LANES), lambda i: (i, 0, 0, 0)),
                   pl.BlockSpec((ne, 128), lambda i: (0, 0))],
        out_shape=[jax.ShapeDtypeStruct((TOP_K, nt), I32),
                   jax.ShapeDtypeStruct((TOP_K, nt), F32),
                   jax.ShapeDtypeStruct((TOP_K, nt), I32),
                   jax.ShapeDtypeStruct((TOP_K, nt), I32),
                   jax.ShapeDtypeStruct((nt // tm, 3, ne, LANES), I32),
                   jax.ShapeDtypeStruct((ne, 128), I32)],
        scratch_shapes=[pltpu.VMEM((ne, 128), F32)],
        compiler_params=_cparams("arbitrary"),
        name="moe_router",
    )(t, w_r.T, b_r[:, None])


def _dispatch_kernel(pend_ref, padded_ref, nused_ref, dest_hbm, t_ref, xbuf_hbm,
                     idx_ref, rows_ref, zero_ref, isem, sem, zsem, *, tm, nseg, n_blk, chunks_per_row, pad):
    i = pl.program_id(0)
    n = TOP_K * tm
    blk = MOE_ROWS * nseg

    @pl.when(i == 0)
    def _():
        zero_ref[...] = jnp.zeros(zero_ref.shape, zero_ref.dtype)

        def group_fill(e):
            return pltpu.make_async_copy(
                zero_ref, xbuf_hbm.at[pl.ds(pl.multiple_of((pend_ref[e] - MOE_ROWS) * nseg, blk), blk), :], zsem)

        def tail_fill(j):
            return pltpu.make_async_copy(zero_ref, xbuf_hbm.at[pl.ds(pl.multiple_of(j * blk, blk), blk), :], zsem)

        def start_group(e, carry):
            @pl.when(padded_ref[e] > 0)
            def _():
                group_fill(e).start()
            return carry

        def wait_group(e, carry):
            @pl.when(padded_ref[e] > 0)
            def _():
                group_fill(e).wait()
            return carry

        def start_tail(j, carry):
            tail_fill(j).start()
            return carry

        def wait_tail(j, carry):
            tail_fill(j).wait()
            return carry

        n_exp = pend_ref.shape[0]
        lax.fori_loop(0, n_exp, start_group, 0)
        lax.fori_loop(nused_ref[0], n_blk, start_tail, 0)
        lax.fori_loop(0, n_exp, wait_group, 0)
        lax.fori_loop(nused_ref[0], n_blk, wait_tail, 0)

    last = pl.num_programs(0) - 1

    def row_copy(slot, k, r):
        src = rows_ref.at[slot, pl.ds(pl.multiple_of(r * nseg, nseg), nseg), :]
        dst = xbuf_hbm.at[pl.ds(pl.multiple_of(idx_ref[slot * n + k * tm + r] * nseg, nseg), nseg), :]
        return pltpu.make_async_copy(src, dst, sem.at[slot])

    def drain_slot(slot, tile):
        def drain(grp, carry):
            for u in range(DMA_UNROLL):
                for k in range(TOP_K):
                    row_copy(slot, k, grp * DMA_UNROLL + u).wait()
            return carry

        _for_real_groups(tile, tm, chunks_per_row, pad, drain)

    for slot in range(2):
        @pl.when(i % 2 == slot)
        def _():
            @pl.when(i >= 2)
            def _():
                drain_slot(slot, i - 2)

            cp = pltpu.make_async_copy(dest_hbm.at[pl.ds(pl.multiple_of(i * n, n), n)],
                                       idx_ref.at[pl.ds(slot * n, n)], isem)
            cp.start()
            x = t_ref[...]
            for c in range(nseg):
                rows_ref[slot, pl.ds(c, tm, stride=nseg), :] = x[:, c * LANES:(c + 1) * LANES]
            cp.wait()

            def issue(grp, carry):
                for u in range(DMA_UNROLL):
                    for k in range(TOP_K):
                        row_copy(slot, k, grp * DMA_UNROLL + u).start(priority=k % DMA_QUEUES)
                return carry

            _for_real_groups(i, tm, chunks_per_row, pad, issue)

            @pl.when(i == last)
            def _():
                drain_slot(slot, i)

                @pl.when(i >= 1)
                def _():
                    drain_slot(1 - slot, i - 1)


def _dispatch(t, dest_tiles, pend, padded, n_used, n_rows, tm, chunks_per_row, pad):
    nt, d = t.shape
    nseg = d // LANES
    return pl.pallas_call(
        functools.partial(_dispatch_kernel, tm=tm, nseg=nseg, n_blk=n_rows // MOE_ROWS,
                          chunks_per_row=chunks_per_row, pad=pad),
        grid_spec=pltpu.PrefetchScalarGridSpec(
            num_scalar_prefetch=3,
            grid=(nt // tm,),
            in_specs=[pl.BlockSpec(memory_space=pl.ANY),
                      pl.BlockSpec((tm, d), lambda i, *_: (i, 0))],
            out_specs=pl.BlockSpec(memory_space=pl.ANY),
            scratch_shapes=[pltpu.SMEM((2 * TOP_K * tm,), I32), pltpu.VMEM((2, tm * nseg, LANES), t.dtype),
                            pltpu.VMEM((MOE_ROWS * nseg, LANES), t.dtype),
                            pltpu.SemaphoreType.DMA(()), pltpu.SemaphoreType.DMA((2,)),
                            pltpu.SemaphoreType.DMA(())],
        ),
        out_shape=jax.ShapeDtypeStruct((n_rows * nseg, LANES), t.dtype),
        compiler_params=pltpu.CompilerParams(dimension_semantics=("arbitrary",), has_side_effects=True,
                                             vmem_limit_bytes=VMEM_LIMIT),
        name="moe_dispatch",
    )(pend, padded, n_used, dest_tiles, t)


def _ffn_kernel(blk_e_ref, nused_ref, x_ref, w1_ref, b1g_ref, b1l_ref, w2_ref, b2_ref, y_ref,
                w1g_ref, w1l_ref, w2c_ref, *, nseg):
    i = pl.program_id(0)
    rows = x_ref.shape[0] // nseg
    e = blk_e_ref[i]
    e_prev = blk_e_ref[jnp.maximum(i - 1, 0)]

    @pl.when(jnp.logical_or(i == 0, e != e_prev))
    def _():
        w = 2 * LANES
        r = lax.broadcasted_iota(I32, (w, w), 0)
        c = lax.broadcasted_iota(I32, (w, w), 1)
        src_col = jnp.where(c < LANES, 2 * c, 2 * (c - LANES) + 1)
        perm = jnp.where(r == src_col, 1.0, 0.0).astype(BF16)
        for cc in range(w1_ref.shape[3] // w):
            chunk = w1_ref[0, 0, :, cc * w:(cc + 1) * w].astype(BF16)
            res = jnp.dot(chunk, perm, preferred_element_type=F32).astype(BF16)
            w1g_ref[:, cc * LANES:(cc + 1) * LANES] = res[:, :LANES]
            w1l_ref[:, cc * LANES:(cc + 1) * LANES] = res[:, LANES:]
        w2c_ref[...] = w2_ref[0, 0].astype(BF16)

    @pl.when(i < nused_ref[0])
    def _():
        x = jnp.concatenate([x_ref[pl.ds(c, rows, stride=nseg), :] for c in range(nseg)], axis=1).astype(BF16)
        hg = jnp.dot(x, w1g_ref[...], preferred_element_type=F32) + b1g_ref[0]
        hl = jnp.dot(x, w1l_ref[...], preferred_element_type=F32) + b1l_ref[0]
        xg = jnp.minimum(hg, SWIGLU_LIMIT)
        xl = jnp.clip(hl, -SWIGLU_LIMIT, SWIGLU_LIMIT)
        act = xg * _sigmoid(SWIGLU_ALPHA * xg) * (xl + 1.0)
        y = jnp.dot(act.astype(BF16), w2c_ref[...], preferred_element_type=F32) + b2_ref[0]
        for c in range(nseg):
            y_ref[pl.ds(c, rows, stride=nseg), :] = y[:, c * LANES:(c + 1) * LANES]

    @pl.when(i >= nused_ref[0])
    def _():
        y_ref[...] = jnp.zeros(y_ref.shape, F32)


def _ffn(x_buf, blk_e, n_used, layer, w1, b1g, b1l, w2, b2):
    _, ne, d, f2 = w1.shape
    f = f2 // 2
    nseg = d // LANES
    n_rows = x_buf.shape[0] // nseg
    n_blk = n_rows // MOE_ROWS
    assert f2 % (2 * LANES) == 0
    wspec = lambda a, b: pl.BlockSpec((1, a, b), lambda i, be, nu: (be[i], 0, 0))
    wfull = lambda a, b: pl.BlockSpec((1, 1, a, b), lambda i, be, nu: (layer, be[i], 0, 0))
    return pl.pallas_call(
        functools.partial(_ffn_kernel, nseg=nseg),
        grid_spec=pltpu.PrefetchScalarGridSpec(
            num_scalar_prefetch=2,
            grid=(n_blk,),
            in_specs=[pl.BlockSpec((MOE_ROWS * nseg, LANES), lambda i, be, nu: (i, 0)),
                      wfull(d, f2), wspec(1, f), wspec(1, f), wfull(f, d), wspec(1, d)],
            out_specs=pl.BlockSpec((MOE_ROWS * nseg, LANES), lambda i, be, nu: (i, 0)),
            scratch_shapes=[pltpu.VMEM((d, f), BF16), pltpu.VMEM((d, f), BF16), pltpu.VMEM((f, d), BF16)],
        ),
        out_shape=jax.ShapeDtypeStruct((n_rows * nseg, LANES), F32),
        compiler_params=_cparams("arbitrary"),
        name="moe_ffn",
    )(blk_e, n_used, x_buf, w1, b1g, b1l, w2, b2)


def _combine_kernel(cnt_ref, lst_ref, dst_ref, pcs_ref, lpos_hbm, gates_hbm, ybuf_hbm, h_ref, g_ref, b_ref, o_ref,
                    idx_ref, gsm_ref, sort_ref, sum_ref, isem, sem, *, tm, nseg, alpha, n_exp):
    i = pl.program_id(0)
    n = TOP_K * tm
    piece = SEG_ROWS * nseg

    def start_runs(slot, tile):
        def per_expert(e, carry):
            cnt = cnt_ref[tile * n_exp + e]
            dst0 = lst_ref[tile * n_exp + e]
            src0 = dst_ref[tile * n_exp + e]

            def piece_copy(j, c2):
                pltpu.make_async_copy(
                    ybuf_hbm.at[pl.ds(pl.multiple_of((src0 + j * SEG_ROWS) * nseg, nseg), piece), :],
                    sort_ref.at[slot, pl.ds(pl.multiple_of((dst0 + j * SEG_ROWS) * nseg, nseg), piece), :],
                    sem.at[slot]).start()
                return c2

            lax.fori_loop(0, (cnt + SEG_ROWS - 1) // SEG_ROWS, piece_copy, 0)
            return carry

        lax.fori_loop(0, n_exp, per_expert, 0)

    def wait_tile(slot, tile):
        piece_wait = pltpu.make_async_copy(ybuf_hbm.at[pl.ds(0, piece), :], sort_ref.at[slot, pl.ds(0, piece), :],
                                           sem.at[slot])

        def wait_piece(j, carry):
            piece_wait.wait()
            return carry

        lax.fori_loop(0, pcs_ref[tile], wait_piece, 0)

    @pl.when(i == 0)
    def _():
        sort_ref[...] = jnp.zeros(sort_ref.shape, F32)
        start_runs(0, 0)

    for cur in range(2):
        @pl.when(i % 2 == cur)
        def _():
            @pl.when(i + 1 < pl.num_programs(0))
            def _():
                start_runs(1 - cur, i + 1)

            c1 = pltpu.make_async_copy(lpos_hbm.at[pl.ds(pl.multiple_of(i * n, n), n)], idx_ref, isem.at[0])
            c2 = pltpu.make_async_copy(gates_hbm.at[pl.ds(pl.multiple_of(i * n, n), n)], gsm_ref, isem.at[1])
            c1.start()
            c2.start()
            wait_tile(cur, i)
            c1.wait()
            c2.wait()

            def reduce(grp, carry):
                for u in range(DMA_UNROLL):
                    r = grp * DMA_UNROLL + u
                    acc = None
                    for k in range(TOP_K):
                        slab = sort_ref[cur, pl.ds(pl.multiple_of(idx_ref[k * tm + r] * nseg, nseg), nseg), :]
                        term = gsm_ref[k * tm + r] * slab
                        acc = term if acc is None else acc + term
                    sum_ref[pl.ds(pl.multiple_of(r * nseg, nseg), nseg), :] = acc
                return carry

            lax.fori_loop(0, tm // DMA_UNROLL, reduce, 0)

    ffn = jnp.concatenate([sum_ref[pl.ds(c, tm, stride=nseg), :] for c in range(nseg)], axis=1)
    o_ref[...] = _layer_norm(alpha * h_ref[...] + ffn, g_ref[...], b_ref[...])


def _combine(y_buf, lpos_tiles, gate_tiles, seg_cnt, seg_lst, seg_dst, tile_pieces, h, g, b, alpha, tm, n_exp):
    nt, d = h.shape
    nseg = d // LANES
    sort_rows = TOP_K * tm + n_exp * SEG_ROWS
    return pl.pallas_call(
        functools.partial(_combine_kernel, tm=tm, nseg=nseg, alpha=alpha, n_exp=n_exp),
        grid_spec=pltpu.PrefetchScalarGridSpec(
            num_scalar_prefetch=4,
            grid=(nt // tm,),
            in_specs=[pl.BlockSpec(memory_space=pl.ANY),
                      pl.BlockSpec(memory_space=pl.ANY),
                      pl.BlockSpec(memory_space=pl.ANY),
                      pl.BlockSpec((tm, d), lambda i, *_: (i, 0)),
                      pl.BlockSpec((1, d), lambda i, *_: (0, 0)),
                      pl.BlockSpec((1, d), lambda i, *_: (0, 0))],
            out_specs=pl.BlockSpec((tm, d), lambda i, *_: (i, 0)),
            scratch_shapes=[pltpu.SMEM((TOP_K * tm,), I32), pltpu.SMEM((TOP_K * tm,), F32),
                            pltpu.VMEM((2, sort_rows * nseg, LANES), F32), pltpu.VMEM((tm * nseg, LANES), F32),
                            pltpu.SemaphoreType.DMA((2,)), pltpu.SemaphoreType.DMA((2,))],
        ),
        out_shape=jax.ShapeDtypeStruct((nt, d), F32),
        compiler_params=_cparams("arbitrary"),
        name="moe_combine",
    )(seg_cnt, seg_lst, seg_dst, tile_pieces, lpos_tiles, gate_tiles, y_buf, h, g, b)


def _moe_layer(t, layer, lp, pad, w_r, b_r, w1, b1, w2, b2, ln_g, ln_b, alpha):
    nt, d = t.shape
    ne = w_r.shape[1]
    tm_r = _pick(nt, (512, 256, 128))
    tm_d = _pick(nt, (512, 256, 128))
    cpr = lp // SSM_CHUNK
    assert lp % SSM_CHUNK == 0 and pad % DMA_UNROLL == 0 and pad < SSM_CHUNK
    assert tm_r == tm_d
    top_e, gates, rank, lpos, tabs, cnt = _router(t, w_r, b_r, tm_r, cpr, pad)

    counts = cnt[:, 0]
    padded = ((counts + MOE_ROWS - 1) // MOE_ROWS) * MOE_ROWS
    pend = jnp.cumsum(padded)
    poff = pend - padded
    expert_ids = jnp.arange(ne, dtype=I32)[:, None, None]
    dest = jnp.sum(jnp.where(top_e[None] == expert_ids, poff[:, None, None], 0), axis=0) + rank
    seg_cnt = tabs[:, 0, :, 0].reshape(-1)
    seg_lst = tabs[:, 1, :, 0].reshape(-1)
    seg_dst = (tabs[:, 2, :, 0] + poff[None, :].astype(I32)).reshape(-1)
    tile_pieces = jnp.sum((tabs[:, 0, :, 0] + SEG_ROWS - 1) // SEG_ROWS, axis=1)
    n_real = (nt - (nt // lp) * pad) * TOP_K
    n_rows = -(-n_real // MOE_ROWS) * MOE_ROWS + (ne + 1) * MOE_ROWS
    n_blk = n_rows // MOE_ROWS
    blk_start = jnp.arange(n_blk, dtype=I32) * MOE_ROWS
    blk_e = jnp.minimum(jnp.sum(blk_start[:, None] >= pend[None, :], axis=1), ne - 1).astype(I32)
    n_used = (pend[-1:] // MOE_ROWS).astype(I32)
    tiles = lambda a: a.reshape(TOP_K, nt // tm_d, tm_d).transpose(1, 0, 2).reshape(-1)

    x_buf = _dispatch(t, tiles(dest.astype(I32)), pend.astype(I32), padded.astype(I32), n_used, n_rows, tm_d,
                      cpr, pad)
    y_buf = _ffn(x_buf, blk_e, n_used, layer, w1, b1[:, None, 0::2], b1[:, None, 1::2], w2, b2[:, None, :])
    return _combine(y_buf, tiles(lpos), tiles(gates), seg_cnt, seg_lst, seg_dst, tile_pieces, t, ln_g[None, :],
                    ln_b[None, :], alpha, tm_d, ne)


def _conformer_layer(h, w1, b1, dw, dwb, ln_g, ln_b, w2, b2, mix_g, mix_b, alpha, pad):
    bsz, lp, d = h.shape
    c = dw.shape[1]
    tl = _pick(lp, (544, 272, 384, 192, 128))
    u = _glu(h, w1[:, :c].astype(BF16), w1[:, c:].astype(BF16), b1[None, :c], b1[None, c:], pad, tl)
    tl_conv = _pick(lp, (1088, 576, 384, 192, 128))
    return _conv_mix(u, h, dw, dwb[None, :], ln_g[None, :], ln_b[None, :], w2.astype(BF16), b2[None, :],
                     mix_g[None, :], mix_b[None, :], alpha, tl_conv)


def _mamba_layer(h, w_in, conv_w, conv_b, dt_bias, a_log, d_skip, norm_g, w_out, mix_g, mix_b, alpha, pad):
    bsz, lp, d = h.shape
    nt = bsz * lp
    d_inner = w_out.shape[0]
    nh = dt_bias.shape[0]
    conv_dim = conv_w.shape[1]
    nproj = d_inner + conv_dim
    hpg = nh // SSM_GROUPS
    tm = _pick(nt, (1024, 512, 256, 128))
    proj = _mm(h.reshape(nt, d), w_in[:, :nproj].astype(BF16), tm, _pick(nproj, (2048, 1024, 512, 256, 128)))
    a_neg = -jnp.exp(a_log.astype(F32))
    dtc, csc, csr = _dt_prep(h, w_in[:, nproj:], dt_bias, a_neg, pad)
    nc = lp // SSM_CHUNK
    csr = csr.reshape(bsz, nc, SSM_GROUPS, hpg, SSM_CHUNK).transpose(0, 2, 1, 3, 4)
    d_exp = jnp.repeat(d_skip.astype(F32), SSM_HEAD_DIM)[None, :]
    y = _ssd(proj.reshape(bsz, lp, nproj), dtc, csc, csr, conv_w, conv_b[None, :], d_exp, norm_g[None, :],
             pad, d_inner)
    out = _mm_resid_ln(y.reshape(nt, d_inner), w_out.astype(BF16), h.reshape(nt, d),
                       mix_g[None, :], mix_b[None, :], alpha, _pick(nt, (512, 256, 128)))
    return out.reshape(bsz, lp, d)


def kernel(x, meta_tokens, conv_w1, conv_b1, conv_dw, conv_dwb, conv_ln_g, conv_ln_b, conv_w2, conv_b2,
           ssm_w_in, ssm_conv_w, ssm_conv_b, ssm_dt_bias, ssm_a_log, ssm_d, ssm_norm_g, ssm_w_out,
           moe_w_router, moe_b_router, moe_w1, moe_b1, moe_w2, moe_b2,
           ln_mix_g, ln_mix_b, ln_ffn_g, ln_ffn_b):
    bsz, seq, d = x.shape
    depth = ln_mix_g.shape[0]
    alpha = (2 * depth) ** 0.25
    pad = SSM_CHUNK - N_META
    lp = pad + N_META + seq
    assert lp % SSM_CHUNK == 0
    meta = jnp.broadcast_to(meta_tokens.astype(x.dtype)[None], (bsz, N_META, d))
    h = jnp.concatenate([jnp.zeros((bsz, pad, d), x.dtype), meta, x], axis=1)
    for i in range(depth):
        j = i // 2
        if i % 2 == 0:
            h = _conformer_layer(h, conv_w1[j], conv_b1[j], conv_dw[j], conv_dwb[j], conv_ln_g[j],
                                 conv_ln_b[j], conv_w2[j], conv_b2[j], ln_mix_g[i], ln_mix_b[i], alpha, pad)
        else:
            h = _mamba_layer(h, ssm_w_in[j], ssm_conv_w[j], ssm_conv_b[j], ssm_dt_bias[j], ssm_a_log[j],
                             ssm_d[j], ssm_norm_g[j], ssm_w_out[j], ln_mix_g[i], ln_mix_b[i], alpha, pad)
        t = _moe_layer(h.reshape(bsz * lp, d), i, lp, pad, moe_w_router[i], moe_b_router[i], moe_w1, moe_b1[i],
                       moe_w2, moe_b2[i], ln_ffn_g[i], ln_ffn_b[i], alpha)
        h = t.reshape(bsz, lp, d)
    return h[:, pad + N_META:]
```
